```python
import jax
import jax.numpy as jnp
from jax import lax
import numpy as np

D_MODEL = 2048
BATCH = 1
SEQ = 16384
DEPTH = 2

Q_BLOCK = 128
SWA_HEADS = 8
SWA_KV_HEADS = 2
SWA_HEAD_DIM = 64
SWA_WINDOW = 128
SB_HEADS = 4
SB_HEAD_DIM = 128
NSA_HEADS = 8
NSA_KV_HEADS = 2
NSA_HEAD_DIM = 64
CMP_BLOCK = 32
CMP_STRIDE = 16
CMP_HIDDEN = 256
SEL_BLOCK = 64
SEL_TOPK = 8
NSA_WINDOW = 512
D_FF = -(-8 * D_MODEL // (3 * 256)) * 256

NEG = -1e30
SEL_FORCE = 1e9
RMS_EPS = 1e-6

IN_SPLITS = (
    SWA_HEADS * SWA_HEAD_DIM, SWA_KV_HEADS * SWA_HEAD_DIM, SWA_KV_HEADS * SWA_HEAD_DIM,
    SB_HEADS * SB_HEAD_DIM, SB_HEADS * SB_HEAD_DIM, SB_HEADS * SB_HEAD_DIM,
    NSA_HEADS * NSA_HEAD_DIM,
    NSA_KV_HEADS * NSA_HEAD_DIM, NSA_KV_HEADS * NSA_HEAD_DIM,
    NSA_KV_HEADS * NSA_HEAD_DIM, NSA_KV_HEADS * NSA_HEAD_DIM,
    NSA_KV_HEADS * NSA_HEAD_DIM, NSA_KV_HEADS * NSA_HEAD_DIM,
    3 * NSA_HEADS,
    3 * D_MODEL,
)
N_IN = sum(IN_SPLITS)

kernel_name = "hybrid_swa_stickbreak_nsa_block"


def rms_norm(x, g):
    xf = x.astype(jnp.float32)
    y = xf * lax.rsqrt(jnp.mean(xf * xf, axis=-1, keepdims=True) + RMS_EPS)
    return (y * g.astype(jnp.float32)).astype(x.dtype)


def alibi_slopes(n):
    return jnp.exp2(-8.0 * jnp.arange(1, n + 1, dtype=jnp.float32) / n)


def split_cols(y, sizes):
    out, start = [], 0
    for n in sizes:
        out.append(y[..., start:start + n])
        start += n
    return out


def masked_softmax(s, mask):
    s = jnp.where(mask, s, NEG)
    m = jnp.max(s, axis=-1, keepdims=True)
    p = jnp.where(mask, jnp.exp(s - m), 0.0)
    return p / jnp.maximum(jnp.sum(p, axis=-1, keepdims=True), 1e-30)


def banded_kv(t, n_prev):
    B, S, G, d = t.shape
    nb = S // Q_BLOCK
    tp = jnp.pad(t, ((0, 0), (n_prev * Q_BLOCK, 0), (0, 0), (0, 0)))
    tb = tp.reshape(B, nb + n_prev, Q_BLOCK, G, d)
    return jnp.concatenate([tb[:, i:i + nb] for i in range(n_prev + 1)], axis=2)


def swa_attention(q, k, v, sinks):
    B, S, G, R, d = q.shape
    nb = S // Q_BLOCK
    n_prev = -(-SWA_WINDOW // Q_BLOCK)
    L = (n_prev + 1) * Q_BLOCK
    qb = q.reshape(B, nb, Q_BLOCK, G, R, d)
    kb = banded_kv(k, n_prev)
    vb = banded_kv(v, n_prev)
    s = jnp.einsum('bnqgrd,bnkgd->bngrqk', qb, kb).astype(jnp.float32) * (d ** -0.5)
    i = jnp.arange(Q_BLOCK)
    j = jnp.arange(L)
    dist = i[:, None] - j[None, :] + n_prev * Q_BLOCK
    key_pos = jnp.arange(nb)[:, None] * Q_BLOCK - n_prev * Q_BLOCK + j[None, :]
    mask = ((dist >= 0) & (dist < SWA_WINDOW))[None] & (key_pos >= 0)[:, None, :]
    slopes = alibi_slopes(G * R).reshape(G, R, 1, 1)
    s = s - slopes * dist.astype(jnp.float32)
    s = jnp.where(mask[None, :, None, None], s, NEG)
    sink = sinks.astype(jnp.float32).reshape(1, 1, G, R, 1, 1)
    m = jnp.maximum(jnp.max(s, axis=-1, keepdims=True), sink)
    p = jnp.exp(s - m)
    p = p / (jnp.sum(p, axis=-1, keepdims=True) + jnp.exp(sink - m))
    o = jnp.einsum('bngrqk,bnkgd->bnqgrd', p.astype(v.dtype), vb)
    return o.reshape(B, S, G * R * d)


def stick_breaking(q, k, v):
    B, S, H, d = q.shape
    f32 = jnp.float32
    nb = S // Q_BLOCK
    n_pair = (nb + 1) // 2
    scale = d ** -0.5
    to_blocks = lambda t: t.reshape(B, nb, Q_BLOCK, H, d).transpose(1, 0, 3, 2, 4)
    qb, kb, vb = to_blocks(q), to_blocks(k), to_blocks(v)
    i = jnp.arange(Q_BLOCK)
    later = (i[:, None] > i[None, :]).astype(f32)

    def block_pair(qi, ki, ls, acc):
        qt = lax.dynamic_index_in_dim(qb, qi, 0, keepdims=False)
        kt = lax.dynamic_index_in_dim(kb, ki, 0, keepdims=False)
        vt = lax.dynamic_index_in_dim(vb, ki, 0, keepdims=False)
        z = jnp.einsum('bhqd,bhkd->bhqk', qt, kt).astype(f32) * scale
        mask = (ki * Q_BLOCK + i)[None, :] < (qi * Q_BLOCK + i)[:, None]
        log_fail = jnp.where(mask, jax.nn.log_sigmoid(-z), 0.0)
        log_surv = ls[..., None] + log_fail @ later
        a = jnp.where(mask, jnp.exp(jax.nn.log_sigmoid(z) + log_surv), 0.0)
        acc = acc + jnp.einsum('bhqk,bhkd->bhqd', a.astype(vt.dtype), vt).astype(f32)
        return ls + jnp.sum(log_fail, axis=-1), acc

    def pair(p):
        q_a = p
        q_b = nb - 1 - p

        def step(carry, j):
            ls_a, acc_a, ls_b, acc_b = carry
            is_a = j <= p
            qi = jnp.where(is_a, q_a, q_b)
            ki = jnp.where(is_a, p - j, nb - j)
            ls, acc = block_pair(qi, ki, jnp.where(is_a, ls_a, ls_b), jnp.where(is_a, acc_a, acc_b))
            new = (jnp.where(is_a, ls, ls_a), jnp.where(is_a, acc, acc_a),
                   jnp.where(is_a, ls_b, ls), jnp.where(is_a, acc_b, acc))
            return new, None

        ls0 = jnp.zeros((B, H, Q_BLOCK), f32)
        acc0 = jnp.zeros((B, H, Q_BLOCK, d), f32)
        (_, acc_a, _, acc_b), _ = lax.scan(step, (ls0, acc0, ls0, acc0), jnp.arange(nb + 1))
        return acc_a, acc_b

    out_a, out_b = lax.map(pair, jnp.arange(n_pair))
    o = jnp.concatenate([out_a, out_b[::-1][2 * n_pair - nb:]], axis=0)
    return o.transpose(1, 0, 3, 2, 4).reshape(B, S, H * d).astype(q.dtype)


def compress_blocks(t, pe, w1, w2):
    B, S, G, d = t.shape
    n_cmp = (S - CMP_BLOCK) // CMP_STRIDE + 1
    idx = jnp.arange(n_cmp)[:, None] * CMP_STRIDE + jnp.arange(CMP_BLOCK)[None, :]
    blk = t[:, idx] + pe[:, None, :]
    blk = blk.transpose(0, 1, 3, 2, 4).reshape(B, n_cmp, G, CMP_BLOCK * d)
    return jax.nn.gelu(blk @ w1) @ w2


def gather_rows(src, idx):
    return jax.vmap(jax.vmap(lambda a, i: a[i]))(src, idx)


def nsa_attention(q, gates, k_c, v_c, k_s, v_s, k_w, v_w,
                  pe_k, w1_k, w2_k, pe_v, w1_v, w2_v):
    B, S, G, R, d = q.shape
    dt = q.dtype
    nb = S // Q_BLOCK
    scale = d ** -0.5
    slopes = alibi_slopes(G * R).reshape(G, R, 1, 1)

    kc = compress_blocks(k_c, pe_k, w1_k, w2_k)
    vc = compress_blocks(v_c, pe_v, w1_v, w2_v)
    n_cmp = kc.shape[1]
    cmp_lo = jnp.arange(n_cmp) * CMP_STRIDE
    cmp_end = cmp_lo + CMP_BLOCK - 1

    n_sel = S // SEL_BLOCK
    n_top = min(SEL_TOPK, n_sel)
    sel_idx = jnp.arange(n_sel)
    sel_lo = sel_idx * SEL_BLOCK
    overlap = ((cmp_lo[:, None] <= sel_lo[None, :] + SEL_BLOCK - 1)
               & (cmp_end[:, None] >= sel_lo[None, :])).astype(jnp.float32)
    kst = k_s.transpose(0, 2, 1, 3)
    vst = v_s.transpose(0, 2, 1, 3)

    Lw = NSA_WINDOW + Q_BLOCK
    kwp = jnp.pad(k_w, ((0, 0), (NSA_WINDOW, 0), (0, 0), (0, 0)))
    vwp = jnp.pad(v_w, ((0, 0), (NSA_WINDOW, 0), (0, 0), (0, 0)))

    def block(args):
        qblk, gblk, bidx = args
        t = bidx * Q_BLOCK + jnp.arange(Q_BLOCK)
        dist_c = t[:, None] - cmp_end[None, :]
        sc = jnp.einsum('bqgrd,bngd->bgrqn', qblk, kc).astype(jnp.float32) * scale
        p_c = masked_softmax(sc - slopes * dist_c.astype(jnp.float32), dist_c >= 0)
        o_c = jnp.einsum('bgrqn,bngd->bqgrd', p_c.astype(dt), vc)
        imp = jnp.sum(p_c, axis=2) @ overlap
        jt = t // SEL_BLOCK
        valid = sel_lo[None, :] <= t[:, None]
        forced = ((sel_idx[None, :] == 0) | (sel_idx[None, :] == jt[:, None])
                  | (sel_idx[None, :] == jt[:, None] - 1))
        score = jnp.where(valid, jnp.where(forced, SEL_FORCE, imp), -SEL_FORCE)
        _, sel = lax.top_k(score, n_top)
        tok = (sel[..., None] * SEL_BLOCK + jnp.arange(SEL_BLOCK)).reshape(
            B, G, Q_BLOCK, n_top * SEL_BLOCK)
        ks = gather_rows(kst, tok)
        vs = gather_rows(vst, tok)
        dist_s = t[None, None, :, None] - tok
        ss = jnp.einsum('bqgrd,bgqnd->bgrqn', qblk, ks).astype(jnp.float32) * scale
        ss = ss - slopes * dist_s[:, :, None].astype(jnp.float32)
        p_s = masked_softmax(ss, (dist_s >= 0)[:, :, None])
        o_s = jnp.einsum('bgrqn,bgqnd->bqgrd', p_s.astype(dt), vs)
        kw = lax.dynamic_slice_in_dim(kwp, bidx * Q_BLOCK, Lw, axis=1)
        vw = lax.dynamic_slice_in_dim(vwp, bidx * Q_BLOCK, Lw, axis=1)
        key_pos = bidx * Q_BLOCK - NSA_WINDOW + jnp.arange(Lw)
        dist_w = t[:, None] - key_pos[None, :]
        mask_w = (dist_w >= 0) & (dist_w < NSA_WINDOW) & (key_pos >= 0)[None, :]
        sw = jnp.einsum('bqgrd,bkgd->bgrqk', qblk, kw).astype(jnp.float32) * scale
        p_w = masked_softmax(sw - slopes * dist_w.astype(jnp.float32), mask_w)
        o_w = jnp.einsum('bgrqk,bkgd->bqgrd', p_w.astype(dt), vw)
        g = jax.nn.sigmoid(gblk.astype(jnp.float32))
        o = g[..., 0:1] * o_c + g[..., 1:2] * o_s + g[..., 2:3] * o_w
        return o.astype(dt)

    qb = jnp.moveaxis(q.reshape(B, nb, Q_BLOCK, G, R, d), 1, 0)
    gb = jnp.moveaxis(gates.reshape(B, nb, Q_BLOCK, G, R, 3), 1, 0)
    o = lax.map(block, (qb, gb, jnp.arange(nb)))
    return jnp.moveaxis(o, 0, 1).reshape(B, S, G * R * d)


def hybrid_mixer(h, w_in, sinks, pe_k, w1_k, w2_k, pe_v, w1_v, w2_v,
                 w_br_a, w_br_b, w_br_c, w_out):
    B, S, _ = h.shape
    (qa, ka, va, qb, kb, vb, qc, kc, vc, ksl, vsl, kwn, vwn, gc, gates) = split_cols(
        h @ w_in, IN_SPLITS)
    ra = SWA_HEADS // SWA_KV_HEADS
    rc = NSA_HEADS // NSA_KV_HEADS
    kv_a = lambda t: t.reshape(B, S, SWA_KV_HEADS, SWA_HEAD_DIM)
    kv_c = lambda t: t.reshape(B, S, NSA_KV_HEADS, NSA_HEAD_DIM)
    sb = lambda t: t.reshape(B, S, SB_HEADS, SB_HEAD_DIM)
    y_a = swa_attention(qa.reshape(B, S, SWA_KV_HEADS, ra, SWA_HEAD_DIM), kv_a(ka), kv_a(va), sinks)
    y_b = stick_breaking(sb(qb), sb(kb), sb(vb))
    y_c = nsa_attention(qc.reshape(B, S, NSA_KV_HEADS, rc, NSA_HEAD_DIM),
                        gc.reshape(B, S, NSA_KV_HEADS, rc, 3),
                        kv_c(kc), kv_c(vc), kv_c(ksl), kv_c(vsl), kv_c(kwn), kv_c(vwn),
                        pe_k, w1_k, w2_k, pe_v, w1_v, w2_v)
    g_a, g_b, g_c = jnp.split(gates, 3, axis=-1)
    merged = (jax.nn.sigmoid(g_a) * (y_a @ w_br_a)
              + jax.nn.sigmoid(g_b) * (y_b @ w_br_b)
              + jax.nn.sigmoid(g_c) * (y_c @ w_br_c))
    return merged @ w_out


def swiglu(h, w_gate, w_up, w_down):
    return (jax.nn.silu(h @ w_gate) * (h @ w_up)) @ w_down


def setup_inputs(seed: int = 0) -> dict:
    key = jax.random.key(seed)
    ks = jax.random.split(key, 20)
    f32 = jnp.float32

    def nrm(k, shape, fan_in):
        return jax.random.normal(k, shape, f32) * (fan_in ** -0.5)

    def gain(k, shape):
        return 1.0 + 0.02 * jax.random.normal(k, shape, f32)

    cmp_in = CMP_BLOCK * NSA_HEAD_DIM
    w_a = SWA_HEADS * SWA_HEAD_DIM
    w_b = SB_HEADS * SB_HEAD_DIM
    w_c = NSA_HEADS * NSA_HEAD_DIM
    return {
        "x": jax.random.normal(ks[0], (BATCH, SEQ, D_MODEL), f32),
        "norm_mix": gain(ks[1], (DEPTH, D_MODEL)),
        "w_in": nrm(ks[2], (DEPTH, D_MODEL, N_IN), D_MODEL),
        "swa_sinks": 0.5 * jax.random.normal(ks[3], (DEPTH, SWA_HEADS), f32),
        "cmp_pe_k": 0.1 * jax.random.normal(ks[4], (DEPTH, CMP_BLOCK, NSA_HEAD_DIM), f32),
        "cmp_w1_k": nrm(ks[5], (DEPTH, cmp_in, CMP_HIDDEN), cmp_in),
        "cmp_w2_k": nrm(ks[6], (DEPTH, CMP_HIDDEN, NSA_HEAD_DIM), CMP_HIDDEN),
        "cmp_pe_v": 0.1 * jax.random.normal(ks[7], (DEPTH, CMP_BLOCK, NSA_HEAD_DIM), f32),
        "cmp_w1_v": nrm(ks[8], (DEPTH, cmp_in, CMP_HIDDEN), cmp_in),
        "cmp_w2_v": nrm(ks[9], (DEPTH, CMP_HIDDEN, NSA_HEAD_DIM), CMP_HIDDEN),
        "w_branch_swa": nrm(ks[10], (DEPTH, w_a, D_MODEL), w_a),
        "w_branch_sb": nrm(ks[11], (DEPTH, w_b, D_MODEL), w_b),
        "w_branch_nsa": nrm(ks[12], (DEPTH, w_c, D_MODEL), w_c),
        "w_out": nrm(ks[13], (DEPTH, D_MODEL, D_MODEL), D_MODEL),
        "norm_ffn": gain(ks[14], (DEPTH, D_MODEL)),
        "w_gate": nrm(ks[15], (DEPTH, D_MODEL, D_FF), D_MODEL),
        "w_up": nrm(ks[16], (DEPTH, D_MODEL, D_FF), D_MODEL),
        "w_down": nrm(ks[17], (DEPTH, D_FF, D_MODEL), D_FF),
        "norm_final": gain(ks[18], (D_MODEL,)),
    }


def reference(x, norm_mix, w_in, swa_sinks, cmp_pe_k, cmp_w1_k, cmp_w2_k,
              cmp_pe_v, cmp_w1_v, cmp_w2_v, w_branch_swa, w_branch_sb, w_branch_nsa,
              w_out, norm_ffn, w_gate, w_up, w_down, norm_final):
    for layer in range(DEPTH):
        h = rms_norm(x, norm_mix[layer])
        x = x + hybrid_mixer(h, w_in[layer], swa_sinks[layer],
                             cmp_pe_k[layer], cmp_w1_k[layer], cmp_w2_k[layer],
                             cmp_pe_v[layer], cmp_w1_v[layer], cmp_w2_v[layer],
                             w_branch_swa[layer], w_branch_sb[layer], w_branch_nsa[layer],
                             w_out[layer])
        h = rms_norm(x, norm_ffn[layer])
        x = x + swiglu(h, w_gate[layer], w_up[layer], w_down[layer])
    return rms_norm(x, norm_final)
```

```python
import functools

import jax
import jax.numpy as jnp
import numpy as np
from jax import lax
from jax.experimental import pallas as pl
from jax.experimental.pallas import tpu as pltpu

F32 = jnp.float32
BF16 = jnp.bfloat16

D_MODEL = 2048
Q_BLOCK = 128
LANES = 128
HALF = 64
N_PAIR = 4
SWA_WINDOW = 128
NSA_WINDOW = 512
CMP_BLOCK = 32
CMP_STRIDE = 16
CMP_HIDDEN = 256
SEL_BLOCK = 64
SEL_TOPK = 8
SB_HEADS = 4
D_FF = 5632
NEG = -1e30
SEL_FORCE = 1e9
SEL_TAKEN = -3e38
RMS_EPS = 1e-6
VMEM_LIMIT = 56 * 1024 * 1024

N_ATT = 3584
N_GC = 24
N_GATES = 3 * D_MODEL
N_REST = N_GATES + LANES
COL_QA, COL_QC, COL_QB, COL_KB, COL_VB = 0, 4, 8, 12, 16
COL_KA, COL_VA, COL_KC, COL_VC, COL_KS, COL_VS, COL_KW, COL_VW = 20, 21, 22, 23, 24, 25, 26, 27


def _nt(a, b):
    return lax.dot_general(a, b, (((1,), (1,)), ((), ())), preferred_element_type=F32)


def _dot(a, b):
    return jnp.dot(a, b, preferred_element_type=F32)


def _params(*sem):
    return pltpu.CompilerParams(dimension_semantics=sem, vmem_limit_bytes=VMEM_LIMIT)


def _head_slope(pair, group):
    return 2.0 ** -(group * N_PAIR + pair + 1)


def _norm_mm_kernel(x_ref, g_ref, w_ref, o_ref, h_ref):
    @pl.when(pl.program_id(1) == 0)
    def _():
        x = x_ref[...]
        ms = jnp.mean(x * x, axis=-1, keepdims=True)
        h_ref[...] = (x * lax.rsqrt(ms + RMS_EPS) * g_ref[...]).astype(BF16)

    o_ref[...] = _dot(h_ref[...], w_ref[...]).astype(o_ref.dtype)


def norm_matmul(x, g, w, out_dtype, tm, tn):
    s, d = x.shape
    n = w.shape[1]
    return pl.pallas_call(
        _norm_mm_kernel,
        grid=(s // tm, n // tn),
        in_specs=[pl.BlockSpec((tm, d), lambda i, j: (i, 0)),
                  pl.BlockSpec((1, d), lambda i, j: (0, 0)),
                  pl.BlockSpec((d, tn), lambda i, j: (0, j))],
        out_specs=pl.BlockSpec((tm, tn), lambda i, j: (i, j)),
        out_shape=jax.ShapeDtypeStruct((s, n), out_dtype),
        scratch_shapes=[pltpu.VMEM((tm, d), BF16)],
        compiler_params=_params("parallel", "arbitrary"),
        name="norm_matmul",
    )(x, g.reshape(1, d), w)


def _mm_res_kernel(a_ref, w_ref, r_ref, o_ref):
    o_ref[...] = r_ref[...] + _dot(a_ref[...], w_ref[...])


def matmul_residual(a, w, res, tm, tn):
    s, k = a.shape
    n = w.shape[1]
    return pl.pallas_call(
        _mm_res_kernel,
        grid=(s // tm, n // tn),
        in_specs=[pl.BlockSpec((tm, k), lambda i, j: (i, 0)),
                  pl.BlockSpec((k, tn), lambda i, j: (0, j)),
                  pl.BlockSpec((tm, tn), lambda i, j: (i, j))],
        out_specs=pl.BlockSpec((tm, tn), lambda i, j: (i, j)),
        out_shape=jax.ShapeDtypeStruct((s, n), F32),
        compiler_params=_params("parallel", "arbitrary"),
        name="matmul_residual",
    )(a, w, res)


def _merge_kernel(ya_ref, yb_ref, yc_ref, wa_ref, wb_ref, wc_ref,
                  ga_ref, gb_ref, gc_ref, o_ref):
    m = jax.nn.sigmoid(ga_ref[...]) * _dot(ya_ref[...], wa_ref[...])
    m = m + jax.nn.sigmoid(gb_ref[...]) * _dot(yb_ref[...], wb_ref[...])
    m = m + jax.nn.sigmoid(gc_ref[...]) * _dot(yc_ref[...], wc_ref[...])
    o_ref[...] = m.astype(o_ref.dtype)


def merge_branches(ya, yb, yc, wa, wb, wc, y_rest, tm, tn):
    s, k = ya.shape
    n = wa.shape[1]
    nj = n // tn
    y_spec = pl.BlockSpec((tm, k), lambda i, j: (i, 0))
    w_spec = pl.BlockSpec((k, tn), lambda i, j: (0, j))

    def gate_spec(br):
        return pl.BlockSpec((tm, tn), lambda i, j: (i, br * nj + j))

    return pl.pallas_call(
        _merge_kernel,
        grid=(s // tm, nj),
        in_specs=[y_spec, y_spec, y_spec, w_spec, w_spec, w_spec,
                  gate_spec(0), gate_spec(1), gate_spec(2)],
        out_specs=pl.BlockSpec((tm, tn), lambda i, j: (i, j)),
        out_shape=jax.ShapeDtypeStruct((s, n), BF16),
        compiler_params=_params("parallel", "arbitrary"),
        name="merge_branches",
    )(ya, yb, yc, wa, wb, wc, y_rest, y_rest, y_rest)


def _ffn_up_kernel(x_ref, g_ref, wg_ref, wu_ref, o_ref, h_ref):
    @pl.when(pl.program_id(1) == 0)
    def _():
        x = x_ref[...]
        ms = jnp.mean(x * x, axis=-1, keepdims=True)
        h_ref[...] = (x * lax.rsqrt(ms + RMS_EPS) * g_ref[...]).astype(BF16)

    h = h_ref[...]
    o_ref[...] = (jax.nn.silu(_dot(h, wg_ref[...])) * _dot(h, wu_ref[...])).astype(o_ref.dtype)


def ffn_up(x, g, wg, wu, tm, tn):
    s, d = x.shape
    n = wg.shape[1]
    w_spec = pl.BlockSpec((d, tn), lambda i, j: (0, j))
    return pl.pallas_call(
        _ffn_up_kernel,
        grid=(s // tm, n // tn),
        in_specs=[pl.BlockSpec((tm, d), lambda i, j: (i, 0)),
                  pl.BlockSpec((1, d), lambda i, j: (0, 0)),
                  w_spec, w_spec],
        out_specs=pl.BlockSpec((tm, tn), lambda i, j: (i, j)),
        out_shape=jax.ShapeDtypeStruct((s, n), BF16),
        scratch_shapes=[pltpu.VMEM((tm, d), BF16)],
        compiler_params=_params("parallel", "arbitrary"),
        name="ffn_up",
    )(x, g.reshape(1, d), wg, wu)


def _rms_kernel(x_ref, g_ref, o_ref):
    x = x_ref[...]
    ms = jnp.mean(x * x, axis=-1, keepdims=True)
    o_ref[...] = x * lax.rsqrt(ms + RMS_EPS) * g_ref[...]


def rms_norm_rows(x, g, tm):
    s, d = x.shape
    return pl.pallas_call(
        _rms_kernel,
        grid=(s // tm,),
        in_specs=[pl.BlockSpec((tm, d), lambda i: (i, 0)),
                  pl.BlockSpec((1, d), lambda i: (0, 0))],
        out_specs=pl.BlockSpec((tm, d), lambda i: (i, 0)),
        out_shape=jax.ShapeDtypeStruct((s, d), F32),
        compiler_params=_params("parallel"),
        name="rms_norm_rows",
    )(x, g.reshape(1, d))


def _half_masks():
    lane = lax.broadcasted_iota(jnp.int32, (Q_BLOCK, LANES), 1)
    return lane < HALF


def _banded_kernel(*refs, n_prev, window, has_sink, scale):
    nk = n_prev + 1
    if has_sink:
        sink_ref, refs = refs[0], refs[1:]
    q_ref = refs[0]
    k_refs = refs[1:1 + nk]
    v_refs = refs[1 + nk:1 + 2 * nk]
    o_ref = refs[1 + 2 * nk]
    i = pl.program_id(0)
    span = nk * Q_BLOCK
    k_all = jnp.concatenate([k_refs[d][...] for d in range(n_prev, -1, -1)], axis=0)
    vt_all = jnp.concatenate([v_refs[d][0] for d in range(n_prev, -1, -1)], axis=1)
    key_rel = lax.broadcasted_iota(jnp.int32, (span, Q_BLOCK), 0)
    q_rel = lax.broadcasted_iota(jnp.int32, (span, Q_BLOCK), 1)
    dist = q_rel + n_prev * Q_BLOCK - key_rel
    key_pos = (i - n_prev) * Q_BLOCK + key_rel
    mask = (dist >= 0) & (dist < window) & (key_pos >= 0)
    distf = dist.astype(F32)
    lo = _half_masks()
    top = lax.broadcasted_iota(jnp.int32, (Q_BLOCK, Q_BLOCK), 0) < HALF
    for pair in range(N_PAIR):
        q128 = q_ref[:, pair * LANES:(pair + 1) * LANES]
        halves = []
        for group in range(2):
            qm = jnp.where(lo if group == 0 else jnp.logical_not(lo), q128, jnp.zeros_like(q128))
            s = _nt(k_all, qm) * scale - _head_slope(pair, group) * distf
            s = jnp.where(mask, s, NEG)
            m = jnp.max(s, axis=0, keepdims=True)
            if has_sink:
                sink = sink_ref[group * N_PAIR + pair]
                m = jnp.maximum(m, sink)
                p = jnp.exp(s - m)
                denom = jnp.sum(p, axis=0, keepdims=True) + jnp.exp(sink - m)
            else:
                p = jnp.where(mask, jnp.exp(s - m), 0.0)
                denom = jnp.maximum(jnp.sum(p, axis=0, keepdims=True), 1e-30)
            pv = _dot(vt_all, p.astype(BF16))
            halves.append(pv * (1.0 / denom))
        ot = jnp.where(top, halves[0], halves[1])
        o_ref[:, pair * LANES:(pair + 1) * LANES] = ot.T.astype(o_ref.dtype)


def banded_attention(y_att, vt3, col_q, col_k, window, sinks, out_dtype):
    s = y_att.shape[0]
    nb = s // Q_BLOCK
    n_prev = -(-window // Q_BLOCK)
    has_sink = sinks is not None
    in_specs = []
    args = []
    if has_sink:
        in_specs.append(pl.BlockSpec(memory_space=pltpu.SMEM))
        args.append(sinks)
    in_specs.append(pl.BlockSpec((Q_BLOCK, N_PAIR * LANES), lambda i: (i, col_q // N_PAIR)))
    args.append(y_att)
    for d in range(n_prev + 1):
        in_specs.append(pl.BlockSpec((Q_BLOCK, LANES),
                                     lambda i, d=d: (jnp.maximum(i - d, 0), col_k)))
        args.append(y_att)
    for d in range(n_prev + 1):
        in_specs.append(pl.BlockSpec((1, LANES, Q_BLOCK),
                                     lambda i, d=d: (jnp.maximum(i - d, 0), 0, 0)))
        args.append(vt3)
    kern = functools.partial(_banded_kernel, n_prev=n_prev, window=window,
                             has_sink=has_sink, scale=HALF ** -0.5)
    return pl.pallas_call(
        kern,
        grid=(nb,),
        in_specs=in_specs,
        out_specs=pl.BlockSpec((Q_BLOCK, N_PAIR * LANES), lambda i: (i, 0)),
        out_shape=jax.ShapeDtypeStruct((s, N_PAIR * LANES), out_dtype),
        compiler_params=_params("parallel"),
        name="banded_attention_w%d" % window,
    )(*args)


def _sb_kernel(q_ref, k_ref, vt_ref, o_ref, *, scale):
    i = pl.program_id(1)
    q = q_ref[...]
    row = lax.broadcasted_iota(jnp.int32, (Q_BLOCK, Q_BLOCK), 0)
    col = lax.broadcasted_iota(jnp.int32, (Q_BLOCK, Q_BLOCK), 1)
    later = jnp.where(col > row, 1.0, 0.0).astype(BF16)
    before = row < col

    def visit(kj, ls, acc, diagonal):
        z = _nt(k_ref[kj], q) * scale
        lf_raw = -(jnp.maximum(z, 0.0) + jnp.log1p(jnp.exp(-jnp.abs(z))))
        lf = jnp.where(before, lf_raw, 0.0) if diagonal else lf_raw
        log_surv = ls + _dot(later, lf.astype(BF16))
        a = jnp.exp(z + lf_raw + log_surv)
        if diagonal:
            a = jnp.where(before, a, 0.0)
        acc = acc + _dot(vt_ref[kj], a.astype(BF16))
        return ls + jnp.sum(lf, axis=0, keepdims=True), acc

    ls0 = jnp.zeros((1, Q_BLOCK), F32)
    acc0 = jnp.zeros((Q_BLOCK, Q_BLOCK), F32)
    ls, acc = visit(i, ls0, acc0, True)

    def body(t, carry):
        return visit(i - 1 - t, carry[0], carry[1], False)

    ls, acc = lax.fori_loop(0, i, body, (ls, acc))
    o_ref[...] = acc.T.astype(o_ref.dtype)


def stick_breaking(y_att, k4, vt4):
    s = y_att.shape[0]
    nb = s // Q_BLOCK
    kern = functools.partial(_sb_kernel, scale=LANES ** -0.5)
    kv_spec = pl.BlockSpec((None, nb, Q_BLOCK, LANES), lambda h, i: (h, 0, 0, 0))
    return pl.pallas_call(
        kern,
        grid=(SB_HEADS, nb),
        in_specs=[pl.BlockSpec((Q_BLOCK, LANES), lambda h, i: (i, COL_QB + h)),
                  kv_spec, kv_spec],
        out_specs=pl.BlockSpec((Q_BLOCK, LANES), lambda h, i: (i, h)),
        out_shape=jax.ShapeDtypeStruct((s, SB_HEADS * LANES), BF16),
        compiler_params=_params("parallel", "arbitrary"),
        name="stick_breaking",
    )(y_att, k4, vt4)


def _compress_kernel(t_ref, pe_ref, w1_ref, w2_ref, o_ref):
    t = t_ref[...]
    half = w1_ref.shape[0] // 2
    w1 = w1_ref[...]
    a = _dot(t, w1[:half])
    b = _dot(t, w1[half:])
    bias = _dot(pe_ref[...], w1)[0:1]
    n = t.shape[0]
    pre = a + pltpu.roll(b, n - 1, 0) + bias
    hid = jax.nn.gelu(pre)
    o_ref[...] = _dot(hid.astype(BF16), w2_ref[...]).astype(o_ref.dtype)


def compress_blocks(t_flat, pe, w1, w2):
    g, n, k = t_flat.shape
    pe_rows = jnp.zeros((8, 2 * k), BF16).at[0].set(pe.reshape(-1).astype(BF16))
    return pl.pallas_call(
        _compress_kernel,
        grid=(g,),
        in_specs=[pl.BlockSpec((None, n, k), lambda gi: (gi, 0, 0)),
                  pl.BlockSpec((8, 2 * k), lambda gi: (0, 0)),
                  pl.BlockSpec((2 * k, CMP_HIDDEN), lambda gi: (0, 0)),
                  pl.BlockSpec((CMP_HIDDEN, HALF), lambda gi: (0, 0))],
        out_specs=pl.BlockSpec((None, n, HALF), lambda gi: (gi, 0, 0)),
        out_shape=jax.ShapeDtypeStruct((g, n, HALF), BF16),
        compiler_params=_params("parallel"),
        name="compress_blocks",
    )(t_flat, pe_rows, w1.astype(BF16), w2.astype(BF16))


def _nsa_cmp_kernel(q_ref, kc_ref, vct_ref, ov_ref, oc_ref, sel_ref, cnt_ref, *, scale):
    i = pl.program_id(0)
    n_cmp = kc_ref.shape[0]
    n_sel = ov_ref.shape[0]
    kc = kc_ref[...]
    vct = vct_ref[...]
    t = i * Q_BLOCK + lax.broadcasted_iota(jnp.int32, (n_cmp, Q_BLOCK), 1)
    cmp_end = lax.broadcasted_iota(jnp.int32, (n_cmp, Q_BLOCK), 0) * CMP_STRIDE + (CMP_BLOCK - 1)
    dist = t - cmp_end
    mask = dist >= 0
    distf = dist.astype(F32)
    lo = _half_masks()
    top = lax.broadcasted_iota(jnp.int32, (Q_BLOCK, Q_BLOCK), 0) < HALF
    psum = [jnp.zeros((n_cmp, Q_BLOCK), F32), jnp.zeros((n_cmp, Q_BLOCK), F32)]
    for pair in range(N_PAIR):
        q128 = q_ref[:, pair * LANES:(pair + 1) * LANES]
        halves = []
        for group in range(2):
            qm = jnp.where(lo if group == 0 else jnp.logical_not(lo), q128, jnp.zeros_like(q128))
            s = _nt(kc, qm) * scale - _head_slope(pair, group) * distf
            s = jnp.where(mask, s, NEG)
            m = jnp.max(s, axis=0, keepdims=True)
            p = jnp.where(mask, jnp.exp(s - m), 0.0)
            p = p * (1.0 / jnp.maximum(jnp.sum(p, axis=0, keepdims=True), 1e-30))
            halves.append(_dot(vct, p.astype(BF16)))
            psum[group] = psum[group] + p
        ot = jnp.where(top, halves[0], halves[1])
        oc_ref[:, pair * LANES:(pair + 1) * LANES] = ot.T

    ov = ov_ref[...]
    tq = i * Q_BLOCK + lax.broadcasted_iota(jnp.int32, (n_sel, Q_BLOCK), 1)
    jblk = lax.broadcasted_iota(jnp.int32, (n_sel, Q_BLOCK), 0)
    jblk_f = jblk.astype(F32)
    jt = tq // SEL_BLOCK
    valid = jblk * SEL_BLOCK <= tq
    forced = (jblk == 0) | (jblk == jt) | (jblk == jt - 1)
    total = jnp.zeros((n_sel, Q_BLOCK), F32)
    for group in range(2):
        hi = psum[group].astype(BF16)
        lo_part = (psum[group] - hi.astype(F32)).astype(BF16)
        imp = _dot(ov, hi) + _dot(ov, lo_part)
        score = jnp.where(valid, jnp.where(forced, SEL_FORCE, imp), -SEL_FORCE)
        sel = jnp.zeros((n_sel, Q_BLOCK), F32)
        for _ in range(min(SEL_TOPK, n_sel)):
            best = jnp.max(score, axis=0, keepdims=True)
            first = jnp.min(jnp.where(score == best, jblk_f, float(n_sel)), axis=0, keepdims=True)
            hit = jblk_f == first
            sel = jnp.where(hit, 1.0, sel)
            score = jnp.where(hit, SEL_TAKEN, score)
        sel_ref[0, group] = sel
        total = total + sel
    cnt_ref[0] = _nt(jnp.ones((8, Q_BLOCK), BF16), total.astype(BF16))


def nsa_compressed(y_att, kc128, vct128, overlap_t):
    s = y_att.shape[0]
    nb = s // Q_BLOCK
    n_cmp = kc128.shape[0]
    n_sel = overlap_t.shape[0]
    kern = functools.partial(_nsa_cmp_kernel, scale=HALF ** -0.5)
    return pl.pallas_call(
        kern,
        grid=(nb,),
        in_specs=[pl.BlockSpec((Q_BLOCK, N_PAIR * LANES), lambda i: (i, COL_QC // N_PAIR)),
                  pl.BlockSpec((n_cmp, LANES), lambda i: (0, 0)),
                  pl.BlockSpec((LANES, n_cmp), lambda i: (0, 0)),
                  pl.BlockSpec((n_sel, n_cmp), lambda i: (0, 0))],
        out_specs=[pl.BlockSpec((Q_BLOCK, N_PAIR * LANES), lambda i: (i, 0)),
                   pl.BlockSpec((1, 2, n_sel, Q_BLOCK), lambda i: (i, 0, 0, 0)),
                   pl.BlockSpec((1, 8, n_sel), lambda i: (i, 0, 0))],
        out_shape=[jax.ShapeDtypeStruct((s, N_PAIR * LANES), F32),
                   jax.ShapeDtypeStruct((nb, 2, n_sel, Q_BLOCK), F32),
                   jax.ShapeDtypeStruct((nb, 8, n_sel), F32)],
        compiler_params=_params("parallel"),
        name="nsa_compressed",
    )(y_att, kc128, vct128, overlap_t)


def _nsa_sel_kernel(flags_ref, q_ref, ks_ref, vst_ref, sel_ref, gc_ref, oc_ref, ow_ref,
                    o_ref, qm_ref, m_ref, l_ref, acc_ref, *, scale, n_sel):
    i = pl.program_id(0)
    lo = _half_masks()
    for pair in range(N_PAIR):
        q128 = q_ref[:, pair * LANES:(pair + 1) * LANES]
        for group in range(2):
            qm_ref[pair * 2 + group] = jnp.where(
                lo if group == 0 else jnp.logical_not(lo), q128, jnp.zeros_like(q128))
    m_ref[...] = jnp.full(m_ref.shape, NEG, F32)
    l_ref[...] = jnp.zeros(l_ref.shape, F32)
    acc_ref[...] = jnp.zeros(acc_ref.shape, F32)
    top = lax.broadcasted_iota(jnp.int32, (Q_BLOCK, Q_BLOCK), 0) < HALF

    def body(p, carry):
        base = i * n_sel + 2 * p
        active = (flags_ref[base] + flags_ref[base + 1]) > 0

        @pl.when(active)
        def _():
            k = ks_ref[p]
            vt = vst_ref[p]
            key_pos = p * Q_BLOCK + lax.broadcasted_iota(jnp.int32, (Q_BLOCK, Q_BLOCK), 0)
            t = i * Q_BLOCK + lax.broadcasted_iota(jnp.int32, (Q_BLOCK, Q_BLOCK), 1)
            dist = t - key_pos
            distf = dist.astype(F32)
            masks = []
            for group in range(2):
                r0 = sel_ref[0, group, pl.ds(2 * p, 1), :]
                r1 = sel_ref[0, group, pl.ds(2 * p + 1, 1), :]
                picked = jnp.where(top, r0, r1)
                masks.append((picked > 0.5) & (dist >= 0))
            for pair in range(N_PAIR):
                for group in range(2):
                    h = pair * 2 + group
                    mask = masks[group]
                    s = _nt(k, qm_ref[h]) * scale - _head_slope(pair, group) * distf
                    s = jnp.where(mask, s, NEG)
                    m_old = m_ref[h]
                    m_new = jnp.maximum(m_old, jnp.max(s, axis=0, keepdims=True))
                    alpha = jnp.exp(m_old - m_new)
                    pr = jnp.where(mask, jnp.exp(s - m_new), 0.0)
                    l_ref[h] = alpha * l_ref[h] + jnp.sum(pr, axis=0, keepdims=True)
                    pv = _dot(vt, pr.astype(BF16))
                    acc_ref[h] = alpha * acc_ref[h] + pv[group * HALF:(group + 1) * HALF]
                    m_ref[h] = m_new

        return carry

    lax.fori_loop(0, i + 1, body, 0)

    gates = jax.nn.sigmoid(gc_ref[...])
    for pair in range(N_PAIR):
        parts = []
        for group in range(2):
            h = pair * 2 + group
            parts.append(acc_ref[h] * (1.0 / jnp.maximum(l_ref[h], 1e-30)))
        o_s = jnp.concatenate(parts, axis=0).T
        cols = slice(pair * LANES, (pair + 1) * LANES)

        def gate(branch):
            c = branch * 8 + pair * 2
            return jnp.where(lo, gates[:, c:c + 1], gates[:, c + 1:c + 2])

        y = gate(0) * oc_ref[:, cols] + gate(1) * o_s + gate(2) * ow_ref[:, cols]
        o_ref[:, cols] = y.astype(o_ref.dtype)


def nsa_selected(flags, y_att, ks3, vst3, sel_t, y_rest, o_c, o_w):
    s = y_att.shape[0]
    nb = s // Q_BLOCK
    n_sel = sel_t.shape[2]
    kern = functools.partial(_nsa_sel_kernel, scale=HALF ** -0.5, n_sel=n_sel)
    wide = pl.BlockSpec((Q_BLOCK, N_PAIR * LANES), lambda i, f: (i, 0))
    grid_spec = pltpu.PrefetchScalarGridSpec(
        num_scalar_prefetch=1,
        grid=(nb,),
        in_specs=[pl.BlockSpec((Q_BLOCK, N_PAIR * LANES), lambda i, f: (i, COL_QC // N_PAIR)),
                  pl.BlockSpec((nb, Q_BLOCK, LANES), lambda i, f: (0, 0, 0)),
                  pl.BlockSpec((nb, LANES, Q_BLOCK), lambda i, f: (0, 0, 0)),
                  pl.BlockSpec((1, 2, n_sel, Q_BLOCK), lambda i, f: (i, 0, 0, 0)),
                  pl.BlockSpec((Q_BLOCK, LANES), lambda i, f: (i, N_GATES // LANES)),
                  wide, wide],
        out_specs=wide,
        scratch_shapes=[pltpu.VMEM((8, Q_BLOCK, LANES), BF16),
                        pltpu.VMEM((8, 1, Q_BLOCK), F32),
                        pltpu.VMEM((8, 1, Q_BLOCK), F32),
                        pltpu.VMEM((8, HALF, Q_BLOCK), F32)],
    )
    return pl.pallas_call(
        kern,
        grid_spec=grid_spec,
        out_shape=jax.ShapeDtypeStruct((s, N_PAIR * LANES), BF16),
        compiler_params=_params("arbitrary"),
        name="nsa_selected",
    )(flags, y_att, ks3, vst3, sel_t, y_rest, o_c, o_w)


def _pair_cols(w):
    lead = w.shape[:-1]
    return w.reshape(*lead, 2, N_PAIR, HALF).swapaxes(-3, -2).reshape(*lead, 2 * N_PAIR * HALF)


def _pair_rows(w):
    return w.reshape(2, N_PAIR, HALF, w.shape[-1]).swapaxes(0, 1).reshape(2 * N_PAIR * HALF, w.shape[-1])


def _prep_w_in(w_in):
    qa = _pair_cols(w_in[:, 0:512])
    qc = _pair_cols(w_in[:, 2304:2816])
    w_att = jnp.concatenate([qa, qc, w_in[:, 768:2304], w_in[:, 512:768], w_in[:, 2816:N_ATT]],
                            axis=1).astype(BF16)
    gc = w_in[:, N_ATT:N_ATT + N_GC].reshape(-1, 2, N_PAIR, 3)
    gc = gc.transpose(0, 3, 2, 1).reshape(-1, N_GC)
    gc = jnp.pad(gc, ((0, 0), (0, LANES - N_GC)))
    w_rest = jnp.concatenate([w_in[:, N_ATT + N_GC:], gc], axis=1).astype(BF16)
    return w_att, w_rest


def _blocks_t(y_att, col, width=1):
    s = y_att.shape[0]
    nb = s // Q_BLOCK
    t = y_att[:, col * LANES:(col + width) * LANES].reshape(nb, Q_BLOCK, width, LANES)
    return t.transpose(2, 0, 3, 1)


def _blocks(y_att, col, width=1):
    s = y_att.shape[0]
    nb = s // Q_BLOCK
    t = y_att[:, col * LANES:(col + width) * LANES].reshape(nb, Q_BLOCK, width, LANES)
    return t.transpose(2, 0, 1, 3)


def _chunk_rows(y_att, col):
    s = y_att.shape[0]
    t = y_att[:, col * LANES:(col + 1) * LANES].reshape(s // CMP_STRIDE, CMP_STRIDE, 2, HALF)
    return t.transpose(2, 0, 1, 3).reshape(2, s // CMP_STRIDE, CMP_STRIDE * HALF)


def _overlap_t(s):
    n_cmp = s // CMP_STRIDE
    n_sel = s // SEL_BLOCK
    cmp_lo = np.arange(n_cmp) * CMP_STRIDE
    cmp_end = cmp_lo + CMP_BLOCK - 1
    sel_lo = np.arange(n_sel) * SEL_BLOCK
    ov = ((cmp_lo[None, :] <= sel_lo[:, None] + SEL_BLOCK - 1)
          & (cmp_end[None, :] >= sel_lo[:, None]) & (np.arange(n_cmp)[None, :] < n_cmp - 1))
    return jnp.asarray(ov.astype(np.float32), dtype=BF16)


def nsa_attention(y_att, y_rest, pe_k, w1_k, w2_k, pe_v, w1_v, w2_v):
    s = y_att.shape[0]
    kc = compress_blocks(_chunk_rows(y_att, COL_KC), pe_k, w1_k, w2_k)
    vc = compress_blocks(_chunk_rows(y_att, COL_VC), pe_v, w1_v, w2_v)
    kc128 = kc.transpose(1, 0, 2).reshape(kc.shape[1], LANES)
    vct128 = vc.transpose(0, 2, 1).reshape(LANES, vc.shape[1])
    o_c, sel_t, cnt = nsa_compressed(y_att, kc128, vct128, _overlap_t(s))
    flags = (cnt[:, 0, :] > 0.5).astype(jnp.int32).reshape(-1)
    o_w = banded_attention(y_att, _blocks_t(y_att, COL_VW)[0], COL_QC, COL_KW,
                           NSA_WINDOW, None, F32)
    ks3 = _blocks(y_att, COL_KS)[0]
    vst3 = _blocks_t(y_att, COL_VS)[0]
    return nsa_selected(flags, y_att, ks3, vst3, sel_t, y_rest, o_c, o_w)


def mixer_layer(x, norm_mix, w_in, sinks, pe_k, w1_k, w2_k, pe_v, w1_v, w2_v,
                w_br_a, w_br_b, w_br_c, w_out):
    w_att, w_rest = _prep_w_in(w_in)
    y_att = norm_matmul(x, norm_mix, w_att, BF16, 512, 896)
    y_rest = norm_matmul(x, norm_mix, w_rest, F32, 512, 896)
    y_a = banded_attention(y_att, _blocks_t(y_att, COL_VA)[0], COL_QA, COL_KA,
                           SWA_WINDOW, sinks, BF16)
    y_b = stick_breaking(y_att, _blocks(y_att, COL_KB, SB_HEADS),
                         _blocks_t(y_att, COL_VB, SB_HEADS))
    y_c = nsa_attention(y_att, y_rest, pe_k, w1_k, w2_k, pe_v, w1_v, w2_v)
    merged = merge_branches(y_a, y_b, y_c, _pair_rows(w_br_a).astype(BF16),
                            w_br_b.astype(BF16), _pair_rows(w_br_c).astype(BF16),
                            y_rest, 512, 512)
    return matmul_residual(merged, w_out.astype(BF16), x, 512, 512)


def ffn_layer(x, norm_ffn, w_gate, w_up, w_down):
    u = ffn_up(x, norm_ffn, w_gate.astype(BF16), w_up.astype(BF16), 512, 512)
    return matmul_residual(u, w_down.astype(BF16), x, 512, 512)


def kernel(x, norm_mix, w_in, swa_sinks, cmp_pe_k, cmp_w1_k, cmp_w2_k, cmp_pe_v, cmp_w1_v,
           cmp_w2_v, w_branch_swa, w_branch_sb, w_branch_nsa, w_out, norm_ffn, w_gate, w_up,
           w_down, norm_final):
    b, s, d = x.shape
    outs = []
    for bi in range(b):
        xb = x[bi]
        for layer in range(norm_mix.shape[0]):
            xb = mixer_layer(xb, norm_mix[layer], w_in[layer], swa_sinks[layer],
                             cmp_pe_k[layer], cmp_w1_k[layer], cmp_w2_k[layer],
                             cmp_pe_v[layer], cmp_w1_v[layer], cmp_w2_v[layer],
                             w_branch_swa[layer], w_branch_sb[layer], w_branch_nsa[layer],
                             w_out[layer])
            xb = ffn_layer(xb, norm_ffn[layer], w_gate[layer], w_up[layer], w_down[layer])
        outs.append(rms_norm_rows(xb, norm_final, 512))
    return jnp.stack(outs, axis=0)
```

```python
import functools

import jax
import jax.numpy as jnp
import numpy as np
from jax import lax
from jax.experimental import pallas as pl
from jax.experimental.pallas import tpu as pltpu

F32 = jnp.float32
BF16 = jnp.bfloat16

D_MODEL = 2048
Q_BLOCK = 128
LANES = 128
HALF = 64
N_PAIR = 4
SWA_WINDOW = 128
NSA_WINDOW = 512
CMP_BLOCK = 32
CMP_STRIDE = 16
CMP_HIDDEN = 256
SEL_BLOCK = 64
SEL_TOPK = 8
SB_HEADS = 4
SB_TILE = 256
D_FF = 5632
NEG = -1e30
SEL_FORCE = 1e9
SEL_TAKEN = -3e38
RMS_EPS = 1e-6
LOG2E = 1.4426950408889634
VMEM_LIMIT = 56 * 1024 * 1024

N_ATT = 3584
N_GC = 24
N_GATES = 3 * D_MODEL
N_REST = N_GATES + LANES
COL_QA, COL_QC, COL_QB, COL_KB, COL_VB = 0, 4, 8, 12, 16
COL_KA, COL_VA, COL_KC, COL_VC, COL_KS, COL_VS, COL_KW, COL_VW = 20, 21, 22, 23, 24, 25, 26, 27


def _nt(a, b):
    return lax.dot_general(a, b, (((1,), (1,)), ((), ())), preferred_element_type=F32)


def _dot(a, b):
    return jnp.dot(a, b, preferred_element_type=F32)


def _params(*sem):
    return pltpu.CompilerParams(dimension_semantics=sem, vmem_limit_bytes=VMEM_LIMIT)


def _head_slope(pair, group):
    return 2.0 ** -(group * N_PAIR + pair + 1)


def _norm_mm_kernel(x_ref, g_ref, w_ref, o_ref, h_ref):
    @pl.when(pl.program_id(1) == 0)
    def _():
        x = x_ref[...]
        ms = jnp.mean(x * x, axis=-1, keepdims=True)
        h_ref[...] = (x * lax.rsqrt(ms + RMS_EPS) * g_ref[...]).astype(BF16)

    o_ref[...] = _dot(h_ref[...], w_ref[...]).astype(o_ref.dtype)


def norm_matmul(x, g, w, out_dtype, tm, tn):
    s, d = x.shape
    n = w.shape[1]
    return pl.pallas_call(
        _norm_mm_kernel,
        grid=(s // tm, n // tn),
        in_specs=[pl.BlockSpec((tm, d), lambda i, j: (i, 0)),
                  pl.BlockSpec((1, d), lambda i, j: (0, 0)),
                  pl.BlockSpec((d, tn), lambda i, j: (0, j))],
        out_specs=pl.BlockSpec((tm, tn), lambda i, j: (i, j)),
        out_shape=jax.ShapeDtypeStruct((s, n), out_dtype),
        scratch_shapes=[pltpu.VMEM((tm, d), BF16)],
        compiler_params=_params("parallel", "arbitrary"),
        name="norm_matmul",
    )(x, g.reshape(1, d), w)


def _mm_res_kernel(a_ref, w_ref, r_ref, o_ref):
    o_ref[...] = r_ref[...] + _dot(a_ref[...], w_ref[...])


def matmul_residual(a, w, res, tm, tn):
    s, k = a.shape
    n = w.shape[1]
    return pl.pallas_call(
        _mm_res_kernel,
        grid=(s // tm, n // tn),
        in_specs=[pl.BlockSpec((tm, k), lambda i, j: (i, 0)),
                  pl.BlockSpec((k, tn), lambda i, j: (0, j)),
                  pl.BlockSpec((tm, tn), lambda i, j: (i, j))],
        out_specs=pl.BlockSpec((tm, tn), lambda i, j: (i, j)),
        out_shape=jax.ShapeDtypeStruct((s, n), F32),
        compiler_params=_params("parallel", "arbitrary"),
        name="matmul_residual",
    )(a, w, res)


def _merge_kernel(ya_ref, yb_ref, yc_ref, wa_ref, wb_ref, wc_ref,
                  ga_ref, gb_ref, gc_ref, o_ref):
    m = jax.nn.sigmoid(ga_ref[...]) * _dot(ya_ref[...], wa_ref[...])
    m = m + jax.nn.sigmoid(gb_ref[...]) * _dot(yb_ref[...], wb_ref[...])
    m = m + jax.nn.sigmoid(gc_ref[...]) * _dot(yc_ref[...], wc_ref[...])
    o_ref[...] = m.astype(o_ref.dtype)


def merge_branches(ya, yb, yc, wa, wb, wc, y_rest, tm, tn):
    s, k = ya.shape
    n = wa.shape[1]
    nj = n // tn
    y_spec = pl.BlockSpec((tm, k), lambda i, j: (i, 0))
    w_spec = pl.BlockSpec((k, tn), lambda i, j: (0, j))

    def gate_spec(br):
        return pl.BlockSpec((tm, tn), lambda i, j: (i, br * nj + j))

    return pl.pallas_call(
        _merge_kernel,
        grid=(s // tm, nj),
        in_specs=[y_spec, y_spec, y_spec, w_spec, w_spec, w_spec,
                  gate_spec(0), gate_spec(1), gate_spec(2)],
        out_specs=pl.BlockSpec((tm, tn), lambda i, j: (i, j)),
        out_shape=jax.ShapeDtypeStruct((s, n), BF16),
        compiler_params=_params("parallel", "arbitrary"),
        name="merge_branches",
    )(ya, yb, yc, wa, wb, wc, y_rest, y_rest, y_rest)


def _ffn_up_kernel(x_ref, g_ref, wg_ref, wu_ref, o_ref, h_ref):
    @pl.when(pl.program_id(1) == 0)
    def _():
        x = x_ref[...]
        ms = jnp.mean(x * x, axis=-1, keepdims=True)
        h_ref[...] = (x * lax.rsqrt(ms + RMS_EPS) * g_ref[...]).astype(BF16)

    h = h_ref[...]
    o_ref[...] = (jax.nn.silu(_dot(h, wg_ref[...])) * _dot(h, wu_ref[...])).astype(o_ref.dtype)


def ffn_up(x, g, wg, wu, tm, tn):
    s, d = x.shape
    n = wg.shape[1]
    w_spec = pl.BlockSpec((d, tn), lambda i, j: (0, j))
    return pl.pallas_call(
        _ffn_up_kernel,
        grid=(s // tm, n // tn),
        in_specs=[pl.BlockSpec((tm, d), lambda i, j: (i, 0)),
                  pl.BlockSpec((1, d), lambda i, j: (0, 0)),
                  w_spec, w_spec],
        out_specs=pl.BlockSpec((tm, tn), lambda i, j: (i, j)),
        out_shape=jax.ShapeDtypeStruct((s, n), BF16),
        scratch_shapes=[pltpu.VMEM((tm, d), BF16)],
        compiler_params=_params("parallel", "arbitrary"),
        name="ffn_up",
    )(x, g.reshape(1, d), wg, wu)


def _rms_kernel(x_ref, g_ref, o_ref):
    x = x_ref[...]
    ms = jnp.mean(x * x, axis=-1, keepdims=True)
    o_ref[...] = x * lax.rsqrt(ms + RMS_EPS) * g_ref[...]


def rms_norm_rows(x, g, tm):
    s, d = x.shape
    return pl.pallas_call(
        _rms_kernel,
        grid=(s // tm,),
        in_specs=[pl.BlockSpec((tm, d), lambda i: (i, 0)),
                  pl.BlockSpec((1, d), lambda i: (0, 0))],
        out_specs=pl.BlockSpec((tm, d), lambda i: (i, 0)),
        out_shape=jax.ShapeDtypeStruct((s, d), F32),
        compiler_params=_params("parallel"),
        name="rms_norm_rows",
    )(x, g.reshape(1, d))


def _half_masks():
    lane = lax.broadcasted_iota(jnp.int32, (Q_BLOCK, LANES), 1)
    return lane < HALF


def _banded_kernel(*refs, n_prev, window, has_sink, scale):
    nk = n_prev + 1
    if has_sink:
        sink_ref, refs = refs[0], refs[1:]
    q_ref = refs[0]
    k_refs = refs[1:1 + nk]
    v_refs = refs[1 + nk:1 + 2 * nk]
    o_ref = refs[1 + 2 * nk]
    i = pl.program_id(0)
    span = nk * Q_BLOCK
    k_all = jnp.concatenate([k_refs[d][...] for d in range(n_prev, -1, -1)], axis=0)
    vt_all = jnp.concatenate([v_refs[d][0] for d in range(n_prev, -1, -1)], axis=1)
    key_rel = lax.broadcasted_iota(jnp.int32, (span, Q_BLOCK), 0)
    q_rel = lax.broadcasted_iota(jnp.int32, (span, Q_BLOCK), 1)
    dist = q_rel + n_prev * Q_BLOCK - key_rel
    key_pos = (i - n_prev) * Q_BLOCK + key_rel
    mask = (dist >= 0) & (dist < window) & (key_pos >= 0)
    distf = dist.astype(F32)
    lo = _half_masks()
    top = lax.broadcasted_iota(jnp.int32, (Q_BLOCK, Q_BLOCK), 0) < HALF
    for pair in range(N_PAIR):
        q128 = q_ref[:, pair * LANES:(pair + 1) * LANES]
        halves = []
        for group in range(2):
            qm = jnp.where(lo if group == 0 else jnp.logical_not(lo), q128, jnp.zeros_like(q128))
            s = _nt(k_all, qm) * scale - _head_slope(pair, group) * distf
            s = jnp.where(mask, s, NEG)
            m = jnp.max(s, axis=0, keepdims=True)
            if has_sink:
                sink = sink_ref[group * N_PAIR + pair]
                m = jnp.maximum(m, sink)
                p = jnp.exp(s - m)
                denom = jnp.sum(p, axis=0, keepdims=True) + jnp.exp(sink - m)
            else:
                p = jnp.where(mask, jnp.exp(s - m), 0.0)
                denom = jnp.maximum(jnp.sum(p, axis=0, keepdims=True), 1e-30)
            pv = _dot(vt_all, p.astype(BF16))
            halves.append(pv * (1.0 / denom))
        ot = jnp.where(top, halves[0], halves[1])
        o_ref[:, pair * LANES:(pair + 1) * LANES] = ot.T.astype(o_ref.dtype)


def banded_attention(y_att, vt3, col_q, col_k, window, sinks, out_dtype):
    s = y_att.shape[0]
    nb = s // Q_BLOCK
    n_prev = -(-window // Q_BLOCK)
    has_sink = sinks is not None
    in_specs = []
    args = []
    if has_sink:
        in_specs.append(pl.BlockSpec(memory_space=pltpu.SMEM))
        args.append(sinks)
    in_specs.append(pl.BlockSpec((Q_BLOCK, N_PAIR * LANES), lambda i: (i, col_q // N_PAIR)))
    args.append(y_att)
    for d in range(n_prev + 1):
        in_specs.append(pl.BlockSpec((Q_BLOCK, LANES),
                                     lambda i, d=d: (jnp.maximum(i - d, 0), col_k)))
        args.append(y_att)
    for d in range(n_prev + 1):
        in_specs.append(pl.BlockSpec((1, LANES, Q_BLOCK),
                                     lambda i, d=d: (jnp.maximum(i - d, 0), 0, 0)))
        args.append(vt3)
    kern = functools.partial(_banded_kernel, n_prev=n_prev, window=window,
                             has_sink=has_sink, scale=HALF ** -0.5)
    return pl.pallas_call(
        kern,
        grid=(nb,),
        in_specs=in_specs,
        out_specs=pl.BlockSpec((Q_BLOCK, N_PAIR * LANES), lambda i: (i, 0)),
        out_shape=jax.ShapeDtypeStruct((s, N_PAIR * LANES), out_dtype),
        compiler_params=_params("parallel"),
        name="banded_attention_w%d" % window,
    )(*args)


def _sb_kernel(q_ref, k_ref, vt_ref, o_ref, ls_ref, acc_ref, *, scale, heads):
    i = pl.program_id(1)
    row = lax.broadcasted_iota(jnp.int32, (SB_TILE, SB_TILE), 0)
    col = lax.broadcasted_iota(jnp.int32, (SB_TILE, SB_TILE), 1)
    later = jnp.where(col > row, 1.0, 0.0).astype(BF16)
    before = row < col

    def visit(kj, diagonal):
        for h in range(heads):
            q = q_ref[:, h * LANES:(h + 1) * LANES]
            zz = _nt(k_ref[h, kj], q) * (scale * LOG2E)
            sp_raw = jnp.maximum(zz, 0.0) + jnp.log(1.0 + jnp.exp2(-jnp.abs(zz))) * LOG2E
            sp = jnp.where(before, sp_raw, 0.0) if diagonal else sp_raw
            gone = ls_ref[h]
            a = jnp.exp2((zz - sp_raw) - _dot(later, sp.astype(BF16)) - gone)
            if diagonal:
                a = jnp.where(before, a, 0.0)
            acc_ref[h] += _dot(vt_ref[h, kj], a.astype(BF16))
            ls_ref[h] = gone + jnp.sum(sp, axis=0, keepdims=True)

    ls_ref[...] = jnp.zeros(ls_ref.shape, F32)
    acc_ref[...] = jnp.zeros(acc_ref.shape, F32)
    visit(i, True)

    def body(t, carry):
        visit(i - 1 - t, False)
        return carry

    lax.fori_loop(0, i, body, 0)
    for h in range(heads):
        o_ref[:, h * LANES:(h + 1) * LANES] = acc_ref[h].T.astype(o_ref.dtype)


def stick_breaking(y_att, k4, vt4, heads=SB_HEADS):
    s = y_att.shape[0]
    nt = s // SB_TILE
    kern = functools.partial(_sb_kernel, scale=LANES ** -0.5, heads=heads)
    resident = pl.Buffered(1)
    return pl.pallas_call(
        kern,
        grid=(SB_HEADS // heads, nt),
        in_specs=[pl.BlockSpec((SB_TILE, heads * LANES), lambda g, i: (i, COL_QB // heads + g)),
                  pl.BlockSpec((heads, nt, SB_TILE, LANES), lambda g, i: (g, 0, 0, 0),
                               pipeline_mode=resident),
                  pl.BlockSpec((heads, nt, LANES, SB_TILE), lambda g, i: (g, 0, 0, 0),
                               pipeline_mode=resident)],
        out_specs=pl.BlockSpec((SB_TILE, heads * LANES), lambda g, i: (i, g)),
        out_shape=jax.ShapeDtypeStruct((s, SB_HEADS * LANES), BF16),
        scratch_shapes=[pltpu.VMEM((heads, 1, SB_TILE), F32),
                        pltpu.VMEM((heads, LANES, SB_TILE), F32)],
        compiler_params=_params("parallel", "arbitrary"),
        name="stick_breaking",
    )(y_att, k4, vt4)


def _compress_kernel(t_ref, pe_ref, w1_ref, w2_ref, o_ref):
    t = t_ref[...]
    half = w1_ref.shape[0] // 2
    w1 = w1_ref[...]
    a = _dot(t, w1[:half])
    b = _dot(t, w1[half:])
    bias = _dot(pe_ref[...], w1)[0:1]
    n = t.shape[0]
    pre = a + pltpu.roll(b, n - 1, 0) + bias
    hid = jax.nn.gelu(pre)
    o_ref[...] = _dot(hid.astype(BF16), w2_ref[...]).astype(o_ref.dtype)


def compress_blocks(t_flat, pe, w1, w2):
    g, n, k = t_flat.shape
    pe_rows = jnp.zeros((8, 2 * k), BF16).at[0].set(pe.reshape(-1).astype(BF16))
    return pl.pallas_call(
        _compress_kernel,
        grid=(g,),
        in_specs=[pl.BlockSpec((None, n, k), lambda gi: (gi, 0, 0)),
                  pl.BlockSpec((8, 2 * k), lambda gi: (0, 0)),
                  pl.BlockSpec((2 * k, CMP_HIDDEN), lambda gi: (0, 0)),
                  pl.BlockSpec((CMP_HIDDEN, HALF), lambda gi: (0, 0))],
        out_specs=pl.BlockSpec((None, n, HALF), lambda gi: (gi, 0, 0)),
        out_shape=jax.ShapeDtypeStruct((g, n, HALF), BF16),
        compiler_params=_params("parallel"),
        name="compress_blocks",
    )(t_flat, pe_rows, w1.astype(BF16), w2.astype(BF16))


def _nsa_cmp_kernel(q_ref, kc_ref, vct_ref, ov_ref, oc_ref, sel_ref, cnt_ref, *, scale):
    i = pl.program_id(0)
    n_cmp = kc_ref.shape[0]
    n_sel = ov_ref.shape[0]
    kc = kc_ref[...]
    vct = vct_ref[...]
    t = i * Q_BLOCK + lax.broadcasted_iota(jnp.int32, (n_cmp, Q_BLOCK), 1)
    cmp_end = lax.broadcasted_iota(jnp.int32, (n_cmp, Q_BLOCK), 0) * CMP_STRIDE + (CMP_BLOCK - 1)
    dist = t - cmp_end
    mask = dist >= 0
    distf = dist.astype(F32)
    lo = _half_masks()
    top = lax.broadcasted_iota(jnp.int32, (Q_BLOCK, Q_BLOCK), 0) < HALF
    psum = [jnp.zeros((n_cmp, Q_BLOCK), F32), jnp.zeros((n_cmp, Q_BLOCK), F32)]
    for pair in range(N_PAIR):
        q128 = q_ref[:, pair * LANES:(pair + 1) * LANES]
        halves = []
        for group in range(2):
            qm = jnp.where(lo if group == 0 else jnp.logical_not(lo), q128, jnp.zeros_like(q128))
            s = _nt(kc, qm) * scale - _head_slope(pair, group) * distf
            s = jnp.where(mask, s, NEG)
            m = jnp.max(s, axis=0, keepdims=True)
            p = jnp.where(mask, jnp.exp(s - m), 0.0)
            p = p * (1.0 / jnp.maximum(jnp.sum(p, axis=0, keepdims=True), 1e-30))
            halves.append(_dot(vct, p.astype(BF16)))
            psum[group] = psum[group] + p
        ot = jnp.where(top, halves[0], halves[1])
        oc_ref[:, pair * LANES:(pair + 1) * LANES] = ot.T

    ov = ov_ref[...]
    tq = i * Q_BLOCK + lax.broadcasted_iota(jnp.int32, (n_sel, Q_BLOCK), 1)
    jblk = lax.broadcasted_iota(jnp.int32, (n_sel, Q_BLOCK), 0)
    jblk_f = jblk.astype(F32)
    jt = tq // SEL_BLOCK
    valid = jblk * SEL_BLOCK <= tq
    forced = (jblk == 0) | (jblk == jt) | (jblk == jt - 1)
    total = jnp.zeros((n_sel, Q_BLOCK), F32)
    for group in range(2):
        hi = psum[group].astype(BF16)
        lo_part = (psum[group] - hi.astype(F32)).astype(BF16)
        imp = _dot(ov, hi) + _dot(ov, lo_part)
        score = jnp.where(valid, jnp.where(forced, SEL_FORCE, imp), -SEL_FORCE)
        sel = jnp.zeros((n_sel, Q_BLOCK), F32)
        for _ in range(min(SEL_TOPK, n_sel)):
            best = jnp.max(score, axis=0, keepdims=True)
            first = jnp.min(jnp.where(score == best, jblk_f, float(n_sel)), axis=0, keepdims=True)
            hit = jblk_f == first
            sel = jnp.where(hit, 1.0, sel)
            score = jnp.where(hit, SEL_TAKEN, score)
        sel_ref[0, group] = sel
        total = total + sel
    cnt_ref[0] = _nt(jnp.ones((8, Q_BLOCK), BF16), total.astype(BF16))


def nsa_compressed(y_att, kc128, vct128, overlap_t):
    s = y_att.shape[0]
    nb = s // Q_BLOCK
    n_cmp = kc128.shape[0]
    n_sel = overlap_t.shape[0]
    kern = functools.partial(_nsa_cmp_kernel, scale=HALF ** -0.5)
    return pl.pallas_call(
        kern,
        grid=(nb,),
        in_specs=[pl.BlockSpec((Q_BLOCK, N_PAIR * LANES), lambda i: (i, COL_QC // N_PAIR)),
                  pl.BlockSpec((n_cmp, LANES), lambda i: (0, 0)),
                  pl.BlockSpec((LANES, n_cmp), lambda i: (0, 0)),
                  pl.BlockSpec((n_sel, n_cmp), lambda i: (0, 0))],
        out_specs=[pl.BlockSpec((Q_BLOCK, N_PAIR * LANES), lambda i: (i, 0)),
                   pl.BlockSpec((1, 2, n_sel, Q_BLOCK), lambda i: (i, 0, 0, 0)),
                   pl.BlockSpec((1, 8, n_sel), lambda i: (i, 0, 0))],
        out_shape=[jax.ShapeDtypeStruct((s, N_PAIR * LANES), F32),
                   jax.ShapeDtypeStruct((nb, 2, n_sel, Q_BLOCK), F32),
                   jax.ShapeDtypeStruct((nb, 8, n_sel), F32)],
        compiler_params=_params("parallel"),
        name="nsa_compressed",
    )(y_att, kc128, vct128, overlap_t)


def _nsa_sel_kernel(flags_ref, q_ref, ks_ref, vst_ref, sel_ref, gc_ref, oc_ref, ow_ref,
                    o_ref, qm_ref, m_ref, l_ref, acc_ref, *, scale, n_sel):
    i = pl.program_id(0)
    lo = _half_masks()
    for pair in range(N_PAIR):
        q128 = q_ref[:, pair * LANES:(pair + 1) * LANES]
        for group in range(2):
            qm_ref[pair * 2 + group] = jnp.where(
                lo if group == 0 else jnp.logical_not(lo), q128, jnp.zeros_like(q128))
    m_ref[...] = jnp.full(m_ref.shape, NEG, F32)
    l_ref[...] = jnp.zeros(l_ref.shape, F32)
    acc_ref[...] = jnp.zeros(acc_ref.shape, F32)
    top = lax.broadcasted_iota(jnp.int32, (Q_BLOCK, Q_BLOCK), 0) < HALF

    def body(p, carry):
        base = i * n_sel + 2 * p
        active = (flags_ref[base] + flags_ref[base + 1]) > 0

        @pl.when(active)
        def _():
            k = ks_ref[p]
            vt = vst_ref[p]
            key_pos = p * Q_BLOCK + lax.broadcasted_iota(jnp.int32, (Q_BLOCK, Q_BLOCK), 0)
            t = i * Q_BLOCK + lax.broadcasted_iota(jnp.int32, (Q_BLOCK, Q_BLOCK), 1)
            dist = t - key_pos
            distf = dist.astype(F32)
            masks = []
            for group in range(2):
                r0 = sel_ref[0, group, pl.ds(2 * p, 1), :]
                r1 = sel_ref[0, group, pl.ds(2 * p + 1, 1), :]
                picked = jnp.where(top, r0, r1)
                masks.append((picked > 0.5) & (dist >= 0))
            for pair in range(N_PAIR):
                for group in range(2):
                    h = pair * 2 + group
                    mask = masks[group]
                    s = _nt(k, qm_ref[h]) * scale - _head_slope(pair, group) * distf
                    s = jnp.where(mask, s, NEG)
                    m_old = m_ref[h]
                    m_new = jnp.maximum(m_old, jnp.max(s, axis=0, keepdims=True))
                    alpha = jnp.exp(m_old - m_new)
                    pr = jnp.where(mask, jnp.exp(s - m_new), 0.0)
                    l_ref[h] = alpha * l_ref[h] + jnp.sum(pr, axis=0, keepdims=True)
                    pv = _dot(vt, pr.astype(BF16))
                    acc_ref[h] = alpha * acc_ref[h] + pv[group * HALF:(group + 1) * HALF]
                    m_ref[h] = m_new

        return carry

    lax.fori_loop(0, i + 1, body, 0)

    gates = jax.nn.sigmoid(gc_ref[...])
    for pair in range(N_PAIR):
        parts = []
        for group in range(2):
            h = pair * 2 + group
            parts.append(acc_ref[h] * (1.0 / jnp.maximum(l_ref[h], 1e-30)))
        o_s = jnp.concatenate(parts, axis=0).T
        cols = slice(pair * LANES, (pair + 1) * LANES)

        def gate(branch):
            c = branch * 8 + pair * 2
            return jnp.where(lo, gates[:, c:c + 1], gates[:, c + 1:c + 2])

        y = gate(0) * oc_ref[:, cols] + gate(1) * o_s + gate(2) * ow_ref[:, cols]
        o_ref[:, cols] = y.astype(o_ref.dtype)


def nsa_selected(flags, y_att, ks3, vst3, sel_t, y_rest, o_c, o_w):
    s = y_att.shape[0]
    nb = s // Q_BLOCK
    n_sel = sel_t.shape[2]
    kern = functools.partial(_nsa_sel_kernel, scale=HALF ** -0.5, n_sel=n_sel)
    wide = pl.BlockSpec((Q_BLOCK, N_PAIR * LANES), lambda i, f: (i, 0))
    grid_spec = pltpu.PrefetchScalarGridSpec(
        num_scalar_prefetch=1,
        grid=(nb,),
        in_specs=[pl.BlockSpec((Q_BLOCK, N_PAIR * LANES), lambda i, f: (i, COL_QC // N_PAIR)),
                  pl.BlockSpec((nb, Q_BLOCK, LANES), lambda i, f: (0, 0, 0)),
                  pl.BlockSpec((nb, LANES, Q_BLOCK), lambda i, f: (0, 0, 0)),
                  pl.BlockSpec((1, 2, n_sel, Q_BLOCK), lambda i, f: (i, 0, 0, 0)),
                  pl.BlockSpec((Q_BLOCK, LANES), lambda i, f: (i, N_GATES // LANES)),
                  wide, wide],
        out_specs=wide,
        scratch_shapes=[pltpu.VMEM((8, Q_BLOCK, LANES), BF16),
                        pltpu.VMEM((8, 1, Q_BLOCK), F32),
                        pltpu.VMEM((8, 1, Q_BLOCK), F32),
                        pltpu.VMEM((8, HALF, Q_BLOCK), F32)],
    )
    return pl.pallas_call(
        kern,
        grid_spec=grid_spec,
        out_shape=jax.ShapeDtypeStruct((s, N_PAIR * LANES), BF16),
        compiler_params=_params("arbitrary"),
        name="nsa_selected",
    )(flags, y_att, ks3, vst3, sel_t, y_rest, o_c, o_w)


def _pair_cols(w):
    lead = w.shape[:-1]
    return w.reshape(*lead, 2, N_PAIR, HALF).swapaxes(-3, -2).reshape(*lead, 2 * N_PAIR * HALF)


def _pair_rows(w):
    return w.reshape(2, N_PAIR, HALF, w.shape[-1]).swapaxes(0, 1).reshape(2 * N_PAIR * HALF, w.shape[-1])


def _prep_w_in(w_in):
    qa = _pair_cols(w_in[:, 0:512])
    qc = _pair_cols(w_in[:, 2304:2816])
    w_att = jnp.concatenate([qa, qc, w_in[:, 768:2304], w_in[:, 512:768], w_in[:, 2816:N_ATT]],
                            axis=1).astype(BF16)
    gc = w_in[:, N_ATT:N_ATT + N_GC].reshape(-1, 2, N_PAIR, 3)
    gc = gc.transpose(0, 3, 2, 1).reshape(-1, N_GC)
    gc = jnp.pad(gc, ((0, 0), (0, LANES - N_GC)))
    w_rest = jnp.concatenate([w_in[:, N_ATT + N_GC:], gc], axis=1).astype(BF16)
    return w_att, w_rest


def _blocks_t(y_att, col, width=1, tile=Q_BLOCK):
    s = y_att.shape[0]
    t = y_att[:, col * LANES:(col + width) * LANES].reshape(s // tile, tile, width, LANES)
    return t.transpose(2, 0, 3, 1)


def _blocks(y_att, col, width=1, tile=Q_BLOCK):
    s = y_att.shape[0]
    t = y_att[:, col * LANES:(col + width) * LANES].reshape(s // tile, tile, width, LANES)
    return t.transpose(2, 0, 1, 3)


def _chunk_rows(y_att, col):
    s = y_att.shape[0]
    t = y_att[:, col * LANES:(col + 1) * LANES].reshape(s // CMP_STRIDE, CMP_STRIDE, 2, HALF)
    return t.transpose(2, 0, 1, 3).reshape(2, s // CMP_STRIDE, CMP_STRIDE * HALF)


def _overlap_t(s):
    n_cmp = s // CMP_STRIDE
    n_sel = s // SEL_BLOCK
    cmp_lo = np.arange(n_cmp) * CMP_STRIDE
    cmp_end = cmp_lo + CMP_BLOCK - 1
    sel_lo = np.arange(n_sel) * SEL_BLOCK
    ov = ((cmp_lo[None, :] <= sel_lo[:, None] + SEL_BLOCK - 1)
          & (cmp_end[None, :] >= sel_lo[:, None]) & (np.arange(n_cmp)[None, :] < n_cmp - 1))
    return jnp.asarray(ov.astype(np.float32), dtype=BF16)


def nsa_attention(y_att, y_rest, pe_k, w1_k, w2_k, pe_v, w1_v, w2_v):
    s = y_att.shape[0]
    kc = compress_blocks(_chunk_rows(y_att, COL_KC), pe_k, w1_k, w2_k)
    vc = compress_blocks(_chunk_rows(y_att, COL_VC), pe_v, w1_v, w2_v)
    kc128 = kc.transpose(1, 0, 2).reshape(kc.shape[1], LANES)
    vct128 = vc.transpose(0, 2, 1).reshape(LANES, vc.shape[1])
    o_c, sel_t, cnt = nsa_compressed(y_att, kc128, vct128, _overlap_t(s))
    flags = (cnt[:, 0, :] > 0.5).astype(jnp.int32).reshape(-1)
    o_w = banded_attention(y_att, _blocks_t(y_att, COL_VW)[0], COL_QC, COL_KW,
                           NSA_WINDOW, None, F32)
    ks3 = _blocks(y_att, COL_KS)[0]
    vst3 = _blocks_t(y_att, COL_VS)[0]
    return nsa_selected(flags, y_att, ks3, vst3, sel_t, y_rest, o_c, o_w)


def mixer_layer(x, norm_mix, w_in, sinks, pe_k, w1_k, w2_k, pe_v, w1_v, w2_v,
                w_br_a, w_br_b, w_br_c, w_out):
    w_att, w_rest = _prep_w_in(w_in)
    y_att = norm_matmul(x, norm_mix, w_att, BF16, 512, 896)
    y_rest = norm_matmul(x, norm_mix, w_rest, F32, 512, 896)
    y_a = banded_attention(y_att, _blocks_t(y_att, COL_VA)[0], COL_QA, COL_KA,
                           SWA_WINDOW, sinks, BF16)
    y_b = stick_breaking(y_att, _blocks(y_att, COL_KB, SB_HEADS, SB_TILE),
                         _blocks_t(y_att, COL_VB, SB_HEADS, SB_TILE))
    y_c = nsa_attention(y_att, y_rest, pe_k, w1_k, w2_k, pe_v, w1_v, w2_v)
    merged = merge_branches(y_a, y_b, y_c, _pair_rows(w_br_a).astype(BF16),
                            w_br_b.astype(BF16), _pair_rows(w_br_c).astype(BF16),
                            y_rest, 512, 512)
    return matmul_residual(merged, w_out.astype(BF16), x, 512, 512)


def ffn_layer(x, norm_ffn, w_gate, w_up, w_down):
    u = ffn_up(x, norm_ffn, w_gate.astype(BF16), w_up.astype(BF16), 512, 512)
    return matmul_residual(u, w_down.astype(BF16), x, 512, 512)


def kernel(x, norm_mix, w_in, swa_sinks, cmp_pe_k, cmp_w1_k, cmp_w2_k, cmp_pe_v, cmp_w1_v,
           cmp_w2_v, w_branch_swa, w_branch_sb, w_branch_nsa, w_out, norm_ffn, w_gate, w_up,
           w_down, norm_final):
    b, s, d = x.shape
    outs = []
    for bi in range(b):
        xb = x[bi]
        for layer in range(norm_mix.shape[0]):
            xb = mixer_layer(xb, norm_mix[layer], w_in[layer], swa_sinks[layer],
                             cmp_pe_k[layer], cmp_w1_k[layer], cmp_w2_k[layer],
                             cmp_pe_v[layer], cmp_w1_v[layer], cmp_w2_v[layer],
                             w_branch_swa[layer], w_branch_sb[layer], w_branch_nsa[layer],
                             w_out[layer])
            xb = ffn_layer(xb, norm_ffn[layer], w_gate[layer], w_up[layer], w_down[layer])
        outs.append(rms_norm_rows(xb, norm_final, 512))
    return jnp.stack(outs, axis=0)
```

```python
import functools

import jax
import jax.numpy as jnp
import numpy as np
from jax import lax
from jax.experimental import pallas as pl
from jax.experimental.pallas import tpu as pltpu

F32 = jnp.float32
BF16 = jnp.bfloat16

D_MODEL = 2048
Q_BLOCK = 128
LANES = 128
HALF = 64
N_PAIR = 4
SWA_WINDOW = 128
NSA_WINDOW = 512
CMP_BLOCK = 32
CMP_STRIDE = 16
CMP_HIDDEN = 256
SEL_BLOCK = 64
SEL_TOPK = 8
SB_HEADS = 4
SB_TILE = 256
D_FF = 5632
NEG = -1e30
SEL_FORCE = 1e9
SEL_TAKEN = -3e38
RMS_EPS = 1e-6
LOG2E = 1.4426950408889634
SB_DEAD_BITS = 150.0
VMEM_LIMIT = 56 * 1024 * 1024

N_ATT = 3584
N_GC = 24
N_GATES = 3 * D_MODEL
N_REST = N_GATES + LANES
COL_QA, COL_QC, COL_QB, COL_KB, COL_VB = 0, 4, 8, 12, 16
COL_KA, COL_VA, COL_KC, COL_VC, COL_KS, COL_VS, COL_KW, COL_VW = 20, 21, 22, 23, 24, 25, 26, 27


def _nt(a, b):
    return lax.dot_general(a, b, (((1,), (1,)), ((), ())), preferred_element_type=F32)


def _dot(a, b):
    return jnp.dot(a, b, preferred_element_type=F32)


def _params(*sem):
    return pltpu.CompilerParams(dimension_semantics=sem, vmem_limit_bytes=VMEM_LIMIT)


def _head_slope(pair, group):
    return 2.0 ** -(group * N_PAIR + pair + 1)


def _norm_mm_kernel(x_ref, g_ref, w_ref, o_ref, h_ref):
    @pl.when(pl.program_id(1) == 0)
    def _():
        x = x_ref[...]
        ms = jnp.mean(x * x, axis=-1, keepdims=True)
        h_ref[...] = (x * lax.rsqrt(ms + RMS_EPS) * g_ref[...]).astype(BF16)

    o_ref[...] = _dot(h_ref[...], w_ref[...]).astype(o_ref.dtype)


def norm_matmul(x, g, w, out_dtype, tm, tn):
    s, d = x.shape
    n = w.shape[1]
    return pl.pallas_call(
        _norm_mm_kernel,
        grid=(s // tm, n // tn),
        in_specs=[pl.BlockSpec((tm, d), lambda i, j: (i, 0)),
                  pl.BlockSpec((1, d), lambda i, j: (0, 0)),
                  pl.BlockSpec((d, tn), lambda i, j: (0, j))],
        out_specs=pl.BlockSpec((tm, tn), lambda i, j: (i, j)),
        out_shape=jax.ShapeDtypeStruct((s, n), out_dtype),
        scratch_shapes=[pltpu.VMEM((tm, d), BF16)],
        compiler_params=_params("parallel", "arbitrary"),
        name="norm_matmul",
    )(x, g.reshape(1, d), w)


def _mm_res_kernel(a_ref, w_ref, r_ref, o_ref):
    o_ref[...] = r_ref[...] + _dot(a_ref[...], w_ref[...])


def matmul_residual(a, w, res, tm, tn):
    s, k = a.shape
    n = w.shape[1]
    return pl.pallas_call(
        _mm_res_kernel,
        grid=(s // tm, n // tn),
        in_specs=[pl.BlockSpec((tm, k), lambda i, j: (i, 0)),
                  pl.BlockSpec((k, tn), lambda i, j: (0, j)),
                  pl.BlockSpec((tm, tn), lambda i, j: (i, j))],
        out_specs=pl.BlockSpec((tm, tn), lambda i, j: (i, j)),
        out_shape=jax.ShapeDtypeStruct((s, n), F32),
        compiler_params=_params("parallel", "arbitrary"),
        name="matmul_residual",
    )(a, w, res)


def _merge_kernel(ya_ref, yb_ref, yc_ref, wa_ref, wb_ref, wc_ref,
                  ga_ref, gb_ref, gc_ref, o_ref):
    m = jax.nn.sigmoid(ga_ref[...]) * _dot(ya_ref[...], wa_ref[...])
    m = m + jax.nn.sigmoid(gb_ref[...]) * _dot(yb_ref[...], wb_ref[...])
    m = m + jax.nn.sigmoid(gc_ref[...]) * _dot(yc_ref[...], wc_ref[...])
    o_ref[...] = m.astype(o_ref.dtype)


def merge_branches(ya, yb, yc, wa, wb, wc, y_rest, tm, tn):
    s, k = ya.shape
    n = wa.shape[1]
    nj = n // tn
    y_spec = pl.BlockSpec((tm, k), lambda i, j: (i, 0))
    w_spec = pl.BlockSpec((k, tn), lambda i, j: (0, j))

    def gate_spec(br):
        return pl.BlockSpec((tm, tn), lambda i, j: (i, br * nj + j))

    return pl.pallas_call(
        _merge_kernel,
        grid=(s // tm, nj),
        in_specs=[y_spec, y_spec, y_spec, w_spec, w_spec, w_spec,
                  gate_spec(0), gate_spec(1), gate_spec(2)],
        out_specs=pl.BlockSpec((tm, tn), lambda i, j: (i, j)),
        out_shape=jax.ShapeDtypeStruct((s, n), BF16),
        compiler_params=_params("parallel", "arbitrary"),
        name="merge_branches",
    )(ya, yb, yc, wa, wb, wc, y_rest, y_rest, y_rest)


def _ffn_up_kernel(x_ref, g_ref, wg_ref, wu_ref, o_ref, h_ref):
    @pl.when(pl.program_id(1) == 0)
    def _():
        x = x_ref[...]
        ms = jnp.mean(x * x, axis=-1, keepdims=True)
        h_ref[...] = (x * lax.rsqrt(ms + RMS_EPS) * g_ref[...]).astype(BF16)

    h = h_ref[...]
    o_ref[...] = (jax.nn.silu(_dot(h, wg_ref[...])) * _dot(h, wu_ref[...])).astype(o_ref.dtype)


def ffn_up(x, g, wg, wu, tm, tn):
    s, d = x.shape
    n = wg.shape[1]
    w_spec = pl.BlockSpec((d, tn), lambda i, j: (0, j))
    return pl.pallas_call(
        _ffn_up_kernel,
        grid=(s // tm, n // tn),
        in_specs=[pl.BlockSpec((tm, d), lambda i, j: (i, 0)),
                  pl.BlockSpec((1, d), lambda i, j: (0, 0)),
                  w_spec, w_spec],
        out_specs=pl.BlockSpec((tm, tn), lambda i, j: (i, j)),
        out_shape=jax.ShapeDtypeStruct((s, n), BF16),
        scratch_shapes=[pltpu.VMEM((tm, d), BF16)],
        compiler_params=_params("parallel", "arbitrary"),
        name="ffn_up",
    )(x, g.reshape(1, d), wg, wu)


def _rms_kernel(x_ref, g_ref, o_ref):
    x = x_ref[...]
    ms = jnp.mean(x * x, axis=-1, keepdims=True)
    o_ref[...] = x * lax.rsqrt(ms + RMS_EPS) * g_ref[...]


def rms_norm_rows(x, g, tm):
    s, d = x.shape
    return pl.pallas_call(
        _rms_kernel,
        grid=(s // tm,),
        in_specs=[pl.BlockSpec((tm, d), lambda i: (i, 0)),
                  pl.BlockSpec((1, d), lambda i: (0, 0))],
        out_specs=pl.BlockSpec((tm, d), lambda i: (i, 0)),
        out_shape=jax.ShapeDtypeStruct((s, d), F32),
        compiler_params=_params("parallel"),
        name="rms_norm_rows",
    )(x, g.reshape(1, d))


def _half_masks():
    lane = lax.broadcasted_iota(jnp.int32, (Q_BLOCK, LANES), 1)
    return lane < HALF


def _banded_kernel(*refs, n_prev, window, has_sink, scale):
    nk = n_prev + 1
    if has_sink:
        sink_ref, refs = refs[0], refs[1:]
    q_ref = refs[0]
    k_refs = refs[1:1 + nk]
    v_refs = refs[1 + nk:1 + 2 * nk]
    o_ref = refs[1 + 2 * nk]
    i = pl.program_id(0)
    span = nk * Q_BLOCK
    k_all = jnp.concatenate([k_refs[d][...] for d in range(n_prev, -1, -1)], axis=0)
    vt_all = jnp.concatenate([v_refs[d][0] for d in range(n_prev, -1, -1)], axis=1)
    key_rel = lax.broadcasted_iota(jnp.int32, (span, Q_BLOCK), 0)
    q_rel = lax.broadcasted_iota(jnp.int32, (span, Q_BLOCK), 1)
    dist = q_rel + n_prev * Q_BLOCK - key_rel
    key_pos = (i - n_prev) * Q_BLOCK + key_rel
    mask = (dist >= 0) & (dist < window) & (key_pos >= 0)
    distf = dist.astype(F32)
    lo = _half_masks()
    top = lax.broadcasted_iota(jnp.int32, (Q_BLOCK, Q_BLOCK), 0) < HALF
    for pair in range(N_PAIR):
        q128 = q_ref[:, pair * LANES:(pair + 1) * LANES]
        halves = []
        for group in range(2):
            qm = jnp.where(lo if group == 0 else jnp.logical_not(lo), q128, jnp.zeros_like(q128))
            s = _nt(k_all, qm) * scale - _head_slope(pair, group) * distf
            s = jnp.where(mask, s, NEG)
            m = jnp.max(s, axis=0, keepdims=True)
            if has_sink:
                sink = sink_ref[group * N_PAIR + pair]
                m = jnp.maximum(m, sink)
                p = jnp.exp(s - m)
                denom = jnp.sum(p, axis=0, keepdims=True) + jnp.exp(sink - m)
            else:
                p = jnp.where(mask, jnp.exp(s - m), 0.0)
                denom = jnp.maximum(jnp.sum(p, axis=0, keepdims=True), 1e-30)
            pv = _dot(vt_all, p.astype(BF16))
            halves.append(pv * (1.0 / denom))
        ot = jnp.where(top, halves[0], halves[1])
        o_ref[:, pair * LANES:(pair + 1) * LANES] = ot.T.astype(o_ref.dtype)


def banded_attention(y_att, vt3, col_q, col_k, window, sinks, out_dtype):
    s = y_att.shape[0]
    nb = s // Q_BLOCK
    n_prev = -(-window // Q_BLOCK)
    has_sink = sinks is not None
    in_specs = []
    args = []
    if has_sink:
        in_specs.append(pl.BlockSpec(memory_space=pltpu.SMEM))
        args.append(sinks)
    in_specs.append(pl.BlockSpec((Q_BLOCK, N_PAIR * LANES), lambda i: (i, col_q // N_PAIR)))
    args.append(y_att)
    for d in range(n_prev + 1):
        in_specs.append(pl.BlockSpec((Q_BLOCK, LANES),
                                     lambda i, d=d: (jnp.maximum(i - d, 0), col_k)))
        args.append(y_att)
    for d in range(n_prev + 1):
        in_specs.append(pl.BlockSpec((1, LANES, Q_BLOCK),
                                     lambda i, d=d: (jnp.maximum(i - d, 0), 0, 0)))
        args.append(vt3)
    kern = functools.partial(_banded_kernel, n_prev=n_prev, window=window,
                             has_sink=has_sink, scale=HALF ** -0.5)
    return pl.pallas_call(
        kern,
        grid=(nb,),
        in_specs=in_specs,
        out_specs=pl.BlockSpec((Q_BLOCK, N_PAIR * LANES), lambda i: (i, 0)),
        out_shape=jax.ShapeDtypeStruct((s, N_PAIR * LANES), out_dtype),
        compiler_params=_params("parallel"),
        name="banded_attention_w%d" % window,
    )(*args)


def _sb_kernel(q_ref, k_ref, vt_ref, o_ref, ls_ref, acc_ref, *, scale, heads):
    i = pl.program_id(1)
    row = lax.broadcasted_iota(jnp.int32, (SB_TILE, SB_TILE), 0)
    col = lax.broadcasted_iota(jnp.int32, (SB_TILE, SB_TILE), 1)
    later = jnp.where(col > row, 1.0, 0.0).astype(BF16)
    before = row < col

    def visit(kj, diagonal):
        gone = [ls_ref[h] for h in range(heads)]
        pv, gone_new = [], []
        for h in range(heads):
            q = q_ref[:, h * LANES:(h + 1) * LANES]
            zz = _nt(k_ref[h, kj], q) * (scale * LOG2E)
            sp_raw = jnp.maximum(zz, 0.0) + jnp.log(1.0 + jnp.exp2(-jnp.abs(zz))) * LOG2E
            sp = jnp.where(before, sp_raw, 0.0) if diagonal else sp_raw
            a = jnp.exp2((zz - sp_raw) - _dot(later, sp.astype(BF16)) - gone[h])
            if diagonal:
                a = jnp.where(before, a, 0.0)
            pv.append(_dot(vt_ref[h, kj], a.astype(BF16)))
            gone_new.append(gone[h] + jnp.sum(sp, axis=0, keepdims=True))
        alive = gone_new[0]
        for h in range(heads):
            acc_ref[h] += pv[h]
            ls_ref[h] = gone_new[h]
            alive = jnp.minimum(alive, gone_new[h])
        return jnp.min(alive)

    ls_ref[...] = jnp.zeros(ls_ref.shape, F32)
    acc_ref[...] = jnp.zeros(acc_ref.shape, F32)
    alive0 = visit(i, True)

    def cond(carry):
        return (carry[0] < i) & (carry[1] < SB_DEAD_BITS)

    def body(carry):
        return carry[0] + 1, visit(i - 1 - carry[0], False)

    lax.while_loop(cond, body, (jnp.int32(0), alive0))
    for h in range(heads):
        o_ref[:, h * LANES:(h + 1) * LANES] = acc_ref[h].T.astype(o_ref.dtype)


def stick_breaking(y_att, k4, vt4, heads=SB_HEADS):
    s = y_att.shape[0]
    nt = s // SB_TILE
    kern = functools.partial(_sb_kernel, scale=LANES ** -0.5, heads=heads)
    resident = pl.Buffered(1)
    return pl.pallas_call(
        kern,
        grid=(SB_HEADS // heads, nt),
        in_specs=[pl.BlockSpec((SB_TILE, heads * LANES), lambda g, i: (i, COL_QB // heads + g)),
                  pl.BlockSpec((heads, nt, SB_TILE, LANES), lambda g, i: (g, 0, 0, 0),
                               pipeline_mode=resident),
                  pl.BlockSpec((heads, nt, LANES, SB_TILE), lambda g, i: (g, 0, 0, 0),
                               pipeline_mode=resident)],
        out_specs=pl.BlockSpec((SB_TILE, heads * LANES), lambda g, i: (i, g)),
        out_shape=jax.ShapeDtypeStruct((s, SB_HEADS * LANES), BF16),
        scratch_shapes=[pltpu.VMEM((heads, 1, SB_TILE), F32),
                        pltpu.VMEM((heads, LANES, SB_TILE), F32)],
        compiler_params=_params("parallel", "arbitrary"),
        name="stick_breaking",
    )(y_att, k4, vt4)


def _compress_kernel(t_ref, pe_ref, w1_ref, w2_ref, o_ref):
    t = t_ref[...]
    half = w1_ref.shape[0] // 2
    w1 = w1_ref[...]
    a = _dot(t, w1[:half])
    b = _dot(t, w1[half:])
    bias = _dot(pe_ref[...], w1)[0:1]
    n = t.shape[0]
    pre = a + pltpu.roll(b, n - 1, 0) + bias
    hid = jax.nn.gelu(pre)
    o_ref[...] = _dot(hid.astype(BF16), w2_ref[...]).astype(o_ref.dtype)


def compress_blocks(t_flat, pe, w1, w2):
    g, n, k = t_flat.shape
    pe_rows = jnp.zeros((8, 2 * k), BF16).at[0].set(pe.reshape(-1).astype(BF16))
    return pl.pallas_call(
        _compress_kernel,
        grid=(g,),
        in_specs=[pl.BlockSpec((None, n, k), lambda gi: (gi, 0, 0)),
                  pl.BlockSpec((8, 2 * k), lambda gi: (0, 0)),
                  pl.BlockSpec((2 * k, CMP_HIDDEN), lambda gi: (0, 0)),
                  pl.BlockSpec((CMP_HIDDEN, HALF), lambda gi: (0, 0))],
        out_specs=pl.BlockSpec((None, n, HALF), lambda gi: (gi, 0, 0)),
        out_shape=jax.ShapeDtypeStruct((g, n, HALF), BF16),
        compiler_params=_params("parallel"),
        name="compress_blocks",
    )(t_flat, pe_rows, w1.astype(BF16), w2.astype(BF16))


def _nsa_cmp_kernel(q_ref, kc_ref, vct_ref, ov_ref, oc_ref, sel_ref, cnt_ref, *, scale):
    i = pl.program_id(0)
    n_cmp = kc_ref.shape[0]
    n_sel = ov_ref.shape[0]
    kc = kc_ref[...]
    vct = vct_ref[...]
    t = i * Q_BLOCK + lax.broadcasted_iota(jnp.int32, (n_cmp, Q_BLOCK), 1)
    cmp_end = lax.broadcasted_iota(jnp.int32, (n_cmp, Q_BLOCK), 0) * CMP_STRIDE + (CMP_BLOCK - 1)
    dist = t - cmp_end
    mask = dist >= 0
    distf = dist.astype(F32)
    lo = _half_masks()
    top = lax.broadcasted_iota(jnp.int32, (Q_BLOCK, Q_BLOCK), 0) < HALF
    psum = [jnp.zeros((n_cmp, Q_BLOCK), F32), jnp.zeros((n_cmp, Q_BLOCK), F32)]
    for pair in range(N_PAIR):
        q128 = q_ref[:, pair * LANES:(pair + 1) * LANES]
        halves = []
        for group in range(2):
            qm = jnp.where(lo if group == 0 else jnp.logical_not(lo), q128, jnp.zeros_like(q128))
            s = _nt(kc, qm) * scale - _head_slope(pair, group) * distf
            s = jnp.where(mask, s, NEG)
            m = jnp.max(s, axis=0, keepdims=True)
            p = jnp.where(mask, jnp.exp(s - m), 0.0)
            p = p * (1.0 / jnp.maximum(jnp.sum(p, axis=0, keepdims=True), 1e-30))
            halves.append(_dot(vct, p.astype(BF16)))
            psum[group] = psum[group] + p
        ot = jnp.where(top, halves[0], halves[1])
        oc_ref[:, pair * LANES:(pair + 1) * LANES] = ot.T

    ov = ov_ref[...]
    tq = i * Q_BLOCK + lax.broadcasted_iota(jnp.int32, (n_sel, Q_BLOCK), 1)
    jblk = lax.broadcasted_iota(jnp.int32, (n_sel, Q_BLOCK), 0)
    jblk_f = jblk.astype(F32)
    jt = tq // SEL_BLOCK
    valid = jblk * SEL_BLOCK <= tq
    forced = (jblk == 0) | (jblk == jt) | (jblk == jt - 1)
    total = jnp.zeros((n_sel, Q_BLOCK), F32)
    for group in range(2):
        hi = psum[group].astype(BF16)
        lo_part = (psum[group] - hi.astype(F32)).astype(BF16)
        imp = _dot(ov, hi) + _dot(ov, lo_part)
        score = jnp.where(valid, jnp.where(forced, SEL_FORCE, imp), -SEL_FORCE)
        sel = jnp.zeros((n_sel, Q_BLOCK), F32)
        for _ in range(min(SEL_TOPK, n_sel)):
            best = jnp.max(score, axis=0, keepdims=True)
            first = jnp.min(jnp.where(score == best, jblk_f, float(n_sel)), axis=0, keepdims=True)
            hit = jblk_f == first
            sel = jnp.where(hit, 1.0, sel)
            score = jnp.where(hit, SEL_TAKEN, score)
        sel_ref[0, group] = sel
        total = total + sel
    cnt_ref[0] = _nt(jnp.ones((8, Q_BLOCK), BF16), total.astype(BF16))


def nsa_compressed(y_att, kc128, vct128, overlap_t):
    s = y_att.shape[0]
    nb = s // Q_BLOCK
    n_cmp = kc128.shape[0]
    n_sel = overlap_t.shape[0]
    kern = functools.partial(_nsa_cmp_kernel, scale=HALF ** -0.5)
    return pl.pallas_call(
        kern,
        grid=(nb,),
        in_specs=[pl.BlockSpec((Q_BLOCK, N_PAIR * LANES), lambda i: (i, COL_QC // N_PAIR)),
                  pl.BlockSpec((n_cmp, LANES), lambda i: (0, 0)),
                  pl.BlockSpec((LANES, n_cmp), lambda i: (0, 0)),
                  pl.BlockSpec((n_sel, n_cmp), lambda i: (0, 0))],
        out_specs=[pl.BlockSpec((Q_BLOCK, N_PAIR * LANES), lambda i: (i, 0)),
                   pl.BlockSpec((1, 2, n_sel, Q_BLOCK), lambda i: (i, 0, 0, 0)),
                   pl.BlockSpec((1, 8, n_sel), lambda i: (i, 0, 0))],
        out_shape=[jax.ShapeDtypeStruct((s, N_PAIR * LANES), F32),
                   jax.ShapeDtypeStruct((nb, 2, n_sel, Q_BLOCK), F32),
                   jax.ShapeDtypeStruct((nb, 8, n_sel), F32)],
        compiler_params=_params("parallel"),
        name="nsa_compressed",
    )(y_att, kc128, vct128, overlap_t)


def _nsa_sel_kernel(flags_ref, q_ref, ks_ref, vst_ref, sel_ref, slope_ref, gc_ref, oc_ref, ow_ref,
                    o_ref, qm_ref, m_ref, l_ref, acc_ref, *, scale, n_sel):
    i = pl.program_id(0)
    n_heads = 2 * N_PAIR
    lo = _half_masks()
    for pair in range(N_PAIR):
        q128 = q_ref[:, pair * LANES:(pair + 1) * LANES]
        for group in range(2):
            h = pair * 2 + group
            qm_ref[h * Q_BLOCK:(h + 1) * Q_BLOCK, :] = jnp.where(
                lo if group == 0 else jnp.logical_not(lo), q128, jnp.zeros_like(q128))
    m_ref[...] = jnp.full(m_ref.shape, NEG, F32)
    l_ref[...] = jnp.zeros(l_ref.shape, F32)
    acc_ref[...] = jnp.zeros(acc_ref.shape, F32)
    top = lax.broadcasted_iota(jnp.int32, (Q_BLOCK, Q_BLOCK), 0) < HALF
    slopes = slope_ref[...]

    def body(p, carry):
        base = i * n_sel + 2 * p
        active = (flags_ref[base] + flags_ref[base + 1]) > 0

        @pl.when(active)
        def _():
            k = ks_ref[p]
            vt = vst_ref[p]
            key_pos = p * Q_BLOCK + lax.broadcasted_iota(jnp.int32, (Q_BLOCK, Q_BLOCK), 0)
            t = i * Q_BLOCK + lax.broadcasted_iota(jnp.int32, (Q_BLOCK, Q_BLOCK), 1)
            dist = t - key_pos
            distf = dist.astype(F32)
            bias = []
            for group in range(2):
                r0 = sel_ref[0, group, pl.ds(2 * p, 1), :]
                r1 = sel_ref[0, group, pl.ds(2 * p + 1, 1), :]
                picked = jnp.where(top, r0, r1)
                bias.append(jnp.where((picked > 0.5) & (dist >= 0), 0.0, NEG))
            bias_all = jnp.concatenate([bias[h % 2] for h in range(n_heads)], axis=1)
            dist_all = jnp.concatenate([distf] * n_heads, axis=1)
            s = _nt(k, qm_ref[...]) * scale - slopes * dist_all + bias_all
            m_old = m_ref[...]
            m_new = jnp.maximum(m_old, jnp.max(s, axis=0, keepdims=True))
            alpha = jnp.exp(m_old - m_new)
            pr = jnp.where(s > 0.5 * NEG, jnp.exp(s - m_new), 0.0)
            l_ref[...] = alpha * l_ref[...] + jnp.sum(pr, axis=0, keepdims=True)
            pv = _dot(vt, pr.astype(BF16))
            own = jnp.concatenate(
                [pv[(h % 2) * HALF:(h % 2 + 1) * HALF, h * Q_BLOCK:(h + 1) * Q_BLOCK]
                 for h in range(n_heads)], axis=1)
            acc_ref[...] = alpha * acc_ref[...] + own
            m_ref[...] = m_new

        return carry

    lax.fori_loop(0, i + 1, body, 0)

    gates = jax.nn.sigmoid(gc_ref[...])
    o_all = acc_ref[...] * (1.0 / jnp.maximum(l_ref[...], 1e-30))
    for pair in range(N_PAIR):
        h = pair * 2
        o_s = jnp.concatenate([o_all[:, h * Q_BLOCK:(h + 1) * Q_BLOCK],
                               o_all[:, (h + 1) * Q_BLOCK:(h + 2) * Q_BLOCK]], axis=0).T
        cols = slice(pair * LANES, (pair + 1) * LANES)

        def gate(branch):
            c = branch * 8 + pair * 2
            return jnp.where(lo, gates[:, c:c + 1], gates[:, c + 1:c + 2])

        y = gate(0) * oc_ref[:, cols] + gate(1) * o_s + gate(2) * ow_ref[:, cols]
        o_ref[:, cols] = y.astype(o_ref.dtype)


def _slope_row():
    row = np.concatenate([np.full((Q_BLOCK,), _head_slope(h // 2, h % 2), np.float32)
                          for h in range(2 * N_PAIR)])
    return jnp.asarray(row.reshape(1, -1))


def nsa_selected(flags, y_att, ks3, vst3, sel_t, y_rest, o_c, o_w):
    s = y_att.shape[0]
    nb = s // Q_BLOCK
    n_sel = sel_t.shape[2]
    n_heads = 2 * N_PAIR
    kern = functools.partial(_nsa_sel_kernel, scale=HALF ** -0.5, n_sel=n_sel)
    wide = pl.BlockSpec((Q_BLOCK, N_PAIR * LANES), lambda i, f: (i, 0))
    grid_spec = pltpu.PrefetchScalarGridSpec(
        num_scalar_prefetch=1,
        grid=(nb,),
        in_specs=[pl.BlockSpec((Q_BLOCK, N_PAIR * LANES), lambda i, f: (i, COL_QC // N_PAIR)),
                  pl.BlockSpec((nb, Q_BLOCK, LANES), lambda i, f: (0, 0, 0)),
                  pl.BlockSpec((nb, LANES, Q_BLOCK), lambda i, f: (0, 0, 0)),
                  pl.BlockSpec((1, 2, n_sel, Q_BLOCK), lambda i, f: (i, 0, 0, 0)),
                  pl.BlockSpec((1, n_heads * Q_BLOCK), lambda i, f: (0, 0)),
                  pl.BlockSpec((Q_BLOCK, LANES), lambda i, f: (i, N_GATES // LANES)),
                  wide, wide],
        out_specs=wide,
        scratch_shapes=[pltpu.VMEM((n_heads * Q_BLOCK, LANES), BF16),
                        pltpu.VMEM((1, n_heads * Q_BLOCK), F32),
                        pltpu.VMEM((1, n_heads * Q_BLOCK), F32),
                        pltpu.VMEM((HALF, n_heads * Q_BLOCK), F32)],
    )
    return pl.pallas_call(
        kern,
        grid_spec=grid_spec,
        out_shape=jax.ShapeDtypeStruct((s, N_PAIR * LANES), BF16),
        compiler_params=_params("arbitrary"),
        name="nsa_selected",
    )(flags, y_att, ks3, vst3, sel_t, _slope_row(), y_rest, o_c, o_w)


def _pair_cols(w):
    lead = w.shape[:-1]
    return w.reshape(*lead, 2, N_PAIR, HALF).swapaxes(-3, -2).reshape(*lead, 2 * N_PAIR * HALF)


def _pair_rows(w):
    return w.reshape(2, N_PAIR, HALF, w.shape[-1]).swapaxes(0, 1).reshape(2 * N_PAIR * HALF, w.shape[-1])


def _prep_w_in(w_in):
    qa = _pair_cols(w_in[:, 0:512])
    qc = _pair_cols(w_in[:, 2304:2816])
    w_att = jnp.concatenate([qa, qc, w_in[:, 768:2304], w_in[:, 512:768], w_in[:, 2816:N_ATT]],
                            axis=1).astype(BF16)
    gc = w_in[:, N_ATT:N_ATT + N_GC].reshape(-1, 2, N_PAIR, 3)
    gc = gc.transpose(0, 3, 2, 1).reshape(-1, N_GC)
    gc = jnp.pad(gc, ((0, 0), (0, LANES - N_GC)))
    w_rest = jnp.concatenate([w_in[:, N_ATT + N_GC:], gc], axis=1).astype(BF16)
    return w_att, w_rest


def _blocks_t(y_att, col, width=1, tile=Q_BLOCK):
    s = y_att.shape[0]
    t = y_att[:, col * LANES:(col + width) * LANES].reshape(s // tile, tile, width, LANES)
    return t.transpose(2, 0, 3, 1)


def _blocks(y_att, col, width=1, tile=Q_BLOCK):
    s = y_att.shape[0]
    t = y_att[:, col * LANES:(col + width) * LANES].reshape(s // tile, tile, width, LANES)
    return t.transpose(2, 0, 1, 3)


def _chunk_rows(y_att, col):
    s = y_att.shape[0]
    t = y_att[:, col * LANES:(col + 1) * LANES].reshape(s // CMP_STRIDE, CMP_STRIDE, 2, HALF)
    return t.transpose(2, 0, 1, 3).reshape(2, s // CMP_STRIDE, CMP_STRIDE * HALF)


def _overlap_t(s):
    n_cmp = s // CMP_STRIDE
    n_sel = s // SEL_BLOCK
    cmp_lo = np.arange(n_cmp) * CMP_STRIDE
    cmp_end = cmp_lo + CMP_BLOCK - 1
    sel_lo = np.arange(n_sel) * SEL_BLOCK
    ov = ((cmp_lo[None, :] <= sel_lo[:, None] + SEL_BLOCK - 1)
          & (cmp_end[None, :] >= sel_lo[:, None]) & (np.arange(n_cmp)[None, :] < n_cmp - 1))
    return jnp.asarray(ov.astype(np.float32), dtype=BF16)


def nsa_attention(y_att, y_rest, pe_k, w1_k, w2_k, pe_v, w1_v, w2_v):
    s = y_att.shape[0]
    kc = compress_blocks(_chunk_rows(y_att, COL_KC), pe_k, w1_k, w2_k)
    vc = compress_blocks(_chunk_rows(y_att, COL_VC), pe_v, w1_v, w2_v)
    kc128 = kc.transpose(1, 0, 2).reshape(kc.shape[1], LANES)
    vct128 = vc.transpose(0, 2, 1).reshape(LANES, vc.shape[1])
    o_c, sel_t, cnt = nsa_compressed(y_att, kc128, vct128, _overlap_t(s))
    flags = (cnt[:, 0, :] > 0.5).astype(jnp.int32).reshape(-1)
    o_w = banded_attention(y_att, _blocks_t(y_att, COL_VW)[0], COL_QC, COL_KW,
                           NSA_WINDOW, None, F32)
    ks3 = _blocks(y_att, COL_KS)[0]
    vst3 = _blocks_t(y_att, COL_VS)[0]
    return nsa_selected(flags, y_att, ks3, vst3, sel_t, y_rest, o_c, o_w)


def mixer_layer(x, norm_mix, w_in, sinks, pe_k, w1_k, w2_k, pe_v, w1_v, w2_v,
                w_br_a, w_br_b, w_br_c, w_out):
    w_att, w_rest = _prep_w_in(w_in)
    y_att = norm_matmul(x, norm_mix, w_att, BF16, 512, 896)
    y_rest = norm_matmul(x, norm_mix, w_rest, F32, 512, 896)
    y_a = banded_attention(y_att, _blocks_t(y_att, COL_VA)[0], COL_QA, COL_KA,
                           SWA_WINDOW, sinks, BF16)
    y_b = stick_breaking(y_att, _blocks(y_att, COL_KB, SB_HEADS, SB_TILE),
                         _blocks_t(y_att, COL_VB, SB_HEADS, SB_TILE))
    y_c = nsa_attention(y_att, y_rest, pe_k, w1_k, w2_k, pe_v, w1_v, w2_v)
    merged = merge_branches(y_a, y_b, y_c, _pair_rows(w_br_a).astype(BF16),
                            w_br_b.astype(BF16), _pair_rows(w_br_c).astype(BF16),
                            y_rest, 512, 512)
    return matmul_residual(merged, w_out.astype(BF16), x, 512, 512)


def ffn_layer(x, norm_ffn, w_gate, w_up, w_down):
    u = ffn_up(x, norm_ffn, w_gate.astype(BF16), w_up.astype(BF16), 512, 512)
    return matmul_residual(u, w_down.astype(BF16), x, 512, 512)


def kernel(x, norm_mix, w_in, swa_sinks, cmp_pe_k, cmp_w1_k, cmp_w2_k, cmp_pe_v, cmp_w1_v,
           cmp_w2_v, w_branch_swa, w_branch_sb, w_branch_nsa, w_out, norm_ffn, w_gate, w_up,
           w_down, norm_final):
    b, s, d = x.shape
    outs = []
    for bi in range(b):
        xb = x[bi]
        for layer in range(norm_mix.shape[0]):
            xb = mixer_layer(xb, norm_mix[layer], w_in[layer], swa_sinks[layer],
                             cmp_pe_k[layer], cmp_w1_k[layer], cmp_w2_k[layer],
                             cmp_pe_v[layer], cmp_w1_v[layer], cmp_w2_v[layer],
                             w_branch_swa[layer], w_branch_sb[layer], w_branch_nsa[layer],
                             w_out[layer])
            xb = ffn_layer(xb, norm_ffn[layer], w_gate[layer], w_up[layer], w_down[layer])
        outs.append(rms_norm_rows(xb, norm_final, 512))
    return jnp.stack(outs, axis=0)
```

```python
import functools

import jax
import jax.numpy as jnp
import numpy as np
from jax import lax
from jax.experimental import pallas as pl
from jax.experimental.pallas import tpu as pltpu

F32 = jnp.float32
BF16 = jnp.bfloat16

D_MODEL = 2048
Q_BLOCK = 128
LANES = 128
HALF = 64
N_PAIR = 4
SWA_WINDOW = 128
NSA_WINDOW = 512
CMP_BLOCK = 32
CMP_STRIDE = 16
CMP_HIDDEN = 256
CMP_CHUNK = 256
SEL_BLOCK = 64
SEL_TOPK = 8
SB_HEADS = 4
SB_TILE = 256
D_FF = 5632
NEG = -1e30
SEL_FORCE = 1e9
SEL_TAKEN = -3e38
RMS_EPS = 1e-6
LOG2E = 1.4426950408889634
SB_DEAD_BITS = 150.0
VMEM_LIMIT = 56 * 1024 * 1024

N_ATT = 3584
N_GC = 24
N_GATES = 3 * D_MODEL
IN_PROJ_TM = 1024
IN_PROJ_TN = 512
ROW_TM = 256
COL_QA, COL_QC, COL_QB, COL_KB, COL_VB = 0, 4, 8, 12, 16
COL_KA, COL_VA, COL_KC, COL_VC, COL_KS, COL_VS, COL_KW, COL_VW = 20, 21, 22, 23, 24, 25, 26, 27


def _nt(a, b):
    return lax.dot_general(a, b, (((1,), (1,)), ((), ())), preferred_element_type=F32)


def _dot(a, b):
    return jnp.dot(a, b, preferred_element_type=F32)


def _params(*sem):
    return pltpu.CompilerParams(dimension_semantics=sem, vmem_limit_bytes=VMEM_LIMIT)


def _head_slope(pair, group):
    return 2.0 ** -(group * N_PAIR + pair + 1)


def _in_proj_kernel(x_ref, g_ref, w_ref, oa_ref, or_ref, h_ref, *, att_tiles):
    j = pl.program_id(1)

    @pl.when(j == 0)
    def _():
        x = x_ref[...]
        ms = jnp.mean(x * x, axis=-1, keepdims=True)
        h_ref[...] = (x * lax.rsqrt(ms + RMS_EPS) * g_ref[...]).astype(BF16)

    y = _dot(h_ref[...], w_ref[...])

    @pl.when(j < att_tiles)
    def _():
        oa_ref[...] = y.astype(oa_ref.dtype)

    @pl.when(j >= att_tiles)
    def _():
        or_ref[...] = y


def in_projection(x, g, w_all, tm, tn):
    s, d = x.shape
    att_tiles = N_ATT // tn
    rest_tiles = w_all.shape[1] // tn - att_tiles
    kern = functools.partial(_in_proj_kernel, att_tiles=att_tiles)
    return pl.pallas_call(
        kern,
        grid=(s // tm, att_tiles + rest_tiles),
        in_specs=[pl.BlockSpec((tm, d), lambda i, j: (i, 0)),
                  pl.BlockSpec((1, d), lambda i, j: (0, 0)),
                  pl.BlockSpec((d, tn), lambda i, j: (0, j))],
        out_specs=[pl.BlockSpec((tm, tn), lambda i, j: (i, jnp.minimum(j, att_tiles - 1))),
                   pl.BlockSpec((tm, tn), lambda i, j: (i, jnp.maximum(j - att_tiles, 0)))],
        out_shape=[jax.ShapeDtypeStruct((s, N_ATT), BF16),
                   jax.ShapeDtypeStruct((s, rest_tiles * tn), F32)],
        scratch_shapes=[pltpu.VMEM((tm, d), BF16)],
        compiler_params=_params("parallel", "arbitrary"),
        name="in_projection",
    )(x, g.reshape(1, d), w_all)


def _mm_res_kernel(a_ref, w_ref, r_ref, *rest, final_norm):
    y = r_ref[...] + _dot(a_ref[...], w_ref[...])
    if final_norm:
        g_ref, o_ref = rest
        ms = jnp.mean(y * y, axis=-1, keepdims=True)
        y = y * lax.rsqrt(ms + RMS_EPS) * g_ref[...]
    else:
        o_ref, = rest
    o_ref[...] = y


def matmul_residual(a, w, res, tm, norm_g=None):
    s, k = a.shape
    n = w.shape[1]
    in_specs = [pl.BlockSpec((tm, k), lambda i: (i, 0)),
                pl.BlockSpec((k, n), lambda i: (0, 0), pipeline_mode=pl.Buffered(1)),
                pl.BlockSpec((tm, n), lambda i: (i, 0))]
    args = [a, w, res]
    if norm_g is not None:
        in_specs.append(pl.BlockSpec((1, n), lambda i: (0, 0)))
        args.append(norm_g.reshape(1, n))
    return pl.pallas_call(
        functools.partial(_mm_res_kernel, final_norm=norm_g is not None),
        grid=(s // tm,),
        in_specs=in_specs,
        out_specs=pl.BlockSpec((tm, n), lambda i: (i, 0)),
        out_shape=jax.ShapeDtypeStruct((s, n), F32),
        compiler_params=_params("parallel"),
        name="matmul_residual",
    )(*args)


def _merge_out_kernel(ya_ref, yb_ref, yc_ref, wa_ref, wb_ref, wc_ref,
                      ga_ref, gb_ref, gc_ref, wo_ref, x_ref, o_ref):
    m = jax.nn.sigmoid(ga_ref[...]) * _dot(ya_ref[...], wa_ref[...])
    m = m + jax.nn.sigmoid(gb_ref[...]) * _dot(yb_ref[...], wb_ref[...])
    m = m + jax.nn.sigmoid(gc_ref[...]) * _dot(yc_ref[...], wc_ref[...])
    o_ref[...] = x_ref[...] + _dot(m.astype(BF16), wo_ref[...])


def merge_out_projection(ya, yb, yc, wa, wb, wc, y_rest, w_out, x, tm):
    s, k = ya.shape
    n = wa.shape[1]
    resident = pl.Buffered(1)
    y_spec = pl.BlockSpec((tm, k), lambda i: (i, 0))
    w_spec = pl.BlockSpec((k, n), lambda i: (0, 0), pipeline_mode=resident)

    def gate_spec(br):
        return pl.BlockSpec((tm, n), lambda i: (i, br))

    return pl.pallas_call(
        _merge_out_kernel,
        grid=(s // tm,),
        in_specs=[y_spec, y_spec, y_spec, w_spec, w_spec, w_spec,
                  gate_spec(0), gate_spec(1), gate_spec(2),
                  pl.BlockSpec((n, n), lambda i: (0, 0), pipeline_mode=resident),
                  pl.BlockSpec((tm, n), lambda i: (i, 0))],
        out_specs=pl.BlockSpec((tm, n), lambda i: (i, 0)),
        out_shape=jax.ShapeDtypeStruct((s, n), F32),
        compiler_params=_params("parallel"),
        name="merge_out_projection",
    )(ya, yb, yc, wa, wb, wc, y_rest, y_rest, y_rest, w_out, x)


def _ffn_up_kernel(x_ref, g_ref, wg_ref, wu_ref, o_ref, h_ref):
    @pl.when(pl.program_id(1) == 0)
    def _():
        x = x_ref[...]
        ms = jnp.mean(x * x, axis=-1, keepdims=True)
        h_ref[...] = (x * lax.rsqrt(ms + RMS_EPS) * g_ref[...]).astype(BF16)

    h = h_ref[...]
    o_ref[...] = (jax.nn.silu(_dot(h, wg_ref[...])) * _dot(h, wu_ref[...])).astype(o_ref.dtype)


def ffn_up(x, g, wg, wu, tm, tn):
    s, d = x.shape
    n = wg.shape[1]
    w_spec = pl.BlockSpec((d, tn), lambda i, j: (0, j))
    return pl.pallas_call(
        _ffn_up_kernel,
        grid=(s // tm, n // tn),
        in_specs=[pl.BlockSpec((tm, d), lambda i, j: (i, 0)),
                  pl.BlockSpec((1, d), lambda i, j: (0, 0)),
                  w_spec, w_spec],
        out_specs=pl.BlockSpec((tm, tn), lambda i, j: (i, j)),
        out_shape=jax.ShapeDtypeStruct((s, n), BF16),
        scratch_shapes=[pltpu.VMEM((tm, d), BF16)],
        compiler_params=_params("parallel", "arbitrary"),
        name="ffn_up",
    )(x, g.reshape(1, d), wg, wu)


def _half_masks():
    lane = lax.broadcasted_iota(jnp.int32, (Q_BLOCK, LANES), 1)
    return lane < HALF


def _banded_kernel(*refs, n_prev, window, has_sink, scale):
    nk = n_prev + 1
    if has_sink:
        sink_ref, refs = refs[0], refs[1:]
    q_ref = refs[0]
    k_refs = refs[1:1 + nk]
    v_refs = refs[1 + nk:1 + 2 * nk]
    o_ref = refs[1 + 2 * nk]
    i = pl.program_id(0)
    span = nk * Q_BLOCK
    k_all = jnp.concatenate([k_refs[d][...] for d in range(n_prev, -1, -1)], axis=0)
    vt_all = jnp.concatenate([v_refs[d][0] for d in range(n_prev, -1, -1)], axis=1)
    key_rel = lax.broadcasted_iota(jnp.int32, (span, Q_BLOCK), 0)
    q_rel = lax.broadcasted_iota(jnp.int32, (span, Q_BLOCK), 1)
    dist = q_rel + n_prev * Q_BLOCK - key_rel
    key_pos = (i - n_prev) * Q_BLOCK + key_rel
    mask = (dist >= 0) & (dist < window) & (key_pos >= 0)
    distf = dist.astype(F32)
    lo = _half_masks()
    top = lax.broadcasted_iota(jnp.int32, (Q_BLOCK, Q_BLOCK), 0) < HALF
    for pair in range(N_PAIR):
        q128 = q_ref[:, pair * LANES:(pair + 1) * LANES]
        halves = []
        for group in range(2):
            qm = jnp.where(lo if group == 0 else jnp.logical_not(lo), q128, jnp.zeros_like(q128))
            s = _nt(k_all, qm) * scale - _head_slope(pair, group) * distf
            s = jnp.where(mask, s, NEG)
            m = jnp.max(s, axis=0, keepdims=True)
            if has_sink:
                sink = sink_ref[group * N_PAIR + pair]
                m = jnp.maximum(m, sink)
                p = jnp.exp(s - m)
                denom = jnp.sum(p, axis=0, keepdims=True) + jnp.exp(sink - m)
            else:
                p = jnp.where(mask, jnp.exp(s - m), 0.0)
                denom = jnp.maximum(jnp.sum(p, axis=0, keepdims=True), 1e-30)
            pv = _dot(vt_all, p.astype(BF16))
            halves.append(pv * (1.0 / denom))
        ot = jnp.where(top, halves[0], halves[1])
        o_ref[:, pair * LANES:(pair + 1) * LANES] = ot.T.astype(o_ref.dtype)


def banded_attention(y_att, vt3, col_q, col_k, window, sinks, out_dtype):
    s = y_att.shape[0]
    nb = s // Q_BLOCK
    n_prev = -(-window // Q_BLOCK)
    has_sink = sinks is not None
    in_specs = []
    args = []
    if has_sink:
        in_specs.append(pl.BlockSpec(memory_space=pltpu.SMEM))
        args.append(sinks)
    in_specs.append(pl.BlockSpec((Q_BLOCK, N_PAIR * LANES), lambda i: (i, col_q // N_PAIR)))
    args.append(y_att)
    for d in range(n_prev + 1):
        in_specs.append(pl.BlockSpec((Q_BLOCK, LANES),
                                     lambda i, d=d: (jnp.maximum(i - d, 0), col_k)))
        args.append(y_att)
    for d in range(n_prev + 1):
        in_specs.append(pl.BlockSpec((1, LANES, Q_BLOCK),
                                     lambda i, d=d: (jnp.maximum(i - d, 0), 0, 0)))
        args.append(vt3)
    kern = functools.partial(_banded_kernel, n_prev=n_prev, window=window,
                             has_sink=has_sink, scale=HALF ** -0.5)
    return pl.pallas_call(
        kern,
        grid=(nb,),
        in_specs=in_specs,
        out_specs=pl.BlockSpec((Q_BLOCK, N_PAIR * LANES), lambda i: (i, 0)),
        out_shape=jax.ShapeDtypeStruct((s, N_PAIR * LANES), out_dtype),
        compiler_params=_params("parallel"),
        name="banded_attention_w%d" % window,
    )(*args)


def _sb_kernel(q_ref, k_ref, vt_ref, o_ref, ls_ref, acc_ref, *, scale, heads):
    i = pl.program_id(1)
    row = lax.broadcasted_iota(jnp.int32, (SB_TILE, SB_TILE), 0)
    col = lax.broadcasted_iota(jnp.int32, (SB_TILE, SB_TILE), 1)
    later = jnp.where(col > row, 1.0, 0.0).astype(BF16)
    before = row < col

    def visit(kj, diagonal):
        gone = [ls_ref[h] for h in range(heads)]
        pv, gone_new = [], []
        for h in range(heads):
            q = q_ref[:, h * LANES:(h + 1) * LANES]
            zz = _nt(k_ref[h, kj], q) * (scale * LOG2E)
            sp_raw = jnp.maximum(zz, 0.0) + jnp.log(1.0 + jnp.exp2(-jnp.abs(zz))) * LOG2E
            sp = jnp.where(before, sp_raw, 0.0) if diagonal else sp_raw
            a = jnp.exp2((zz - sp_raw) - _dot(later, sp.astype(BF16)) - gone[h])
            if diagonal:
                a = jnp.where(before, a, 0.0)
            pv.append(_dot(vt_ref[h, kj], a.astype(BF16)))
            gone_new.append(gone[h] + jnp.sum(sp, axis=0, keepdims=True))
        alive = gone_new[0]
        for h in range(heads):
            acc_ref[h] += pv[h]
            ls_ref[h] = gone_new[h]
            alive = jnp.minimum(alive, gone_new[h])
        return jnp.min(alive)

    ls_ref[...] = jnp.zeros(ls_ref.shape, F32)
    acc_ref[...] = jnp.zeros(acc_ref.shape, F32)
    alive0 = visit(i, True)

    def cond(carry):
        return (carry[0] < i) & (carry[1] < SB_DEAD_BITS)

    def body(carry):
        return carry[0] + 1, visit(i - 1 - carry[0], False)

    lax.while_loop(cond, body, (jnp.int32(0), alive0))
    for h in range(heads):
        o_ref[:, h * LANES:(h + 1) * LANES] = acc_ref[h].T.astype(o_ref.dtype)


def stick_breaking(y_att, k4, vt4, heads=SB_HEADS):
    s = y_att.shape[0]
    nt = s // SB_TILE
    kern = functools.partial(_sb_kernel, scale=LANES ** -0.5, heads=heads)
    resident = pl.Buffered(1)
    return pl.pallas_call(
        kern,
        grid=(SB_HEADS // heads, nt),
        in_specs=[pl.BlockSpec((SB_TILE, heads * LANES), lambda g, i: (i, COL_QB // heads + g)),
                  pl.BlockSpec((heads, nt, SB_TILE, LANES), lambda g, i: (g, 0, 0, 0),
                               pipeline_mode=resident),
                  pl.BlockSpec((heads, nt, LANES, SB_TILE), lambda g, i: (g, 0, 0, 0),
                               pipeline_mode=resident)],
        out_specs=pl.BlockSpec((SB_TILE, heads * LANES), lambda g, i: (i, g)),
        out_shape=jax.ShapeDtypeStruct((s, SB_HEADS * LANES), BF16),
        scratch_shapes=[pltpu.VMEM((heads, 1, SB_TILE), F32),
                        pltpu.VMEM((heads, LANES, SB_TILE), F32)],
        compiler_params=_params("parallel", "arbitrary"),
        name="stick_breaking",
    )(y_att, k4, vt4)


def _compress_kernel(t_ref, pe_ref, w1_ref, w2_ref, o_ref):
    t = t_ref[...]
    half = w1_ref.shape[0] // 2
    w1 = w1_ref[...]
    a = _dot(t, w1[:half])
    b = _dot(t, w1[half:])
    bias = _dot(pe_ref[...], w1)[0:1]
    n = t.shape[0]
    pre = a + pltpu.roll(b, n - 1, 0) + bias
    hid = jax.nn.gelu(pre)
    o_ref[...] = _dot(hid.astype(BF16), w2_ref[...]).astype(o_ref.dtype)


def compress_blocks(t_flat, pe, w1, w2):
    g, n, k = t_flat.shape
    pe_rows = jnp.zeros((8, 2 * k), BF16).at[0].set(pe.reshape(-1).astype(BF16))
    return pl.pallas_call(
        _compress_kernel,
        grid=(g,),
        in_specs=[pl.BlockSpec((None, n, k), lambda gi: (gi, 0, 0)),
                  pl.BlockSpec((8, 2 * k), lambda gi: (0, 0)),
                  pl.BlockSpec((2 * k, CMP_HIDDEN), lambda gi: (0, 0)),
                  pl.BlockSpec((CMP_HIDDEN, HALF), lambda gi: (0, 0))],
        out_specs=pl.BlockSpec((None, n, HALF), lambda gi: (gi, 0, 0)),
        out_shape=jax.ShapeDtypeStruct((g, n, HALF), BF16),
        compiler_params=_params("parallel"),
        name="compress_blocks",
    )(t_flat, pe_rows, w1.astype(BF16), w2.astype(BF16))


def _nsa_cmp_kernel(q_ref, kc_ref, vct_ref, ov_ref, oc_ref, sel_ref, cnt_ref, *, scale):
    i = pl.program_id(0)
    n_all = kc_ref.shape[0]
    chunk = min(CMP_CHUNK, n_all)
    needed = lax.div(8 * i + 7 + (chunk - 1), chunk)
    body = functools.partial(_nsa_cmp_body, q_ref, kc_ref, vct_ref, ov_ref, oc_ref, sel_ref,
                             cnt_ref, scale=scale)
    for c in range(1, n_all // chunk + 1):
        pl.when(needed == c)(functools.partial(body, n_cmp=c * chunk))


def _nsa_cmp_body(q_ref, kc_ref, vct_ref, ov_ref, oc_ref, sel_ref, cnt_ref, *, scale, n_cmp):
    i = pl.program_id(0)
    n_sel = ov_ref.shape[0]
    kc = kc_ref[:n_cmp, :]
    vct = vct_ref[:, :n_cmp]
    t = i * Q_BLOCK + lax.broadcasted_iota(jnp.int32, (n_cmp, Q_BLOCK), 1)
    cmp_end = lax.broadcasted_iota(jnp.int32, (n_cmp, Q_BLOCK), 0) * CMP_STRIDE + (CMP_BLOCK - 1)
    dist = t - cmp_end
    mask = dist >= 0
    distf = dist.astype(F32)
    lo = _half_masks()
    top = lax.broadcasted_iota(jnp.int32, (Q_BLOCK, Q_BLOCK), 0) < HALF
    psum = [jnp.zeros((n_cmp, Q_BLOCK), F32), jnp.zeros((n_cmp, Q_BLOCK), F32)]
    for pair in range(N_PAIR):
        q128 = q_ref[:, pair * LANES:(pair + 1) * LANES]
        halves = []
        for group in range(2):
            qm = jnp.where(lo if group == 0 else jnp.logical_not(lo), q128, jnp.zeros_like(q128))
            s = _nt(kc, qm) * scale - _head_slope(pair, group) * distf
            s = jnp.where(mask, s, NEG)
            m = jnp.max(s, axis=0, keepdims=True)
            p = jnp.where(mask, jnp.exp(s - m), 0.0)
            p = p * (1.0 / jnp.maximum(jnp.sum(p, axis=0, keepdims=True), 1e-30))
            halves.append(_dot(vct, p.astype(BF16)))
            psum[group] = psum[group] + p
        ot = jnp.where(top, halves[0], halves[1])
        oc_ref[:, pair * LANES:(pair + 1) * LANES] = ot.T

    ov = ov_ref[:, :n_cmp]
    tq = i * Q_BLOCK + lax.broadcasted_iota(jnp.int32, (n_sel, Q_BLOCK), 1)
    jblk = lax.broadcasted_iota(jnp.int32, (n_sel, Q_BLOCK), 0)
    jblk_f = jblk.astype(F32)
    jt = tq // SEL_BLOCK
    valid = jblk * SEL_BLOCK <= tq
    forced = (jblk == 0) | (jblk == jt) | (jblk == jt - 1)
    total = jnp.zeros((n_sel, Q_BLOCK), F32)
    for group in range(2):
        hi = psum[group].astype(BF16)
        lo_part = (psum[group] - hi.astype(F32)).astype(BF16)
        imp = _dot(ov, hi) + _dot(ov, lo_part)
        score = jnp.where(valid, jnp.where(forced, SEL_FORCE, imp), -SEL_FORCE)
        sel = jnp.zeros((n_sel, Q_BLOCK), F32)
        for _ in range(min(SEL_TOPK, n_sel)):
            best = jnp.max(score, axis=0, keepdims=True)
            first = jnp.min(jnp.where(score == best, jblk_f, float(n_sel)), axis=0, keepdims=True)
            hit = jblk_f == first
            sel = jnp.where(hit, 1.0, sel)
            score = jnp.where(hit, SEL_TAKEN, score)
        sel_ref[0, group] = sel
        total = total + sel
    cnt_ref[0] = _nt(jnp.ones((8, Q_BLOCK), BF16), total.astype(BF16))


def nsa_compressed(y_att, kc128, vct128, overlap_t):
    s = y_att.shape[0]
    nb = s // Q_BLOCK
    n_cmp = kc128.shape[0]
    n_sel = overlap_t.shape[0]
    kern = functools.partial(_nsa_cmp_kernel, scale=HALF ** -0.5)
    return pl.pallas_call(
        kern,
        grid=(nb,),
        in_specs=[pl.BlockSpec((Q_BLOCK, N_PAIR * LANES), lambda i: (i, COL_QC // N_PAIR)),
                  pl.BlockSpec((n_cmp, LANES), lambda i: (0, 0)),
                  pl.BlockSpec((LANES, n_cmp), lambda i: (0, 0)),
                  pl.BlockSpec((n_sel, n_cmp), lambda i: (0, 0))],
        out_specs=[pl.BlockSpec((Q_BLOCK, N_PAIR * LANES), lambda i: (i, 0)),
                   pl.BlockSpec((1, 2, n_sel, Q_BLOCK), lambda i: (i, 0, 0, 0)),
                   pl.BlockSpec((1, 8, n_sel), lambda i: (i, 0, 0))],
        out_shape=[jax.ShapeDtypeStruct((s, N_PAIR * LANES), F32),
                   jax.ShapeDtypeStruct((nb, 2, n_sel, Q_BLOCK), F32),
                   jax.ShapeDtypeStruct((nb, 8, n_sel), F32)],
        compiler_params=_params("parallel"),
        name="nsa_compressed",
    )(y_att, kc128, vct128, overlap_t)


def _nsa_sel_kernel(flags_ref, q_ref, ks_ref, vst_ref, sel_ref, slope_ref, gc_ref, oc_ref, ow_ref,
                    o_ref, qm_ref, m_ref, l_ref, acc_ref, *, scale, n_sel):
    i = pl.program_id(0)
    n_heads = 2 * N_PAIR
    lo = _half_masks()
    for pair in range(N_PAIR):
        q128 = q_ref[:, pair * LANES:(pair + 1) * LANES]
        for group in range(2):
            h = pair * 2 + group
            qm_ref[h * Q_BLOCK:(h + 1) * Q_BLOCK, :] = jnp.where(
                lo if group == 0 else jnp.logical_not(lo), q128, jnp.zeros_like(q128))
    m_ref[...] = jnp.full(m_ref.shape, NEG, F32)
    l_ref[...] = jnp.zeros(l_ref.shape, F32)
    acc_ref[...] = jnp.zeros(acc_ref.shape, F32)
    top = lax.broadcasted_iota(jnp.int32, (Q_BLOCK, Q_BLOCK), 0) < HALF
    slopes = slope_ref[...]

    def body(p, carry):
        base = i * n_sel + 2 * p
        active = (flags_ref[base] + flags_ref[base + 1]) > 0

        @pl.when(active)
        def _():
            k = ks_ref[p]
            vt = vst_ref[p]
            key_pos = p * Q_BLOCK + lax.broadcasted_iota(jnp.int32, (Q_BLOCK, Q_BLOCK), 0)
            t = i * Q_BLOCK + lax.broadcasted_iota(jnp.int32, (Q_BLOCK, Q_BLOCK), 1)
            dist = t - key_pos
            distf = dist.astype(F32)
            bias = []
            for group in range(2):
                r0 = sel_ref[0, group, pl.ds(2 * p, 1), :]
                r1 = sel_ref[0, group, pl.ds(2 * p + 1, 1), :]
                picked = jnp.where(top, r0, r1)
                bias.append(jnp.where((picked > 0.5) & (dist >= 0), 0.0, NEG))
            bias_all = jnp.concatenate([bias[h % 2] for h in range(n_heads)], axis=1)
            dist_all = jnp.concatenate([distf] * n_heads, axis=1)
            s = _nt(k, qm_ref[...]) * scale - slopes * dist_all + bias_all
            m_old = m_ref[...]
            m_new = jnp.maximum(m_old, jnp.max(s, axis=0, keepdims=True))
            alpha = jnp.exp(m_old - m_new)
            pr = jnp.where(s > 0.5 * NEG, jnp.exp(s - m_new), 0.0)
            l_ref[...] = alpha * l_ref[...] + jnp.sum(pr, axis=0, keepdims=True)
            pv = _dot(vt, pr.astype(BF16))
            own = jnp.concatenate(
                [pv[(h % 2) * HALF:(h % 2 + 1) * HALF, h * Q_BLOCK:(h + 1) * Q_BLOCK]
                 for h in range(n_heads)], axis=1)
            acc_ref[...] = alpha * acc_ref[...] + own
            m_ref[...] = m_new

        return carry

    lax.fori_loop(0, i + 1, body, 0)

    gates = jax.nn.sigmoid(gc_ref[...])
    o_all = acc_ref[...] * (1.0 / jnp.maximum(l_ref[...], 1e-30))
    for pair in range(N_PAIR):
        h = pair * 2
        o_s = jnp.concatenate([o_all[:, h * Q_BLOCK:(h + 1) * Q_BLOCK],
                               o_all[:, (h + 1) * Q_BLOCK:(h + 2) * Q_BLOCK]], axis=0).T
        cols = slice(pair * LANES, (pair + 1) * LANES)

        def gate(branch):
            c = branch * 8 + pair * 2
            return jnp.where(lo, gates[:, c:c + 1], gates[:, c + 1:c + 2])

        y = gate(0) * oc_ref[:, cols] + gate(1) * o_s + gate(2) * ow_ref[:, cols]
        o_ref[:, cols] = y.astype(o_ref.dtype)


def _slope_row():
    row = np.concatenate([np.full((Q_BLOCK,), _head_slope(h // 2, h % 2), np.float32)
                          for h in range(2 * N_PAIR)])
    return jnp.asarray(row.reshape(1, -1))


def nsa_selected(flags, y_att, ks3, vst3, sel_t, y_rest, o_c, o_w):
    s = y_att.shape[0]
    nb = s // Q_BLOCK
    n_sel = sel_t.shape[2]
    n_heads = 2 * N_PAIR
    kern = functools.partial(_nsa_sel_kernel, scale=HALF ** -0.5, n_sel=n_sel)
    wide = pl.BlockSpec((Q_BLOCK, N_PAIR * LANES), lambda i, f: (i, 0))
    grid_spec = pltpu.PrefetchScalarGridSpec(
        num_scalar_prefetch=1,
        grid=(nb,),
        in_specs=[pl.BlockSpec((Q_BLOCK, N_PAIR * LANES), lambda i, f: (i, COL_QC // N_PAIR)),
                  pl.BlockSpec((nb, Q_BLOCK, LANES), lambda i, f: (0, 0, 0)),
                  pl.BlockSpec((nb, LANES, Q_BLOCK), lambda i, f: (0, 0, 0)),
                  pl.BlockSpec((1, 2, n_sel, Q_BLOCK), lambda i, f: (i, 0, 0, 0)),
                  pl.BlockSpec((1, n_heads * Q_BLOCK), lambda i, f: (0, 0)),
                  pl.BlockSpec((Q_BLOCK, LANES), lambda i, f: (i, N_GATES // LANES)),
                  wide, wide],
        out_specs=wide,
        scratch_shapes=[pltpu.VMEM((n_heads * Q_BLOCK, LANES), BF16),
                        pltpu.VMEM((1, n_heads * Q_BLOCK), F32),
                        pltpu.VMEM((1, n_heads * Q_BLOCK), F32),
                        pltpu.VMEM((HALF, n_heads * Q_BLOCK), F32)],
    )
    return pl.pallas_call(
        kern,
        grid_spec=grid_spec,
        out_shape=jax.ShapeDtypeStruct((s, N_PAIR * LANES), BF16),
        compiler_params=_params("arbitrary"),
        name="nsa_selected",
    )(flags, y_att, ks3, vst3, sel_t, _slope_row(), y_rest, o_c, o_w)


def _pair_cols(w):
    lead = w.shape[:-1]
    return w.reshape(*lead, 2, N_PAIR, HALF).swapaxes(-3, -2).reshape(*lead, 2 * N_PAIR * HALF)


def _pair_rows(w):
    return w.reshape(2, N_PAIR, HALF, w.shape[-1]).swapaxes(0, 1).reshape(2 * N_PAIR * HALF, w.shape[-1])


def _prep_w_in(w_in):
    qa = _pair_cols(w_in[:, 0:512])
    qc = _pair_cols(w_in[:, 2304:2816])
    w_att = jnp.concatenate([qa, qc, w_in[:, 768:2304], w_in[:, 512:768], w_in[:, 2816:N_ATT]],
                            axis=1).astype(BF16)
    gc = w_in[:, N_ATT:N_ATT + N_GC].reshape(-1, 2, N_PAIR, 3)
    gc = gc.transpose(0, 3, 2, 1).reshape(-1, N_GC)
    gc = jnp.pad(gc, ((0, 0), (0, IN_PROJ_TN - N_GC)))
    return jnp.concatenate([w_att, w_in[:, N_ATT + N_GC:].astype(BF16), gc.astype(BF16)], axis=1)


def _blocks_t(y_att, col, width=1, tile=Q_BLOCK):
    s = y_att.shape[0]
    t = y_att[:, col * LANES:(col + width) * LANES].reshape(s // tile, tile, width, LANES)
    return t.transpose(2, 0, 3, 1)


def _blocks(y_att, col, width=1, tile=Q_BLOCK):
    s = y_att.shape[0]
    t = y_att[:, col * LANES:(col + width) * LANES].reshape(s // tile, tile, width, LANES)
    return t.transpose(2, 0, 1, 3)


def _chunk_rows(y_att, col):
    s = y_att.shape[0]
    t = y_att[:, col * LANES:(col + 1) * LANES].reshape(s // CMP_STRIDE, CMP_STRIDE, 2, HALF)
    return t.transpose(2, 0, 1, 3).reshape(2, s // CMP_STRIDE, CMP_STRIDE * HALF)


def _overlap_t(s):
    n_cmp = s // CMP_STRIDE
    n_sel = s // SEL_BLOCK
    cmp_lo = np.arange(n_cmp) * CMP_STRIDE
    cmp_end = cmp_lo + CMP_BLOCK - 1
    sel_lo = np.arange(n_sel) * SEL_BLOCK
    ov = ((cmp_lo[None, :] <= sel_lo[:, None] + SEL_BLOCK - 1)
          & (cmp_end[None, :] >= sel_lo[:, None]) & (np.arange(n_cmp)[None, :] < n_cmp - 1))
    return jnp.asarray(ov.astype(np.float32), dtype=BF16)


def nsa_attention(y_att, y_rest, pe_k, w1_k, w2_k, pe_v, w1_v, w2_v):
    s = y_att.shape[0]
    kc = compress_blocks(_chunk_rows(y_att, COL_KC), pe_k, w1_k, w2_k)
    vc = compress_blocks(_chunk_rows(y_att, COL_VC), pe_v, w1_v, w2_v)
    kc128 = kc.transpose(1, 0, 2).reshape(kc.shape[1], LANES)
    vct128 = vc.transpose(0, 2, 1).reshape(LANES, vc.shape[1])
    o_c, sel_t, cnt = nsa_compressed(y_att, kc128, vct128, _overlap_t(s))
    flags = (cnt[:, 0, :] > 0.5).astype(jnp.int32).reshape(-1)
    o_w = banded_attention(y_att, _blocks_t(y_att, COL_VW)[0], COL_QC, COL_KW,
                           NSA_WINDOW, None, F32)
    ks3 = _blocks(y_att, COL_KS)[0]
    vst3 = _blocks_t(y_att, COL_VS)[0]
    return nsa_selected(flags, y_att, ks3, vst3, sel_t, y_rest, o_c, o_w)


def mixer_layer(x, norm_mix, w_in, sinks, pe_k, w1_k, w2_k, pe_v, w1_v, w2_v,
                w_br_a, w_br_b, w_br_c, w_out):
    y_att, y_rest = in_projection(x, norm_mix, _prep_w_in(w_in), IN_PROJ_TM, IN_PROJ_TN)
    y_a = banded_attention(y_att, _blocks_t(y_att, COL_VA)[0], COL_QA, COL_KA,
                           SWA_WINDOW, sinks, BF16)
    y_b = stick_breaking(y_att, _blocks(y_att, COL_KB, SB_HEADS, SB_TILE),
                         _blocks_t(y_att, COL_VB, SB_HEADS, SB_TILE))
    y_c = nsa_attention(y_att, y_rest, pe_k, w1_k, w2_k, pe_v, w1_v, w2_v)
    return merge_out_projection(y_a, y_b, y_c, _pair_rows(w_br_a).astype(BF16),
                                w_br_b.astype(BF16), _pair_rows(w_br_c).astype(BF16),
                                y_rest, w_out.astype(BF16), x, ROW_TM)


def ffn_layer(x, norm_ffn, w_gate, w_up, w_down, norm_after=None):
    u = ffn_up(x, norm_ffn, w_gate.astype(BF16), w_up.astype(BF16), 512, 512)
    return matmul_residual(u, w_down.astype(BF16), x, ROW_TM, norm_after)


def kernel(x, norm_mix, w_in, swa_sinks, cmp_pe_k, cmp_w1_k, cmp_w2_k, cmp_pe_v, cmp_w1_v,
           cmp_w2_v, w_branch_swa, w_branch_sb, w_branch_nsa, w_out, norm_ffn, w_gate, w_up,
           w_down, norm_final):
    b, s, d = x.shape
    outs = []
    for bi in range(b):
        xb = x[bi]
        depth = norm_mix.shape[0]
        assert depth >= 1
        for layer in range(depth):
            xb = mixer_layer(xb, norm_mix[layer], w_in[layer], swa_sinks[layer],
                             cmp_pe_k[layer], cmp_w1_k[layer], cmp_w2_k[layer],
                             cmp_pe_v[layer], cmp_w1_v[layer], cmp_w2_v[layer],
                             w_branch_swa[layer], w_branch_sb[layer], w_branch_nsa[layer],
                             w_out[layer])
            xb = ffn_layer(xb, norm_ffn[layer], w_gate[layer], w_up[layer], w_down[layer],
                           norm_final if layer == depth - 1 else None)
        outs.append(xb)
    return jnp.stack(outs, axis=0)
```

```python
import functools

import jax
import jax.numpy as jnp
import numpy as np
from jax import lax
from jax.experimental import pallas as pl
from jax.experimental.pallas import tpu as pltpu

F32 = jnp.float32
BF16 = jnp.bfloat16

D_MODEL = 2048
Q_BLOCK = 128
LANES = 128
HALF = 64
N_PAIR = 4
SWA_WINDOW = 128
NSA_WINDOW = 512
CMP_BLOCK = 32
CMP_STRIDE = 16
CMP_HIDDEN = 256
CMP_CHUNK = 256
SEL_BLOCK = 64
SEL_TOPK = 8
SB_HEADS = 4
SB_TILE = 256
D_FF = 5632
NEG = -1e30
SEL_FORCE = 1e9
SEL_TAKEN = -3e38
RMS_EPS = 1e-6
LOG2E = 1.4426950408889634
SB_DEAD_BITS = 150.0
VMEM_LIMIT = 56 * 1024 * 1024

N_ATT = 3584
N_GC = 24
N_GATES = 3 * D_MODEL
IN_PROJ_TM = 1024
IN_PROJ_TN = 512
FFN_TN = 512
ROW_TM = 256
COL_QA, COL_QC, COL_QB, COL_KB, COL_VB = 0, 4, 8, 12, 16
COL_KA, COL_VA, COL_KC, COL_VC, COL_KS, COL_VS, COL_KW, COL_VW = 20, 21, 22, 23, 24, 25, 26, 27


def _nt(a, b):
    return lax.dot_general(a, b, (((1,), (1,)), ((), ())), preferred_element_type=F32)


def _dot(a, b):
    return jnp.dot(a, b, preferred_element_type=F32)


def _params(*sem):
    return pltpu.CompilerParams(dimension_semantics=sem, vmem_limit_bytes=VMEM_LIMIT)


def _head_slope(pair, group):
    return 2.0 ** -(group * N_PAIR + pair + 1)


def _in_proj_kernel(x_ref, g_ref, w_ref, oa_ref, or_ref, h_ref, *, att_tiles):
    j = pl.program_id(1)

    @pl.when(j == 0)
    def _():
        x = x_ref[...]
        ms = jnp.mean(x * x, axis=-1, keepdims=True)
        h_ref[...] = (x * lax.rsqrt(ms + RMS_EPS) * g_ref[...]).astype(BF16)

    y = _dot(h_ref[...], w_ref[...])

    @pl.when(j < att_tiles)
    def _():
        oa_ref[...] = y.astype(oa_ref.dtype)

    @pl.when(j >= att_tiles)
    def _():
        or_ref[...] = y


def in_projection(x, g, w_tiles, tm):
    s, d = x.shape
    n_tiles, _, tn = w_tiles.shape
    att_tiles = N_ATT // tn
    rest_tiles = n_tiles - att_tiles
    kern = functools.partial(_in_proj_kernel, att_tiles=att_tiles)
    return pl.pallas_call(
        kern,
        grid=(s // tm, n_tiles),
        in_specs=[pl.BlockSpec((tm, d), lambda i, j: (i, 0)),
                  pl.BlockSpec((1, d), lambda i, j: (0, 0)),
                  pl.BlockSpec((None, d, tn), lambda i, j: (j, 0, 0))],
        out_specs=[pl.BlockSpec((tm, tn), lambda i, j: (i, jnp.minimum(j, att_tiles - 1))),
                   pl.BlockSpec((tm, tn), lambda i, j: (i, jnp.maximum(j - att_tiles, 0)))],
        out_shape=[jax.ShapeDtypeStruct((s, N_ATT), BF16),
                   jax.ShapeDtypeStruct((s, rest_tiles * tn), F32)],
        scratch_shapes=[pltpu.VMEM((tm, d), BF16)],
        compiler_params=_params("parallel", "arbitrary"),
        name="in_projection",
    )(x, g.reshape(1, d), w_tiles)


def _mm_res_kernel(a_ref, w_ref, r_ref, *rest, final_norm):
    y = r_ref[...] + _dot(a_ref[...], w_ref[...])
    if final_norm:
        g_ref, o_ref = rest
        ms = jnp.mean(y * y, axis=-1, keepdims=True)
        y = y * lax.rsqrt(ms + RMS_EPS) * g_ref[...]
    else:
        o_ref, = rest
    o_ref[...] = y


def matmul_residual(a, w, res, tm, norm_g=None):
    s, k = a.shape
    n = w.shape[1]
    in_specs = [pl.BlockSpec((tm, k), lambda i: (i, 0)),
                pl.BlockSpec((k, n), lambda i: (0, 0), pipeline_mode=pl.Buffered(1)),
                pl.BlockSpec((tm, n), lambda i: (i, 0))]
    args = [a, w, res]
    if norm_g is not None:
        in_specs.append(pl.BlockSpec((1, n), lambda i: (0, 0)))
        args.append(norm_g.reshape(1, n))
    return pl.pallas_call(
        functools.partial(_mm_res_kernel, final_norm=norm_g is not None),
        grid=(s // tm,),
        in_specs=in_specs,
        out_specs=pl.BlockSpec((tm, n), lambda i: (i, 0)),
        out_shape=jax.ShapeDtypeStruct((s, n), F32),
        compiler_params=_params("parallel"),
        name="matmul_residual",
    )(*args)


def _merge_out_kernel(ya_ref, yb_ref, yc_ref, wa_ref, wb_ref, wc_ref,
                      ga_ref, gb_ref, gc_ref, wo_ref, x_ref, o_ref):
    m = jax.nn.sigmoid(ga_ref[...]) * _dot(ya_ref[...], wa_ref[...])
    m = m + jax.nn.sigmoid(gb_ref[...]) * _dot(yb_ref[...], wb_ref[...])
    m = m + jax.nn.sigmoid(gc_ref[...]) * _dot(yc_ref[...], wc_ref[...])
    o_ref[...] = x_ref[...] + _dot(m.astype(BF16), wo_ref[...])


def merge_out_projection(ya, yb, yc, wa, wb, wc, y_rest, w_out, x, tm):
    s, k = ya.shape
    n = wa.shape[1]
    resident = pl.Buffered(1)
    y_spec = pl.BlockSpec((tm, k), lambda i: (i, 0))
    w_spec = pl.BlockSpec((k, n), lambda i: (0, 0), pipeline_mode=resident)

    def gate_spec(br):
        return pl.BlockSpec((tm, n), lambda i: (i, br))

    return pl.pallas_call(
        _merge_out_kernel,
        grid=(s // tm,),
        in_specs=[y_spec, y_spec, y_spec, w_spec, w_spec, w_spec,
                  gate_spec(0), gate_spec(1), gate_spec(2),
                  pl.BlockSpec((n, n), lambda i: (0, 0), pipeline_mode=resident),
                  pl.BlockSpec((tm, n), lambda i: (i, 0))],
        out_specs=pl.BlockSpec((tm, n), lambda i: (i, 0)),
        out_shape=jax.ShapeDtypeStruct((s, n), F32),
        compiler_params=_params("parallel"),
        name="merge_out_projection",
    )(ya, yb, yc, wa, wb, wc, y_rest, y_rest, y_rest, w_out, x)


def _ffn_up_kernel(x_ref, g_ref, wg_ref, wu_ref, o_ref, h_ref):
    @pl.when(pl.program_id(1) == 0)
    def _():
        x = x_ref[...]
        ms = jnp.mean(x * x, axis=-1, keepdims=True)
        h_ref[...] = (x * lax.rsqrt(ms + RMS_EPS) * g_ref[...]).astype(BF16)

    h = h_ref[...]
    o_ref[...] = (jax.nn.silu(_dot(h, wg_ref[...])) * _dot(h, wu_ref[...])).astype(o_ref.dtype)


def _column_tiles(w, tn):
    k, n = w.shape
    return w.reshape(k, n // tn, tn).transpose(1, 0, 2)


def ffn_up(x, g, wg, wu, tm):
    s, d = x.shape
    nt, _, tn = wg.shape
    n = nt * tn
    w_spec = pl.BlockSpec((None, d, tn), lambda i, j: (j, 0, 0))
    return pl.pallas_call(
        _ffn_up_kernel,
        grid=(s // tm, nt),
        in_specs=[pl.BlockSpec((tm, d), lambda i, j: (i, 0)),
                  pl.BlockSpec((1, d), lambda i, j: (0, 0)),
                  w_spec, w_spec],
        out_specs=pl.BlockSpec((tm, tn), lambda i, j: (i, j)),
        out_shape=jax.ShapeDtypeStruct((s, n), BF16),
        scratch_shapes=[pltpu.VMEM((tm, d), BF16)],
        compiler_params=_params("parallel", "arbitrary"),
        name="ffn_up",
    )(x, g.reshape(1, d), wg, wu)


def _half_masks():
    lane = lax.broadcasted_iota(jnp.int32, (Q_BLOCK, LANES), 1)
    return lane < HALF


def _banded_kernel(*refs, n_prev, window, has_sink, scale):
    nk = n_prev + 1
    if has_sink:
        sink_ref, refs = refs[0], refs[1:]
    q_ref = refs[0]
    k_refs = refs[1:1 + nk]
    v_refs = refs[1 + nk:1 + 2 * nk]
    o_ref = refs[1 + 2 * nk]
    i = pl.program_id(0)
    span = nk * Q_BLOCK
    k_all = jnp.concatenate([k_refs[d][...] for d in range(n_prev, -1, -1)], axis=0)
    vt_all = jnp.concatenate([v_refs[d][0] for d in range(n_prev, -1, -1)], axis=1)
    key_rel = lax.broadcasted_iota(jnp.int32, (span, Q_BLOCK), 0)
    q_rel = lax.broadcasted_iota(jnp.int32, (span, Q_BLOCK), 1)
    dist = q_rel + n_prev * Q_BLOCK - key_rel
    key_pos = (i - n_prev) * Q_BLOCK + key_rel
    mask = (dist >= 0) & (dist < window) & (key_pos >= 0)
    distf = dist.astype(F32)
    lo = _half_masks()
    top = lax.broadcasted_iota(jnp.int32, (Q_BLOCK, Q_BLOCK), 0) < HALF
    for pair in range(N_PAIR):
        q128 = q_ref[:, pair * LANES:(pair + 1) * LANES]
        halves = []
        for group in range(2):
            qm = jnp.where(lo if group == 0 else jnp.logical_not(lo), q128, jnp.zeros_like(q128))
            s = _nt(k_all, qm) * scale - _head_slope(pair, group) * distf
            s = jnp.where(mask, s, NEG)
            m = jnp.max(s, axis=0, keepdims=True)
            if has_sink:
                sink = sink_ref[group * N_PAIR + pair]
                m = jnp.maximum(m, sink)
                p = jnp.exp(s - m)
                denom = jnp.sum(p, axis=0, keepdims=True) + jnp.exp(sink - m)
            else:
                p = jnp.where(mask, jnp.exp(s - m), 0.0)
                denom = jnp.maximum(jnp.sum(p, axis=0, keepdims=True), 1e-30)
            pv = _dot(vt_all, p.astype(BF16))
            halves.append(pv * (1.0 / denom))
        ot = jnp.where(top, halves[0], halves[1])
        o_ref[:, pair * LANES:(pair + 1) * LANES] = ot.T.astype(o_ref.dtype)


def banded_attention(y_att, vt3, col_q, col_k, window, sinks, out_dtype):
    s = y_att.shape[0]
    nb = s // Q_BLOCK
    n_prev = -(-window // Q_BLOCK)
    has_sink = sinks is not None
    in_specs = []
    args = []
    if has_sink:
        in_specs.append(pl.BlockSpec(memory_space=pltpu.SMEM))
        args.append(sinks)
    in_specs.append(pl.BlockSpec((Q_BLOCK, N_PAIR * LANES), lambda i: (i, col_q // N_PAIR)))
    args.append(y_att)
    for d in range(n_prev + 1):
        in_specs.append(pl.BlockSpec((Q_BLOCK, LANES),
                                     lambda i, d=d: (jnp.maximum(i - d, 0), col_k)))
        args.append(y_att)
    for d in range(n_prev + 1):
        in_specs.append(pl.BlockSpec((1, LANES, Q_BLOCK),
                                     lambda i, d=d: (jnp.maximum(i - d, 0), 0, 0)))
        args.append(vt3)
    kern = functools.partial(_banded_kernel, n_prev=n_prev, window=window,
                             has_sink=has_sink, scale=HALF ** -0.5)
    return pl.pallas_call(
        kern,
        grid=(nb,),
        in_specs=in_specs,
        out_specs=pl.BlockSpec((Q_BLOCK, N_PAIR * LANES), lambda i: (i, 0)),
        out_shape=jax.ShapeDtypeStruct((s, N_PAIR * LANES), out_dtype),
        compiler_params=_params("parallel"),
        name="banded_attention_w%d" % window,
    )(*args)


def _sb_kernel(q_ref, k_ref, vt_ref, o_ref, ls_ref, acc_ref, *, scale, heads):
    i = pl.program_id(1)
    row = lax.broadcasted_iota(jnp.int32, (SB_TILE, SB_TILE), 0)
    col = lax.broadcasted_iota(jnp.int32, (SB_TILE, SB_TILE), 1)
    later = jnp.where(col > row, 1.0, 0.0).astype(BF16)
    before = row < col

    def visit(kj, diagonal):
        gone = [ls_ref[h] for h in range(heads)]
        pv, gone_new = [], []
        for h in range(heads):
            q = q_ref[:, h * LANES:(h + 1) * LANES]
            k = k_ref[pl.ds(pl.multiple_of(kj * SB_TILE, SB_TILE), SB_TILE),
                      h * LANES:(h + 1) * LANES]
            zz = _nt(k, q) * (scale * LOG2E)
            sp_raw = jnp.maximum(zz, 0.0) + jnp.log(1.0 + jnp.exp2(-jnp.abs(zz))) * LOG2E
            sp = jnp.where(before, sp_raw, 0.0) if diagonal else sp_raw
            a = jnp.exp2((zz - sp_raw) - _dot(later, sp.astype(BF16)) - gone[h])
            if diagonal:
                a = jnp.where(before, a, 0.0)
            pv.append(_dot(vt_ref[h, kj], a.astype(BF16)))
            gone_new.append(gone[h] + jnp.sum(sp, axis=0, keepdims=True))
        alive = gone_new[0]
        for h in range(heads):
            acc_ref[h] += pv[h]
            ls_ref[h] = gone_new[h]
            alive = jnp.minimum(alive, gone_new[h])
        return jnp.min(alive)

    ls_ref[...] = jnp.zeros(ls_ref.shape, F32)
    acc_ref[...] = jnp.zeros(acc_ref.shape, F32)
    alive0 = visit(i, True)

    def cond(carry):
        return (carry[0] < i) & (carry[1] < SB_DEAD_BITS)

    def body(carry):
        return carry[0] + 1, visit(i - 1 - carry[0], False)

    lax.while_loop(cond, body, (jnp.int32(0), alive0))
    for h in range(heads):
        o_ref[:, h * LANES:(h + 1) * LANES] = acc_ref[h].T.astype(o_ref.dtype)


def stick_breaking(y_att, vt4, heads=SB_HEADS):
    s = y_att.shape[0]
    nt = s // SB_TILE
    kern = functools.partial(_sb_kernel, scale=LANES ** -0.5, heads=heads)
    resident = pl.Buffered(1)
    return pl.pallas_call(
        kern,
        grid=(SB_HEADS // heads, nt),
        in_specs=[pl.BlockSpec((SB_TILE, heads * LANES), lambda g, i: (i, COL_QB // heads + g)),
                  pl.BlockSpec((s, heads * LANES), lambda g, i: (0, COL_KB // heads + g),
                               pipeline_mode=resident),
                  pl.BlockSpec((heads, nt, LANES, SB_TILE), lambda g, i: (g, 0, 0, 0),
                               pipeline_mode=resident)],
        out_specs=pl.BlockSpec((SB_TILE, heads * LANES), lambda g, i: (i, g)),
        out_shape=jax.ShapeDtypeStruct((s, SB_HEADS * LANES), BF16),
        scratch_shapes=[pltpu.VMEM((heads, 1, SB_TILE), F32),
                        pltpu.VMEM((heads, LANES, SB_TILE), F32)],
        compiler_params=_params("parallel", "arbitrary"),
        name="stick_breaking",
    )(y_att, y_att, vt4)


def _compress_kernel(t_ref, pe_ref, w1_ref, w2_ref, o_ref):
    t = t_ref[...]
    half = w1_ref.shape[0] // 2
    w1 = w1_ref[...]
    a = _dot(t, w1[:half])
    b = _dot(t, w1[half:])
    bias = _dot(pe_ref[...], w1)[0:1]
    n = t.shape[0]
    pre = a + pltpu.roll(b, n - 1, 0) + bias
    hid = jax.nn.gelu(pre)
    o_ref[...] = _dot(hid.astype(BF16), w2_ref[...]).astype(o_ref.dtype)


def compress_blocks(t_flat, pe, w1, w2):
    g, n, k = t_flat.shape
    pe_rows = jnp.zeros((8, 2 * k), BF16).at[0].set(pe.reshape(-1).astype(BF16))
    return pl.pallas_call(
        _compress_kernel,
        grid=(g,),
        in_specs=[pl.BlockSpec((None, n, k), lambda gi: (gi, 0, 0)),
                  pl.BlockSpec((8, 2 * k), lambda gi: (0, 0)),
                  pl.BlockSpec((2 * k, CMP_HIDDEN), lambda gi: (0, 0)),
                  pl.BlockSpec((CMP_HIDDEN, HALF), lambda gi: (0, 0))],
        out_specs=pl.BlockSpec((None, n, HALF), lambda gi: (gi, 0, 0)),
        out_shape=jax.ShapeDtypeStruct((g, n, HALF), BF16),
        compiler_params=_params("parallel"),
        name="compress_blocks",
    )(t_flat, pe_rows, w1.astype(BF16), w2.astype(BF16))


def _nsa_cmp_kernel(q_ref, kc_ref, vct_ref, ov_ref, oc_ref, sel_ref, cnt_ref, *, scale):
    i = pl.program_id(0)
    n_all = kc_ref.shape[0]
    chunk = min(CMP_CHUNK, n_all)
    needed = lax.div(8 * i + 7 + (chunk - 1), chunk)
    body = functools.partial(_nsa_cmp_body, q_ref, kc_ref, vct_ref, ov_ref, oc_ref, sel_ref,
                             cnt_ref, scale=scale)
    for c in range(1, n_all // chunk + 1):
        pl.when(needed == c)(functools.partial(body, n_cmp=c * chunk))


def _nsa_cmp_body(q_ref, kc_ref, vct_ref, ov_ref, oc_ref, sel_ref, cnt_ref, *, scale, n_cmp):
    i = pl.program_id(0)
    n_sel = ov_ref.shape[0]
    kc = kc_ref[:n_cmp, :]
    vct = vct_ref[:, :n_cmp]
    t = i * Q_BLOCK + lax.broadcasted_iota(jnp.int32, (n_cmp, Q_BLOCK), 1)
    cmp_end = lax.broadcasted_iota(jnp.int32, (n_cmp, Q_BLOCK), 0) * CMP_STRIDE + (CMP_BLOCK - 1)
    dist = t - cmp_end
    mask = dist >= 0
    distf = dist.astype(F32)
    lo = _half_masks()
    top = lax.broadcasted_iota(jnp.int32, (Q_BLOCK, Q_BLOCK), 0) < HALF
    psum = [jnp.zeros((n_cmp, Q_BLOCK), F32), jnp.zeros((n_cmp, Q_BLOCK), F32)]
    for pair in range(N_PAIR):
        q128 = q_ref[:, pair * LANES:(pair + 1) * LANES]
        halves = []
        for group in range(2):
            qm = jnp.where(lo if group == 0 else jnp.logical_not(lo), q128, jnp.zeros_like(q128))
            s = _nt(kc, qm) * scale - _head_slope(pair, group) * distf
            s = jnp.where(mask, s, NEG)
            m = jnp.max(s, axis=0, keepdims=True)
            p = jnp.where(mask, jnp.exp(s - m), 0.0)
            p = p * (1.0 / jnp.maximum(jnp.sum(p, axis=0, keepdims=True), 1e-30))
            halves.append(_dot(vct, p.astype(BF16)))
            psum[group] = psum[group] + p
        ot = jnp.where(top, halves[0], halves[1])
        oc_ref[:, pair * LANES:(pair + 1) * LANES] = ot.T

    ov = ov_ref[:, :n_cmp]
    tq = i * Q_BLOCK + lax.broadcasted_iota(jnp.int32, (n_sel, Q_BLOCK), 1)
    jblk = lax.broadcasted_iota(jnp.int32, (n_sel, Q_BLOCK), 0)
    jblk_f = jblk.astype(F32)
    jt = tq // SEL_BLOCK
    valid = jblk * SEL_BLOCK <= tq
    forced = (jblk == 0) | (jblk == jt) | (jblk == jt - 1)
    total = jnp.zeros((n_sel, Q_BLOCK), F32)
    for group in range(2):
        hi = psum[group].astype(BF16)
        lo_part = (psum[group] - hi.astype(F32)).astype(BF16)
        imp = _dot(ov, hi) + _dot(ov, lo_part)
        score = jnp.where(valid, jnp.where(forced, SEL_FORCE, imp), -SEL_FORCE)
        sel = jnp.zeros((n_sel, Q_BLOCK), F32)
        for _ in range(min(SEL_TOPK, n_sel)):
            best = jnp.max(score, axis=0, keepdims=True)
            first = jnp.min(jnp.where(score == best, jblk_f, float(n_sel)), axis=0, keepdims=True)
            hit = jblk_f == first
            sel = jnp.where(hit, 1.0, sel)
            score = jnp.where(hit, SEL_TAKEN, score)
        sel_ref[0, group] = sel
        total = total + sel
    cnt_ref[0] = _nt(jnp.ones((8, Q_BLOCK), BF16), total.astype(BF16))


def nsa_compressed(y_att, kc128, vct128, overlap_t):
    s = y_att.shape[0]
    nb = s // Q_BLOCK
    n_cmp = kc128.shape[0]
    n_sel = overlap_t.shape[0]
    kern = functools.partial(_nsa_cmp_kernel, scale=HALF ** -0.5)
    return pl.pallas_call(
        kern,
        grid=(nb,),
        in_specs=[pl.BlockSpec((Q_BLOCK, N_PAIR * LANES), lambda i: (i, COL_QC // N_PAIR)),
                  pl.BlockSpec((n_cmp, LANES), lambda i: (0, 0)),
                  pl.BlockSpec((LANES, n_cmp), lambda i: (0, 0)),
                  pl.BlockSpec((n_sel, n_cmp), lambda i: (0, 0))],
        out_specs=[pl.BlockSpec((Q_BLOCK, N_PAIR * LANES), lambda i: (i, 0)),
                   pl.BlockSpec((1, 2, n_sel, Q_BLOCK), lambda i: (i, 0, 0, 0)),
                   pl.BlockSpec((1, 8, n_sel), lambda i: (i, 0, 0))],
        out_shape=[jax.ShapeDtypeStruct((s, N_PAIR * LANES), F32),
                   jax.ShapeDtypeStruct((nb, 2, n_sel, Q_BLOCK), F32),
                   jax.ShapeDtypeStruct((nb, 8, n_sel), F32)],
        compiler_params=_params("parallel"),
        name="nsa_compressed",
    )(y_att, kc128, vct128, overlap_t)


def _nsa_sel_kernel(flags_ref, q_ref, ks_ref, vst_ref, sel_ref, slope_ref, gc_ref, oc_ref, ow_ref,
                    o_ref, qm_ref, m_ref, l_ref, acc_ref, todo_ref, *, scale, n_sel):
    i = pl.program_id(0)
    n_heads = 2 * N_PAIR
    lo = _half_masks()
    for pair in range(N_PAIR):
        q128 = q_ref[:, pair * LANES:(pair + 1) * LANES]
        for group in range(2):
            h = pair * 2 + group
            qm_ref[h * Q_BLOCK:(h + 1) * Q_BLOCK, :] = jnp.where(
                lo if group == 0 else jnp.logical_not(lo), q128, jnp.zeros_like(q128))
    m_ref[...] = jnp.full(m_ref.shape, NEG, F32)
    l_ref[...] = jnp.zeros(l_ref.shape, F32)
    acc_ref[...] = jnp.zeros(acc_ref.shape, F32)
    top = lax.broadcasted_iota(jnp.int32, (Q_BLOCK, Q_BLOCK), 0) < HALF
    slopes = slope_ref[...]

    def collect(p, n):
        base = i * n_sel + 2 * p
        active = (flags_ref[base] + flags_ref[base + 1]) > 0

        @pl.when(active)
        def _():
            todo_ref[n] = p

        return n + active.astype(jnp.int32)

    n_active = lax.fori_loop(0, i + 1, collect, jnp.int32(0))

    def tile_terms(p, live):
        key_pos = p * Q_BLOCK + lax.broadcasted_iota(jnp.int32, (Q_BLOCK, Q_BLOCK), 0)
        t = i * Q_BLOCK + lax.broadcasted_iota(jnp.int32, (Q_BLOCK, Q_BLOCK), 1)
        dist = t - key_pos
        bias = []
        for group in range(2):
            r0 = sel_ref[0, group, pl.ds(2 * p, 1), :]
            r1 = sel_ref[0, group, pl.ds(2 * p + 1, 1), :]
            picked = jnp.where(top, r0, r1)
            masked = jnp.where((picked > 0.5) & (dist >= 0), 0.0, NEG)
            bias.append(masked if live is True else jnp.where(live, masked, NEG))
        return ks_ref[p], vst_ref[p], dist.astype(F32), bias

    def body(step, carry):
        first = 2 * step
        has_second = first + 1 < n_active
        p0 = todo_ref[first]
        p1 = todo_ref[jnp.where(has_second, first + 1, first)]
        k0, vt0, dist0, bias0 = tile_terms(p0, True)
        k1, vt1, dist1, bias1 = tile_terms(p1, has_second)
        k = jnp.concatenate([k0, k1], axis=0)
        vt = jnp.concatenate([vt0, vt1], axis=1)
        bias_all = jnp.concatenate(
            [jnp.concatenate([bias0[h % 2], bias1[h % 2]], axis=0) for h in range(n_heads)], axis=1)
        dist_all = jnp.concatenate([jnp.concatenate([dist0, dist1], axis=0)] * n_heads, axis=1)
        s = _nt(k, qm_ref[...]) * scale - slopes * dist_all + bias_all
        m_old = m_ref[...]
        m_new = jnp.maximum(m_old, jnp.max(s, axis=0, keepdims=True))
        alpha = jnp.exp(m_old - m_new)
        pr = jnp.where(s > 0.5 * NEG, jnp.exp(s - m_new), 0.0)
        l_ref[...] = alpha * l_ref[...] + jnp.sum(pr, axis=0, keepdims=True)
        pv = _dot(vt, pr.astype(BF16))
        own = jnp.concatenate(
            [pv[(h % 2) * HALF:(h % 2 + 1) * HALF, h * Q_BLOCK:(h + 1) * Q_BLOCK]
             for h in range(n_heads)], axis=1)
        acc_ref[...] = alpha * acc_ref[...] + own
        m_ref[...] = m_new
        return carry

    lax.fori_loop(0, lax.div(n_active + 1, 2), body, 0)

    gates = jax.nn.sigmoid(gc_ref[...])
    o_all = acc_ref[...] * (1.0 / jnp.maximum(l_ref[...], 1e-30))
    for pair in range(N_PAIR):
        h = pair * 2
        o_s = jnp.concatenate([o_all[:, h * Q_BLOCK:(h + 1) * Q_BLOCK],
                               o_all[:, (h + 1) * Q_BLOCK:(h + 2) * Q_BLOCK]], axis=0).T
        cols = slice(pair * LANES, (pair + 1) * LANES)

        def gate(branch):
            c = branch * 8 + pair * 2
            return jnp.where(lo, gates[:, c:c + 1], gates[:, c + 1:c + 2])

        y = gate(0) * oc_ref[:, cols] + gate(1) * o_s + gate(2) * ow_ref[:, cols]
        o_ref[:, cols] = y.astype(o_ref.dtype)


def _slope_row():
    row = np.concatenate([np.full((Q_BLOCK,), _head_slope(h // 2, h % 2), np.float32)
                          for h in range(2 * N_PAIR)])
    return jnp.asarray(row.reshape(1, -1))


def nsa_selected(flags, y_att, ks3, vst3, sel_t, y_rest, o_c, o_w):
    s = y_att.shape[0]
    nb = s // Q_BLOCK
    n_sel = sel_t.shape[2]
    n_heads = 2 * N_PAIR
    kern = functools.partial(_nsa_sel_kernel, scale=HALF ** -0.5, n_sel=n_sel)
    wide = pl.BlockSpec((Q_BLOCK, N_PAIR * LANES), lambda i, f: (i, 0))
    grid_spec = pltpu.PrefetchScalarGridSpec(
        num_scalar_prefetch=1,
        grid=(nb,),
        in_specs=[pl.BlockSpec((Q_BLOCK, N_PAIR * LANES), lambda i, f: (i, COL_QC // N_PAIR)),
                  pl.BlockSpec((nb, Q_BLOCK, LANES), lambda i, f: (0, 0, 0)),
                  pl.BlockSpec((nb, LANES, Q_BLOCK), lambda i, f: (0, 0, 0)),
                  pl.BlockSpec((1, 2, n_sel, Q_BLOCK), lambda i, f: (i, 0, 0, 0)),
                  pl.BlockSpec((1, n_heads * Q_BLOCK), lambda i, f: (0, 0)),
                  pl.BlockSpec((Q_BLOCK, LANES), lambda i, f: (i, N_GATES // LANES)),
                  wide, wide],
        out_specs=wide,
        scratch_shapes=[pltpu.VMEM((n_heads * Q_BLOCK, LANES), BF16),
                        pltpu.VMEM((1, n_heads * Q_BLOCK), F32),
                        pltpu.VMEM((1, n_heads * Q_BLOCK), F32),
                        pltpu.VMEM((HALF, n_heads * Q_BLOCK), F32),
                        pltpu.SMEM((nb,), jnp.int32)],
    )
    return pl.pallas_call(
        kern,
        grid_spec=grid_spec,
        out_shape=jax.ShapeDtypeStruct((s, N_PAIR * LANES), BF16),
        compiler_params=_params("arbitrary"),
        name="nsa_selected",
    )(flags, y_att, ks3, vst3, sel_t, _slope_row(), y_rest, o_c, o_w)


def _pair_cols(w):
    lead = w.shape[:-1]
    return w.reshape(*lead, 2, N_PAIR, HALF).swapaxes(-3, -2).reshape(*lead, 2 * N_PAIR * HALF)


def _pair_rows(w):
    return w.reshape(2, N_PAIR, HALF, w.shape[-1]).swapaxes(0, 1).reshape(2 * N_PAIR * HALF, w.shape[-1])


def _prep_w_in(w_in):
    qa = _pair_cols(w_in[:, 0:512])
    qc = _pair_cols(w_in[:, 2304:2816])
    w_att = jnp.concatenate([qa, qc, w_in[:, 768:2304], w_in[:, 512:768], w_in[:, 2816:N_ATT]],
                            axis=1).astype(BF16)
    gc = w_in[:, N_ATT:N_ATT + N_GC].reshape(-1, 2, N_PAIR, 3)
    gc = gc.transpose(0, 3, 2, 1).reshape(-1, N_GC)
    gc = jnp.pad(gc, ((0, 0), (0, IN_PROJ_TN - N_GC)))
    w_all = jnp.concatenate([w_att, w_in[:, N_ATT + N_GC:].astype(BF16), gc.astype(BF16)], axis=1)
    return _column_tiles(w_all, IN_PROJ_TN)


def _blocks_t(y_att, col, width=1, tile=Q_BLOCK):
    s = y_att.shape[0]
    t = y_att[:, col * LANES:(col + width) * LANES].reshape(s // tile, tile, width, LANES)
    return t.transpose(2, 0, 3, 1)


def _blocks(y_att, col, width=1, tile=Q_BLOCK):
    s = y_att.shape[0]
    t = y_att[:, col * LANES:(col + width) * LANES].reshape(s // tile, tile, width, LANES)
    return t.transpose(2, 0, 1, 3)


def _chunk_rows(y_att, col):
    s = y_att.shape[0]
    t = y_att[:, col * LANES:(col + 1) * LANES].reshape(s // CMP_STRIDE, CMP_STRIDE, 2, HALF)
    return t.transpose(2, 0, 1, 3).reshape(2, s // CMP_STRIDE, CMP_STRIDE * HALF)


def _overlap_t(s):
    n_cmp = s // CMP_STRIDE
    n_sel = s // SEL_BLOCK
    cmp_lo = np.arange(n_cmp) * CMP_STRIDE
    cmp_end = cmp_lo + CMP_BLOCK - 1
    sel_lo = np.arange(n_sel) * SEL_BLOCK
    ov = ((cmp_lo[None, :] <= sel_lo[:, None] + SEL_BLOCK - 1)
          & (cmp_end[None, :] >= sel_lo[:, None]) & (np.arange(n_cmp)[None, :] < n_cmp - 1))
    return jnp.asarray(ov.astype(np.float32), dtype=BF16)


def nsa_attention(y_att, y_rest, pe_k, w1_k, w2_k, pe_v, w1_v, w2_v):
    s = y_att.shape[0]
    kc = compress_blocks(_chunk_rows(y_att, COL_KC), pe_k, w1_k, w2_k)
    vc = compress_blocks(_chunk_rows(y_att, COL_VC), pe_v, w1_v, w2_v)
    kc128 = kc.transpose(1, 0, 2).reshape(kc.shape[1], LANES)
    vct128 = vc.transpose(0, 2, 1).reshape(LANES, vc.shape[1])
    o_c, sel_t, cnt = nsa_compressed(y_att, kc128, vct128, _overlap_t(s))
    flags = (cnt[:, 0, :] > 0.5).astype(jnp.int32).reshape(-1)
    o_w = banded_attention(y_att, _blocks_t(y_att, COL_VW)[0], COL_QC, COL_KW,
                           NSA_WINDOW, None, F32)
    ks3 = _blocks(y_att, COL_KS)[0]
    vst3 = _blocks_t(y_att, COL_VS)[0]
    return nsa_selected(flags, y_att, ks3, vst3, sel_t, y_rest, o_c, o_w)


def mixer_layer(x, norm_mix, w_in, sinks, pe_k, w1_k, w2_k, pe_v, w1_v, w2_v,
                w_br_a, w_br_b, w_br_c, w_out):
    y_att, y_rest = in_projection(x, norm_mix, _prep_w_in(w_in), IN_PROJ_TM)
    y_a = banded_attention(y_att, _blocks_t(y_att, COL_VA)[0], COL_QA, COL_KA,
                           SWA_WINDOW, sinks, BF16)
    y_b = stick_breaking(y_att, _blocks_t(y_att, COL_VB, SB_HEADS, SB_TILE))
    y_c = nsa_attention(y_att, y_rest, pe_k, w1_k, w2_k, pe_v, w1_v, w2_v)
    return merge_out_projection(y_a, y_b, y_c, _pair_rows(w_br_a).astype(BF16),
                                w_br_b.astype(BF16), _pair_rows(w_br_c).astype(BF16),
                                y_rest, w_out.astype(BF16), x, ROW_TM)


def ffn_layer(x, norm_ffn, w_gate, w_up, w_down, norm_after=None):
    u = ffn_up(x, norm_ffn, _column_tiles(w_gate.astype(BF16), FFN_TN),
               _column_tiles(w_up.astype(BF16), FFN_TN), IN_PROJ_TM)
    return matmul_residual(u, w_down.astype(BF16), x, ROW_TM, norm_after)


def kernel(x, norm_mix, w_in, swa_sinks, cmp_pe_k, cmp_w1_k, cmp_w2_k, cmp_pe_v, cmp_w1_v,
           cmp_w2_v, w_branch_swa, w_branch_sb, w_branch_nsa, w_out, norm_ffn, w_gate, w_up,
           w_down, norm_final):
    b, s, d = x.shape
    outs = []
    for bi in range(b):
        xb = x[bi]
        depth = norm_mix.shape[0]
        assert depth >= 1
        for layer in range(depth):
            xb = mixer_layer(xb, norm_mix[layer], w_in[layer], swa_sinks[layer],
                             cmp_pe_k[layer], cmp_w1_k[layer], cmp_w2_k[layer],
                             cmp_pe_v[layer], cmp_w1_v[layer], cmp_w2_v[layer],
                             w_branch_swa[layer], w_branch_sb[layer], w_branch_nsa[layer],
                             w_out[layer])
            xb = ffn_layer(xb, norm_ffn[layer], w_gate[layer], w_up[layer], w_down[layer],
                           norm_final if layer == depth - 1 else None)
        outs.append(xb)
    return jnp.stack(outs, axis=0)
```

```python
import functools

import jax
import jax.numpy as jnp
import numpy as np
from jax import lax
from jax.experimental import pallas as pl
from jax.experimental.pallas import tpu as pltpu

F32 = jnp.float32
BF16 = jnp.bfloat16

D_MODEL = 2048
Q_BLOCK = 128
LANES = 128
HALF = 64
N_PAIR = 4
SWA_WINDOW = 128
NSA_WINDOW = 512
CMP_BLOCK = 32
CMP_STRIDE = 16
CMP_HIDDEN = 256
CMP_CHUNK = 256
SEL_BLOCK = 64
SEL_TOPK = 8
SB_HEADS = 4
SB_TILE = 256
D_FF = 5632
NEG = -1e30
SEL_FORCE = 1e9
SEL_TAKEN = -3e38
RMS_EPS = 1e-6
LOG2E = 1.4426950408889634
SB_DEAD_BITS = 150.0
VMEM_LIMIT = 56 * 1024 * 1024

N_ATT = 3584
N_GC = 24
N_GATES = 3 * D_MODEL
IN_PROJ_TM = 1024
IN_PROJ_TN = 512
FFN_TN = 512
ROW_TM = 256
COL_QA, COL_QC, COL_QB, COL_KB, COL_VB = 0, 4, 8, 12, 16
COL_KA, COL_KC, COL_VC, COL_KS, COL_KW = 20, 21, 22, 23, 24
COL_VA, COL_VS, COL_VW = 25, 26, 27
TRANSPOSE_ROWS = 2048


def _nt(a, b):
    return lax.dot_general(a, b, (((1,), (1,)), ((), ())), preferred_element_type=F32)


def _dot(a, b):
    return jnp.dot(a, b, preferred_element_type=F32)


def _params(*sem):
    return pltpu.CompilerParams(dimension_semantics=sem, vmem_limit_bytes=VMEM_LIMIT)


def _head_slope(pair, group):
    return 2.0 ** -(group * N_PAIR + pair + 1)


def _in_proj_kernel(x_ref, g_ref, w_ref, oa_ref, or_ref, h_ref, *, att_tiles):
    j = pl.program_id(1)

    @pl.when(j == 0)
    def _():
        x = x_ref[...]
        ms = jnp.mean(x * x, axis=-1, keepdims=True)
        h_ref[...] = (x * lax.rsqrt(ms + RMS_EPS) * g_ref[...]).astype(BF16)

    y = _dot(h_ref[...], w_ref[...])

    @pl.when(j < att_tiles)
    def _():
        oa_ref[...] = y.astype(oa_ref.dtype)

    @pl.when(j >= att_tiles)
    def _():
        or_ref[...] = y


def in_projection(x, g, w_all, tm, tn):
    s, d = x.shape
    n_tiles = w_all.shape[1] // tn
    att_tiles = N_ATT // tn
    rest_tiles = n_tiles - att_tiles
    kern = functools.partial(_in_proj_kernel, att_tiles=att_tiles)
    return pl.pallas_call(
        kern,
        grid=(s // tm, n_tiles),
        in_specs=[pl.BlockSpec((tm, d), lambda i, j: (i, 0)),
                  pl.BlockSpec((1, d), lambda i, j: (0, 0)),
                  pl.BlockSpec((d, tn), lambda i, j: (0, j))],
        out_specs=[pl.BlockSpec((tm, tn), lambda i, j: (i, jnp.minimum(j, att_tiles - 1))),
                   pl.BlockSpec((tm, tn), lambda i, j: (i, jnp.maximum(j - att_tiles, 0)))],
        out_shape=[jax.ShapeDtypeStruct((s, N_ATT), BF16),
                   jax.ShapeDtypeStruct((s, rest_tiles * tn), F32)],
        scratch_shapes=[pltpu.VMEM((tm, d), BF16)],
        compiler_params=_params("parallel", "arbitrary"),
        name="in_projection",
    )(x, g.reshape(1, d), w_all)


def _mm_res_kernel(a_ref, w_ref, r_ref, *rest, final_norm):
    y = r_ref[...] + _dot(a_ref[...], w_ref[...])
    if final_norm:
        g_ref, o_ref = rest
        ms = jnp.mean(y * y, axis=-1, keepdims=True)
        y = y * lax.rsqrt(ms + RMS_EPS) * g_ref[...]
    else:
        o_ref, = rest
    o_ref[...] = y


def matmul_residual(a, w, res, tm, norm_g=None):
    s, k = a.shape
    n = w.shape[1]
    in_specs = [pl.BlockSpec((tm, k), lambda i: (i, 0)),
                pl.BlockSpec((k, n), lambda i: (0, 0), pipeline_mode=pl.Buffered(1)),
                pl.BlockSpec((tm, n), lambda i: (i, 0))]
    args = [a, w, res]
    if norm_g is not None:
        in_specs.append(pl.BlockSpec((1, n), lambda i: (0, 0)))
        args.append(norm_g.reshape(1, n))
    return pl.pallas_call(
        functools.partial(_mm_res_kernel, final_norm=norm_g is not None),
        grid=(s // tm,),
        in_specs=in_specs,
        out_specs=pl.BlockSpec((tm, n), lambda i: (i, 0)),
        out_shape=jax.ShapeDtypeStruct((s, n), F32),
        compiler_params=_params("parallel"),
        name="matmul_residual",
    )(*args)


def _merge_out_kernel(ya_ref, yb_ref, yc_ref, wa_ref, wb_ref, wc_ref,
                      ga_ref, gb_ref, gc_ref, wo_ref, x_ref, o_ref):
    m = jax.nn.sigmoid(ga_ref[...]) * _dot(ya_ref[...], wa_ref[...])
    m = m + jax.nn.sigmoid(gb_ref[...]) * _dot(yb_ref[...], wb_ref[...])
    m = m + jax.nn.sigmoid(gc_ref[...]) * _dot(yc_ref[...], wc_ref[...])
    o_ref[...] = x_ref[...] + _dot(m.astype(BF16), wo_ref[...])


def merge_out_projection(ya, yb, yc, wa, wb, wc, y_rest, w_out, x, tm):
    s, k = ya.shape
    n = wa.shape[1]
    resident = pl.Buffered(1)
    y_spec = pl.BlockSpec((tm, k), lambda i: (i, 0))
    w_spec = pl.BlockSpec((k, n), lambda i: (0, 0), pipeline_mode=resident)

    def gate_spec(br):
        return pl.BlockSpec((tm, n), lambda i: (i, br))

    return pl.pallas_call(
        _merge_out_kernel,
        grid=(s // tm,),
        in_specs=[y_spec, y_spec, y_spec, w_spec, w_spec, w_spec,
                  gate_spec(0), gate_spec(1), gate_spec(2),
                  pl.BlockSpec((n, n), lambda i: (0, 0), pipeline_mode=resident),
                  pl.BlockSpec((tm, n), lambda i: (i, 0))],
        out_specs=pl.BlockSpec((tm, n), lambda i: (i, 0)),
        out_shape=jax.ShapeDtypeStruct((s, n), F32),
        compiler_params=_params("parallel"),
        name="merge_out_projection",
    )(ya, yb, yc, wa, wb, wc, y_rest, y_rest, y_rest, w_out, x)


def _ffn_up_kernel(x_ref, g_ref, wg_ref, wu_ref, o_ref, h_ref):
    @pl.when(pl.program_id(1) == 0)
    def _():
        x = x_ref[...]
        ms = jnp.mean(x * x, axis=-1, keepdims=True)
        h_ref[...] = (x * lax.rsqrt(ms + RMS_EPS) * g_ref[...]).astype(BF16)

    h = h_ref[...]
    o_ref[...] = (jax.nn.silu(_dot(h, wg_ref[...])) * _dot(h, wu_ref[...])).astype(o_ref.dtype)


def ffn_up(x, g, wg, wu, tm, tn):
    s, d = x.shape
    n = wg.shape[1]
    w_spec = pl.BlockSpec((d, tn), lambda i, j: (0, j))
    return pl.pallas_call(
        _ffn_up_kernel,
        grid=(s // tm, n // tn),
        in_specs=[pl.BlockSpec((tm, d), lambda i, j: (i, 0)),
                  pl.BlockSpec((1, d), lambda i, j: (0, 0)),
                  w_spec, w_spec],
        out_specs=pl.BlockSpec((tm, tn), lambda i, j: (i, j)),
        out_shape=jax.ShapeDtypeStruct((s, n), BF16),
        scratch_shapes=[pltpu.VMEM((tm, d), BF16)],
        compiler_params=_params("parallel", "arbitrary"),
        name="ffn_up",
    )(x, g.reshape(1, d), wg, wu)


def _half_masks():
    lane = lax.broadcasted_iota(jnp.int32, (Q_BLOCK, LANES), 1)
    return lane < HALF


def _banded_kernel(*refs, n_prev, window, has_sink):
    nk = n_prev + 1
    if has_sink:
        sink_ref, refs = refs[0], refs[1:]
    q_ref = refs[0]
    k_refs = refs[1:1 + nk]
    v_refs = refs[1 + nk:1 + 2 * nk]
    o_ref = refs[1 + 2 * nk]
    i = pl.program_id(0)
    span = nk * Q_BLOCK
    k_all = jnp.concatenate([k_refs[d][...] for d in range(n_prev, -1, -1)], axis=0)
    vt_all = jnp.concatenate([v_refs[d][...] for d in range(n_prev, -1, -1)], axis=1)
    key_rel = lax.broadcasted_iota(jnp.int32, (span, Q_BLOCK), 0)
    q_rel = lax.broadcasted_iota(jnp.int32, (span, Q_BLOCK), 1)
    dist = q_rel + n_prev * Q_BLOCK - key_rel
    key_pos = (i - n_prev) * Q_BLOCK + key_rel
    mask = (dist >= 0) & (dist < window) & (key_pos >= 0)
    distf = dist.astype(F32)
    lo = _half_masks()
    top = lax.broadcasted_iota(jnp.int32, (Q_BLOCK, Q_BLOCK), 0) < HALF
    for pair in range(N_PAIR):
        q128 = q_ref[:, pair * LANES:(pair + 1) * LANES]
        halves = []
        for group in range(2):
            qm = jnp.where(lo if group == 0 else jnp.logical_not(lo), q128, jnp.zeros_like(q128))
            s = _nt(k_all, qm) - _head_slope(pair, group) * distf
            s = jnp.where(mask, s, NEG)
            m = jnp.max(s, axis=0, keepdims=True)
            if has_sink:
                sink = sink_ref[group * N_PAIR + pair]
                m = jnp.maximum(m, sink)
                p = jnp.exp(s - m)
                denom = jnp.sum(p, axis=0, keepdims=True) + jnp.exp(sink - m)
            else:
                p = jnp.where(mask, jnp.exp(s - m), 0.0)
                denom = jnp.maximum(jnp.sum(p, axis=0, keepdims=True), 1e-30)
            pv = _dot(vt_all, p.astype(BF16))
            halves.append(pv * (1.0 / denom))
        ot = jnp.where(top, halves[0], halves[1])
        o_ref[:, pair * LANES:(pair + 1) * LANES] = ot.T.astype(o_ref.dtype)


def banded_attention(y_att, vt4, vt_index, col_q, col_k, window, sinks, out_dtype):
    s = y_att.shape[0]
    nb = s // Q_BLOCK
    n_prev = -(-window // Q_BLOCK)
    has_sink = sinks is not None
    in_specs = []
    args = []
    if has_sink:
        in_specs.append(pl.BlockSpec(memory_space=pltpu.SMEM))
        args.append(sinks)
    in_specs.append(pl.BlockSpec((Q_BLOCK, N_PAIR * LANES), lambda i: (i, col_q // N_PAIR)))
    args.append(y_att)
    for d in range(n_prev + 1):
        in_specs.append(pl.BlockSpec((Q_BLOCK, LANES),
                                     lambda i, d=d: (jnp.maximum(i - d, 0), col_k)))
        args.append(y_att)
    for d in range(n_prev + 1):
        in_specs.append(pl.BlockSpec((None, None, LANES, Q_BLOCK),
                                     lambda i, d=d: (vt_index, jnp.maximum(i - d, 0), 0, 0)))
        args.append(vt4)
    kern = functools.partial(_banded_kernel, n_prev=n_prev, window=window, has_sink=has_sink)
    return pl.pallas_call(
        kern,
        grid=(nb,),
        in_specs=in_specs,
        out_specs=pl.BlockSpec((Q_BLOCK, N_PAIR * LANES), lambda i: (i, 0)),
        out_shape=jax.ShapeDtypeStruct((s, N_PAIR * LANES), out_dtype),
        compiler_params=_params("parallel"),
        name="banded_attention_w%d" % window,
    )(*args)


def _sb_kernel(q_ref, k_ref, vt_ref, o_ref, ls_ref, acc_ref, *, scale, heads):
    i = pl.program_id(1)
    row = lax.broadcasted_iota(jnp.int32, (SB_TILE, SB_TILE), 0)
    col = lax.broadcasted_iota(jnp.int32, (SB_TILE, SB_TILE), 1)
    later = jnp.where(col > row, 1.0, 0.0).astype(BF16)
    before = row < col

    def visit(kj, diagonal):
        gone = [ls_ref[h] for h in range(heads)]
        pv, gone_new = [], []
        for h in range(heads):
            q = q_ref[:, h * LANES:(h + 1) * LANES]
            k = k_ref[pl.ds(pl.multiple_of(kj * SB_TILE, SB_TILE), SB_TILE),
                      h * LANES:(h + 1) * LANES]
            zz = _nt(k, q) * (scale * LOG2E)
            sp_raw = jnp.maximum(zz, 0.0) + jnp.log(1.0 + jnp.exp2(-jnp.abs(zz))) * LOG2E
            sp = jnp.where(before, sp_raw, 0.0) if diagonal else sp_raw
            a = jnp.exp2((zz - sp_raw) - _dot(later, sp.astype(BF16)) - gone[h])
            if diagonal:
                a = jnp.where(before, a, 0.0)
            pv.append(_dot(vt_ref[h, kj], a.astype(BF16)))
            gone_new.append(gone[h] + jnp.sum(sp, axis=0, keepdims=True))
        alive = gone_new[0]
        for h in range(heads):
            acc_ref[h] += pv[h]
            ls_ref[h] = gone_new[h]
            alive = jnp.minimum(alive, gone_new[h])
        return jnp.min(alive)

    ls_ref[...] = jnp.zeros(ls_ref.shape, F32)
    acc_ref[...] = jnp.zeros(acc_ref.shape, F32)
    alive0 = visit(i, True)

    def cond(carry):
        return (carry[0] < i) & (carry[1] < SB_DEAD_BITS)

    def body(carry):
        return carry[0] + 1, visit(i - 1 - carry[0], False)

    lax.while_loop(cond, body, (jnp.int32(0), alive0))
    for h in range(heads):
        o_ref[:, h * LANES:(h + 1) * LANES] = acc_ref[h].T.astype(o_ref.dtype)


def stick_breaking(y_att, vt4, heads=SB_HEADS):
    s = y_att.shape[0]
    nt = s // SB_TILE
    kern = functools.partial(_sb_kernel, scale=LANES ** -0.5, heads=heads)
    resident = pl.Buffered(1)
    return pl.pallas_call(
        kern,
        grid=(SB_HEADS // heads, nt),
        in_specs=[pl.BlockSpec((SB_TILE, heads * LANES), lambda g, i: (i, COL_QB // heads + g)),
                  pl.BlockSpec((s, heads * LANES), lambda g, i: (0, COL_KB // heads + g),
                               pipeline_mode=resident),
                  pl.BlockSpec((heads, nt, LANES, SB_TILE), lambda g, i: (g, 0, 0, 0),
                               pipeline_mode=resident)],
        out_specs=pl.BlockSpec((SB_TILE, heads * LANES), lambda g, i: (i, g)),
        out_shape=jax.ShapeDtypeStruct((s, SB_HEADS * LANES), BF16),
        scratch_shapes=[pltpu.VMEM((heads, 1, SB_TILE), F32),
                        pltpu.VMEM((heads, LANES, SB_TILE), F32)],
        compiler_params=_params("parallel", "arbitrary"),
        name="stick_breaking",
    )(y_att, y_att, vt4)


def _compress_kernel(t_ref, pe_ref, w1_ref, w2_ref, o_ref):
    t = t_ref[...]
    half = w1_ref.shape[0] // 2
    w1 = w1_ref[...]
    a = _dot(t, w1[:half])
    b = _dot(t, w1[half:])
    bias = _dot(pe_ref[...], w1)[0:1]
    n = t.shape[0]
    pre = a + pltpu.roll(b, n - 1, 0) + bias
    hid = jax.nn.gelu(pre)
    o_ref[...] = _dot(hid.astype(BF16), w2_ref[...]).astype(o_ref.dtype)


def compress_blocks(t_flat, pe, w1, w2):
    g, n, k = t_flat.shape
    pe_rows = jnp.zeros((8, 2 * k), BF16).at[0].set(pe.reshape(-1).astype(BF16))
    return pl.pallas_call(
        _compress_kernel,
        grid=(g,),
        in_specs=[pl.BlockSpec((None, n, k), lambda gi: (gi, 0, 0)),
                  pl.BlockSpec((8, 2 * k), lambda gi: (0, 0)),
                  pl.BlockSpec((2 * k, CMP_HIDDEN), lambda gi: (0, 0)),
                  pl.BlockSpec((CMP_HIDDEN, HALF), lambda gi: (0, 0))],
        out_specs=pl.BlockSpec((None, n, HALF), lambda gi: (gi, 0, 0)),
        out_shape=jax.ShapeDtypeStruct((g, n, HALF), BF16),
        compiler_params=_params("parallel"),
        name="compress_blocks",
    )(t_flat, pe_rows, w1.astype(BF16), w2.astype(BF16))


def _nsa_cmp_kernel(q_ref, kc_ref, vct_ref, ov_ref, oc_ref, sel_ref, cnt_ref):
    i = pl.program_id(0)
    n_all = kc_ref.shape[0]
    chunk = min(CMP_CHUNK, n_all)
    needed = lax.div(8 * i + 7 + (chunk - 1), chunk)
    body = functools.partial(_nsa_cmp_body, q_ref, kc_ref, vct_ref, ov_ref, oc_ref, sel_ref, cnt_ref)
    for c in range(1, n_all // chunk + 1):
        pl.when(needed == c)(functools.partial(body, n_cmp=c * chunk))


def _nsa_cmp_body(q_ref, kc_ref, vct_ref, ov_ref, oc_ref, sel_ref, cnt_ref, *, n_cmp):
    i = pl.program_id(0)
    n_sel = ov_ref.shape[0]
    kc = kc_ref[:n_cmp, :]
    vct = vct_ref[:, :n_cmp]
    t = i * Q_BLOCK + lax.broadcasted_iota(jnp.int32, (n_cmp, Q_BLOCK), 1)
    cmp_end = lax.broadcasted_iota(jnp.int32, (n_cmp, Q_BLOCK), 0) * CMP_STRIDE + (CMP_BLOCK - 1)
    dist = t - cmp_end
    mask = dist >= 0
    distf = dist.astype(F32)
    lo = _half_masks()
    top = lax.broadcasted_iota(jnp.int32, (Q_BLOCK, Q_BLOCK), 0) < HALF
    psum = [jnp.zeros((n_cmp, Q_BLOCK), F32), jnp.zeros((n_cmp, Q_BLOCK), F32)]
    for pair in range(N_PAIR):
        q128 = q_ref[:, pair * LANES:(pair + 1) * LANES]
        halves = []
        for group in range(2):
            qm = jnp.where(lo if group == 0 else jnp.logical_not(lo), q128, jnp.zeros_like(q128))
            s = _nt(kc, qm) - _head_slope(pair, group) * distf
            s = jnp.where(mask, s, NEG)
            m = jnp.max(s, axis=0, keepdims=True)
            p = jnp.where(mask, jnp.exp(s - m), 0.0)
            p = p * (1.0 / jnp.maximum(jnp.sum(p, axis=0, keepdims=True), 1e-30))
            halves.append(_dot(vct, p.astype(BF16)))
            psum[group] = psum[group] + p
        ot = jnp.where(top, halves[0], halves[1])
        oc_ref[:, pair * LANES:(pair + 1) * LANES] = ot.T

    ov = ov_ref[:, :n_cmp]
    tq = i * Q_BLOCK + lax.broadcasted_iota(jnp.int32, (n_sel, Q_BLOCK), 1)
    jblk = lax.broadcasted_iota(jnp.int32, (n_sel, Q_BLOCK), 0)
    jblk_f = jblk.astype(F32)
    jt = tq // SEL_BLOCK
    valid = jblk * SEL_BLOCK <= tq
    forced = (jblk == 0) | (jblk == jt) | (jblk == jt - 1)
    total = jnp.zeros((n_sel, Q_BLOCK), F32)
    for group in range(2):
        hi = psum[group].astype(BF16)
        lo_part = (psum[group] - hi.astype(F32)).astype(BF16)
        imp = _dot(ov, hi) + _dot(ov, lo_part)
        score = jnp.where(valid, jnp.where(forced, SEL_FORCE, imp), -SEL_FORCE)
        sel = jnp.zeros((n_sel, Q_BLOCK), F32)
        for _ in range(min(SEL_TOPK, n_sel)):
            best = jnp.max(score, axis=0, keepdims=True)
            first = jnp.min(jnp.where(score == best, jblk_f, float(n_sel)), axis=0, keepdims=True)
            hit = jblk_f == first
            sel = jnp.where(hit, 1.0, sel)
            score = jnp.where(hit, SEL_TAKEN, score)
        sel_ref[0, group] = sel
        total = total + sel
    cnt_ref[0] = _nt(jnp.ones((8, Q_BLOCK), BF16), total.astype(BF16))


def nsa_compressed(y_att, kc128, vct128, overlap_t):
    s = y_att.shape[0]
    nb = s // Q_BLOCK
    n_cmp = kc128.shape[0]
    n_sel = overlap_t.shape[0]
    return pl.pallas_call(
        _nsa_cmp_kernel,
        grid=(nb,),
        in_specs=[pl.BlockSpec((Q_BLOCK, N_PAIR * LANES), lambda i: (i, COL_QC // N_PAIR)),
                  pl.BlockSpec((n_cmp, LANES), lambda i: (0, 0)),
                  pl.BlockSpec((LANES, n_cmp), lambda i: (0, 0)),
                  pl.BlockSpec((n_sel, n_cmp), lambda i: (0, 0))],
        out_specs=[pl.BlockSpec((Q_BLOCK, N_PAIR * LANES), lambda i: (i, 0)),
                   pl.BlockSpec((1, 2, n_sel, Q_BLOCK), lambda i: (i, 0, 0, 0)),
                   pl.BlockSpec((1, 8, n_sel), lambda i: (i, 0, 0))],
        out_shape=[jax.ShapeDtypeStruct((s, N_PAIR * LANES), F32),
                   jax.ShapeDtypeStruct((nb, 2, n_sel, Q_BLOCK), F32),
                   jax.ShapeDtypeStruct((nb, 8, n_sel), F32)],
        compiler_params=_params("parallel"),
        name="nsa_compressed",
    )(y_att, kc128, vct128, overlap_t)


def _nsa_sel_kernel(flags_ref, q_ref, ks_ref, vst_ref, sel_ref, slope_ref, gc_ref, oc_ref, ow_ref,
                    o_ref, qm_ref, m_ref, l_ref, acc_ref, todo_ref, *, n_sel):
    i = pl.program_id(0)
    n_heads = 2 * N_PAIR
    lo = _half_masks()
    for pair in range(N_PAIR):
        q128 = q_ref[:, pair * LANES:(pair + 1) * LANES]
        for group in range(2):
            h = pair * 2 + group
            qm_ref[h * Q_BLOCK:(h + 1) * Q_BLOCK, :] = jnp.where(
                lo if group == 0 else jnp.logical_not(lo), q128, jnp.zeros_like(q128))
    m_ref[...] = jnp.full(m_ref.shape, NEG, F32)
    l_ref[...] = jnp.zeros(l_ref.shape, F32)
    acc_ref[...] = jnp.zeros(acc_ref.shape, F32)
    top = lax.broadcasted_iota(jnp.int32, (Q_BLOCK, Q_BLOCK), 0) < HALF
    slopes = slope_ref[...]

    def collect(p, n):
        base = i * n_sel + 2 * p
        active = (flags_ref[base] + flags_ref[base + 1]) > 0

        @pl.when(active)
        def _():
            todo_ref[n] = p

        return n + active.astype(jnp.int32)

    n_active = lax.fori_loop(0, i + 1, collect, jnp.int32(0))

    def tile_terms(p, live):
        key_pos = p * Q_BLOCK + lax.broadcasted_iota(jnp.int32, (Q_BLOCK, Q_BLOCK), 0)
        t = i * Q_BLOCK + lax.broadcasted_iota(jnp.int32, (Q_BLOCK, Q_BLOCK), 1)
        dist = t - key_pos
        bias = []
        for group in range(2):
            r0 = sel_ref[0, group, pl.ds(2 * p, 1), :]
            r1 = sel_ref[0, group, pl.ds(2 * p + 1, 1), :]
            picked = jnp.where(top, r0, r1)
            masked = jnp.where((picked > 0.5) & (dist >= 0), 0.0, NEG)
            bias.append(masked if live is True else jnp.where(live, masked, NEG))
        return ks_ref[p], vst_ref[p], dist.astype(F32), bias

    def body(step, carry):
        first = 2 * step
        has_second = first + 1 < n_active
        p0 = todo_ref[first]
        p1 = todo_ref[jnp.where(has_second, first + 1, first)]
        k0, vt0, dist0, bias0 = tile_terms(p0, True)
        k1, vt1, dist1, bias1 = tile_terms(p1, has_second)
        k = jnp.concatenate([k0, k1], axis=0)
        vt = jnp.concatenate([vt0, vt1], axis=1)
        bias_all = jnp.concatenate(
            [jnp.concatenate([bias0[h % 2], bias1[h % 2]], axis=0) for h in range(n_heads)], axis=1)
        dist_all = jnp.concatenate([jnp.concatenate([dist0, dist1], axis=0)] * n_heads, axis=1)
        s = _nt(k, qm_ref[...]) - slopes * dist_all + bias_all
        m_old = m_ref[...]
        m_new = jnp.maximum(m_old, jnp.max(s, axis=0, keepdims=True))
        alpha = jnp.exp(m_old - m_new)
        pr = jnp.where(s > 0.5 * NEG, jnp.exp(s - m_new), 0.0)
        l_ref[...] = alpha * l_ref[...] + jnp.sum(pr, axis=0, keepdims=True)
        pv = _dot(vt, pr.astype(BF16))
        own = jnp.concatenate(
            [pv[(h % 2) * HALF:(h % 2 + 1) * HALF, h * Q_BLOCK:(h + 1) * Q_BLOCK]
             for h in range(n_heads)], axis=1)
        acc_ref[...] = alpha * acc_ref[...] + own
        m_ref[...] = m_new
        return carry

    lax.fori_loop(0, lax.div(n_active + 1, 2), body, 0)

    gates = jax.nn.sigmoid(gc_ref[...])
    o_all = acc_ref[...] * (1.0 / jnp.maximum(l_ref[...], 1e-30))
    for pair in range(N_PAIR):
        h = pair * 2
        o_s = jnp.concatenate([o_all[:, h * Q_BLOCK:(h + 1) * Q_BLOCK],
                               o_all[:, (h + 1) * Q_BLOCK:(h + 2) * Q_BLOCK]], axis=0).T
        cols = slice(pair * LANES, (pair + 1) * LANES)

        def gate(branch):
            c = branch * 8 + pair * 2
            return jnp.where(lo, gates[:, c:c + 1], gates[:, c + 1:c + 2])

        y = gate(0) * oc_ref[:, cols] + gate(1) * o_s + gate(2) * ow_ref[:, cols]
        o_ref[:, cols] = y.astype(o_ref.dtype)


def _slope_row():
    row = np.concatenate([np.full((Q_BLOCK,), _head_slope(h // 2, h % 2), np.float32)
                          for h in range(2 * N_PAIR)])
    return jnp.asarray(row.reshape(1, -1))


def nsa_selected(flags, y_att, ks3, vt4, vt_index, sel_t, y_rest, o_c, o_w):
    s = y_att.shape[0]
    nb = s // Q_BLOCK
    n_sel = sel_t.shape[2]
    n_heads = 2 * N_PAIR
    kern = functools.partial(_nsa_sel_kernel, n_sel=n_sel)
    wide = pl.BlockSpec((Q_BLOCK, N_PAIR * LANES), lambda i, f: (i, 0))
    grid_spec = pltpu.PrefetchScalarGridSpec(
        num_scalar_prefetch=1,
        grid=(nb,),
        in_specs=[pl.BlockSpec((Q_BLOCK, N_PAIR * LANES), lambda i, f: (i, COL_QC // N_PAIR)),
                  pl.BlockSpec((nb, Q_BLOCK, LANES), lambda i, f: (0, 0, 0)),
                  pl.BlockSpec((None, nb, LANES, Q_BLOCK), lambda i, f: (vt_index, 0, 0, 0)),
                  pl.BlockSpec((1, 2, n_sel, Q_BLOCK), lambda i, f: (i, 0, 0, 0)),
                  pl.BlockSpec((1, n_heads * Q_BLOCK), lambda i, f: (0, 0)),
                  pl.BlockSpec((Q_BLOCK, LANES), lambda i, f: (i, N_GATES // LANES)),
                  wide, wide],
        out_specs=wide,
        scratch_shapes=[pltpu.VMEM((n_heads * Q_BLOCK, LANES), BF16),
                        pltpu.VMEM((1, n_heads * Q_BLOCK), F32),
                        pltpu.VMEM((1, n_heads * Q_BLOCK), F32),
                        pltpu.VMEM((HALF, n_heads * Q_BLOCK), F32),
                        pltpu.SMEM((nb,), jnp.int32)],
    )
    return pl.pallas_call(
        kern,
        grid_spec=grid_spec,
        out_shape=jax.ShapeDtypeStruct((s, N_PAIR * LANES), BF16),
        compiler_params=_params("arbitrary"),
        name="nsa_selected",
    )(flags, y_att, ks3, vt4, sel_t, _slope_row(), y_rest, o_c, o_w)


def _pair_cols(w):
    lead = w.shape[:-1]
    return w.reshape(*lead, 2, N_PAIR, HALF).swapaxes(-3, -2).reshape(*lead, 2 * N_PAIR * HALF)


def _pair_rows(w):
    return w.reshape(2, N_PAIR, HALF, w.shape[-1]).swapaxes(0, 1).reshape(2 * N_PAIR * HALF, w.shape[-1])


def _prep_w_in(w_in):
    qa = _pair_cols(w_in[:, 0:512]) * HALF ** -0.5
    qc = _pair_cols(w_in[:, 2304:2816]) * HALF ** -0.5
    cols = lambda a, b: w_in[:, a:b]
    w_att = jnp.concatenate(
        [qa, qc, cols(768, 2304),
         cols(512, 640), cols(2816, 3072), cols(3072, 3200), cols(3328, 3456),
         cols(640, 768), cols(3200, 3328), cols(3456, 3584)],
        axis=1).astype(BF16)
    gc = w_in[:, N_ATT:N_ATT + N_GC].reshape(-1, 2, N_PAIR, 3)
    gc = gc.transpose(0, 3, 2, 1).reshape(-1, N_GC)
    gc = jnp.pad(gc, ((0, 0), (0, IN_PROJ_TN - N_GC)))
    return jnp.concatenate([w_att, w_in[:, N_ATT + N_GC:].astype(BF16), gc.astype(BF16)], axis=1)


def _transpose_kernel(x_ref, o_ref):
    tile = o_ref.shape[-1]
    for c in range(o_ref.shape[0]):
        o_ref[c] = x_ref[c * tile:(c + 1) * tile, :].T


def transpose_tiles(y_att, col, width, tile):
    s = y_att.shape[0]
    rows = min(TRANSPOSE_ROWS, s)
    return pl.pallas_call(
        _transpose_kernel,
        grid=(width, s // rows),
        in_specs=[pl.BlockSpec((rows, LANES), lambda c, r: (r, col + c))],
        out_specs=pl.BlockSpec((None, rows // tile, LANES, tile), lambda c, r: (c, r, 0, 0)),
        out_shape=jax.ShapeDtypeStruct((width, s // tile, LANES, tile), y_att.dtype),
        compiler_params=_params("parallel", "parallel"),
        name="transpose_tiles",
    )(y_att)


def _blocks(y_att, col, width=1, tile=Q_BLOCK):
    s = y_att.shape[0]
    t = y_att[:, col * LANES:(col + width) * LANES].reshape(s // tile, tile, width, LANES)
    return t.transpose(2, 0, 1, 3)


def _chunk_rows(y_att, col):
    s = y_att.shape[0]
    t = y_att[:, col * LANES:(col + 1) * LANES].reshape(s // CMP_STRIDE, CMP_STRIDE, 2, HALF)
    return t.transpose(2, 0, 1, 3).reshape(2, s // CMP_STRIDE, CMP_STRIDE * HALF)


def _overlap_t(s):
    n_cmp = s // CMP_STRIDE
    n_sel = s // SEL_BLOCK
    cmp_lo = np.arange(n_cmp) * CMP_STRIDE
    cmp_end = cmp_lo + CMP_BLOCK - 1
    sel_lo = np.arange(n_sel) * SEL_BLOCK
    ov = ((cmp_lo[None, :] <= sel_lo[:, None] + SEL_BLOCK - 1)
          & (cmp_end[None, :] >= sel_lo[:, None]) & (np.arange(n_cmp)[None, :] < n_cmp - 1))
    return jnp.asarray(ov.astype(np.float32), dtype=BF16)


def nsa_attention(y_att, y_rest, vt4, pe_k, w1_k, w2_k, pe_v, w1_v, w2_v):
    s = y_att.shape[0]
    kc = compress_blocks(_chunk_rows(y_att, COL_KC), pe_k, w1_k, w2_k)
    vc = compress_blocks(_chunk_rows(y_att, COL_VC), pe_v, w1_v, w2_v)
    kc128 = kc.transpose(1, 0, 2).reshape(kc.shape[1], LANES)
    vct128 = vc.transpose(0, 2, 1).reshape(LANES, vc.shape[1])
    o_c, sel_t, cnt = nsa_compressed(y_att, kc128, vct128, _overlap_t(s))
    flags = (cnt[:, 0, :] > 0.5).astype(jnp.int32).reshape(-1)
    o_w = banded_attention(y_att, vt4, COL_VW - COL_VA, COL_QC, COL_KW, NSA_WINDOW, None, F32)
    ks3 = _blocks(y_att, COL_KS)[0]
    return nsa_selected(flags, y_att, ks3, vt4, COL_VS - COL_VA, sel_t, y_rest, o_c, o_w)


def mixer_layer(x, norm_mix, w_in, sinks, pe_k, w1_k, w2_k, pe_v, w1_v, w2_v,
                w_br_a, w_br_b, w_br_c, w_out):
    y_att, y_rest = in_projection(x, norm_mix, _prep_w_in(w_in), IN_PROJ_TM, IN_PROJ_TN)
    vt4 = transpose_tiles(y_att, COL_VA, 3, Q_BLOCK)
    y_a = banded_attention(y_att, vt4, 0, COL_QA, COL_KA, SWA_WINDOW, sinks, BF16)
    y_b = stick_breaking(y_att, transpose_tiles(y_att, COL_VB, SB_HEADS, SB_TILE))
    y_c = nsa_attention(y_att, y_rest, vt4, pe_k, w1_k, w2_k, pe_v, w1_v, w2_v)
    return merge_out_projection(y_a, y_b, y_c, _pair_rows(w_br_a).astype(BF16),
                                w_br_b.astype(BF16), _pair_rows(w_br_c).astype(BF16),
                                y_rest, w_out.astype(BF16), x, ROW_TM)


def ffn_layer(x, norm_ffn, w_gate, w_up, w_down, norm_after=None):
    u = ffn_up(x, norm_ffn, w_gate.astype(BF16), w_up.astype(BF16), IN_PROJ_TM, FFN_TN)
    return matmul_residual(u, w_down.astype(BF16), x, ROW_TM, norm_after)


def kernel(x, norm_mix, w_in, swa_sinks, cmp_pe_k, cmp_w1_k, cmp_w2_k, cmp_pe_v, cmp_w1_v,
           cmp_w2_v, w_branch_swa, w_branch_sb, w_branch_nsa, w_out, norm_ffn, w_gate, w_up,
           w_down, norm_final):
    b, s, d = x.shape
    outs = []
    for bi in range(b):
        xb = x[bi]
        depth = norm_mix.shape[0]
        assert depth >= 1
        for layer in range(depth):
            xb = mixer_layer(xb, norm_mix[layer], w_in[layer], swa_sinks[layer],
                             cmp_pe_k[layer], cmp_w1_k[layer], cmp_w2_k[layer],
                             cmp_pe_v[layer], cmp_w1_v[layer], cmp_w2_v[layer],
                             w_branch_swa[layer], w_branch_sb[layer], w_branch_nsa[layer],
                             w_out[layer])
            xb = ffn_layer(xb, norm_ffn[layer], w_gate[layer], w_up[layer], w_down[layer],
                           norm_final if layer == depth - 1 else None)
        outs.append(xb)
    return jnp.stack(outs, axis=0)
```

```python
import functools

import jax
import jax.numpy as jnp
import numpy as np
from jax import lax
from jax.experimental import pallas as pl
from jax.experimental.pallas import tpu as pltpu

F32 = jnp.float32
BF16 = jnp.bfloat16

D_MODEL = 2048
Q_BLOCK = 128
LANES = 128
HALF = 64
N_PAIR = 4
SWA_WINDOW = 128
NSA_WINDOW = 512
CMP_BLOCK = 32
CMP_STRIDE = 16
CMP_HIDDEN = 256
CMP_CHUNK = 256
SEL_BLOCK = 64
SEL_TOPK = 8
SB_HEADS = 4
SB_TILE = 256
D_FF = 5632
NEG = -1e30
SEL_FORCE = 1e9
SEL_TAKEN = -3e38
RMS_EPS = 1e-6
LOG2E = 1.4426950408889634
SB_DEAD_BITS = 150.0
VMEM_LIMIT = 56 * 1024 * 1024

N_ATT = 3584
N_GC = 24
N_GATES = 3 * D_MODEL
IN_PROJ_TM = 1024
IN_PROJ_TN = 512
FFN_TN = 512
ROW_TM = 256
COL_QA, COL_QC, COL_QB, COL_KB, COL_VB = 0, 4, 8, 12, 16
COL_KA, COL_KC, COL_VC, COL_KS, COL_KW = 20, 21, 22, 23, 24
COL_VA, COL_VS, COL_VW = 25, 26, 27
TRANSPOSE_ROWS = 2048


def _nt(a, b):
    return lax.dot_general(a, b, (((1,), (1,)), ((), ())), preferred_element_type=F32)


def _dot(a, b):
    return jnp.dot(a, b, preferred_element_type=F32)


def _params(*sem):
    return pltpu.CompilerParams(dimension_semantics=sem, vmem_limit_bytes=VMEM_LIMIT)


def _head_slope(pair, group):
    return 2.0 ** -(group * N_PAIR + pair + 1)


def _in_proj_kernel(x_ref, g_ref, w_ref, oa_ref, or_ref, h_ref, *, att_tiles):
    j = pl.program_id(1)

    @pl.when(j == 0)
    def _():
        x = x_ref[...]
        ms = jnp.mean(x * x, axis=-1, keepdims=True)
        h_ref[...] = (x * lax.rsqrt(ms + RMS_EPS) * g_ref[...]).astype(BF16)

    y = _dot(h_ref[...], w_ref[...])

    @pl.when(j < att_tiles)
    def _():
        oa_ref[...] = y.astype(oa_ref.dtype)

    @pl.when(j >= att_tiles)
    def _():
        or_ref[...] = y


def in_projection(x, g, w_all, tm, tn):
    s, d = x.shape
    n_tiles = w_all.shape[1] // tn
    att_tiles = N_ATT // tn
    rest_tiles = n_tiles - att_tiles
    kern = functools.partial(_in_proj_kernel, att_tiles=att_tiles)
    return pl.pallas_call(
        kern,
        grid=(s // tm, n_tiles),
        in_specs=[pl.BlockSpec((tm, d), lambda i, j: (i, 0)),
                  pl.BlockSpec((1, d), lambda i, j: (0, 0)),
                  pl.BlockSpec((d, tn), lambda i, j: (0, j))],
        out_specs=[pl.BlockSpec((tm, tn), lambda i, j: (i, jnp.minimum(j, att_tiles - 1))),
                   pl.BlockSpec((tm, tn), lambda i, j: (i, jnp.maximum(j - att_tiles, 0)))],
        out_shape=[jax.ShapeDtypeStruct((s, N_ATT), BF16),
                   jax.ShapeDtypeStruct((s, rest_tiles * tn), F32)],
        scratch_shapes=[pltpu.VMEM((tm, d), BF16)],
        compiler_params=_params("parallel", "arbitrary"),
        name="in_projection",
    )(x, g.reshape(1, d), w_all)


def _mm_res_kernel(a_ref, w_ref, r_ref, *rest, final_norm):
    y = r_ref[...] + _dot(a_ref[...], w_ref[...])
    if final_norm:
        g_ref, o_ref = rest
        ms = jnp.mean(y * y, axis=-1, keepdims=True)
        y = y * lax.rsqrt(ms + RMS_EPS) * g_ref[...]
    else:
        o_ref, = rest
    o_ref[...] = y


def matmul_residual(a, w, res, tm, norm_g=None):
    s, k = a.shape
    n = w.shape[1]
    in_specs = [pl.BlockSpec((tm, k), lambda i: (i, 0)),
                pl.BlockSpec((k, n), lambda i: (0, 0), pipeline_mode=pl.Buffered(1)),
                pl.BlockSpec((tm, n), lambda i: (i, 0))]
    args = [a, w, res]
    if norm_g is not None:
        in_specs.append(pl.BlockSpec((1, n), lambda i: (0, 0)))
        args.append(norm_g.reshape(1, n))
    return pl.pallas_call(
        functools.partial(_mm_res_kernel, final_norm=norm_g is not None),
        grid=(s // tm,),
        in_specs=in_specs,
        out_specs=pl.BlockSpec((tm, n), lambda i: (i, 0)),
        out_shape=jax.ShapeDtypeStruct((s, n), F32),
        compiler_params=_params("parallel"),
        name="matmul_residual",
    )(*args)


def _merge_out_kernel(ya_ref, yb_ref, yc_ref, wa_ref, wb_ref, wc_ref,
                      ga_ref, gb_ref, gc_ref, wo_ref, x_ref, o_ref):
    m = jax.nn.sigmoid(ga_ref[...]) * _dot(ya_ref[...], wa_ref[...])
    m = m + jax.nn.sigmoid(gb_ref[...]) * _dot(yb_ref[...], wb_ref[...])
    m = m + jax.nn.sigmoid(gc_ref[...]) * _dot(yc_ref[...], wc_ref[...])
    o_ref[...] = x_ref[...] + _dot(m.astype(BF16), wo_ref[...])


def merge_out_projection(ya, yb, yc, wa, wb, wc, y_rest, w_out, x, tm):
    s, k = ya.shape
    n = wa.shape[1]
    resident = pl.Buffered(1)
    y_spec = pl.BlockSpec((tm, k), lambda i: (i, 0))
    w_spec = pl.BlockSpec((k, n), lambda i: (0, 0), pipeline_mode=resident)

    def gate_spec(br):
        return pl.BlockSpec((tm, n), lambda i: (i, br))

    return pl.pallas_call(
        _merge_out_kernel,
        grid=(s // tm,),
        in_specs=[y_spec, y_spec, y_spec, w_spec, w_spec, w_spec,
                  gate_spec(0), gate_spec(1), gate_spec(2),
                  pl.BlockSpec((n, n), lambda i: (0, 0), pipeline_mode=resident),
                  pl.BlockSpec((tm, n), lambda i: (i, 0))],
        out_specs=pl.BlockSpec((tm, n), lambda i: (i, 0)),
        out_shape=jax.ShapeDtypeStruct((s, n), F32),
        compiler_params=_params("parallel"),
        name="merge_out_projection",
    )(ya, yb, yc, wa, wb, wc, y_rest, y_rest, y_rest, w_out, x)


def _ffn_up_kernel(x_ref, g_ref, wg_ref, wu_ref, o_ref, h_ref):
    @pl.when(pl.program_id(1) == 0)
    def _():
        x = x_ref[...]
        ms = jnp.mean(x * x, axis=-1, keepdims=True)
        h_ref[...] = (x * lax.rsqrt(ms + RMS_EPS) * g_ref[...]).astype(BF16)

    h = h_ref[...]
    o_ref[...] = (jax.nn.silu(_dot(h, wg_ref[...])) * _dot(h, wu_ref[...])).astype(o_ref.dtype)


def ffn_up(x, g, wg, wu, tm, tn):
    s, d = x.shape
    n = wg.shape[1]
    w_spec = pl.BlockSpec((d, tn), lambda i, j: (0, j))
    return pl.pallas_call(
        _ffn_up_kernel,
        grid=(s // tm, n // tn),
        in_specs=[pl.BlockSpec((tm, d), lambda i, j: (i, 0)),
                  pl.BlockSpec((1, d), lambda i, j: (0, 0)),
                  w_spec, w_spec],
        out_specs=pl.BlockSpec((tm, tn), lambda i, j: (i, j)),
        out_shape=jax.ShapeDtypeStruct((s, n), BF16),
        scratch_shapes=[pltpu.VMEM((tm, d), BF16)],
        compiler_params=_params("parallel", "arbitrary"),
        name="ffn_up",
    )(x, g.reshape(1, d), wg, wu)


def _half_masks():
    lane = lax.broadcasted_iota(jnp.int32, (Q_BLOCK, LANES), 1)
    return lane < HALF


def _scaled_q(q):
    return q * jnp.asarray(HALF ** -0.5, q.dtype)


def _banded_kernel(*refs, n_prev, window, has_sink):
    nk = n_prev + 1
    if has_sink:
        sink_ref, refs = refs[0], refs[1:]
    q_ref = refs[0]
    k_refs = refs[1:1 + nk]
    v_refs = refs[1 + nk:1 + 2 * nk]
    o_ref = refs[1 + 2 * nk]
    i = pl.program_id(0)
    span = nk * Q_BLOCK
    k_all = jnp.concatenate([k_refs[d][...] for d in range(n_prev, -1, -1)], axis=0)
    vt_all = jnp.concatenate([v_refs[d][...] for d in range(n_prev, -1, -1)], axis=1)
    key_rel = lax.broadcasted_iota(jnp.int32, (span, Q_BLOCK), 0)
    q_rel = lax.broadcasted_iota(jnp.int32, (span, Q_BLOCK), 1)
    dist = q_rel + n_prev * Q_BLOCK - key_rel
    key_pos = (i - n_prev) * Q_BLOCK + key_rel
    mask = (dist >= 0) & (dist < window) & (key_pos >= 0)
    bias = jnp.where(mask, 0.0, NEG)
    distf = dist.astype(F32)
    lo = _half_masks()
    top = lax.broadcasted_iota(jnp.int32, (Q_BLOCK, Q_BLOCK), 0) < HALF
    for pair in range(N_PAIR):
        q128 = _scaled_q(q_ref[:, pair * LANES:(pair + 1) * LANES])
        halves = []
        for group in range(2):
            qm = jnp.where(lo if group == 0 else jnp.logical_not(lo), q128, jnp.zeros_like(q128))
            s = _nt(k_all, qm) - _head_slope(pair, group) * distf + bias
            m = jnp.max(s, axis=0, keepdims=True)
            if has_sink:
                sink = sink_ref[group * N_PAIR + pair]
                m = jnp.maximum(m, sink)
                p = jnp.exp(s - m)
                denom = jnp.sum(p, axis=0, keepdims=True) + jnp.exp(sink - m)
            else:
                p = jnp.exp(s - m)
                denom = jnp.maximum(jnp.sum(p, axis=0, keepdims=True), 1e-30)
            pv = _dot(vt_all, p.astype(BF16))
            halves.append(pv * (1.0 / denom))
        ot = jnp.where(top, halves[0], halves[1])
        o_ref[:, pair * LANES:(pair + 1) * LANES] = ot.T.astype(o_ref.dtype)


def banded_attention(y_att, vt4, vt_index, col_q, col_k, window, sinks, out_dtype):
    s = y_att.shape[0]
    nb = s // Q_BLOCK
    n_prev = -(-window // Q_BLOCK)
    has_sink = sinks is not None
    in_specs = []
    args = []
    if has_sink:
        in_specs.append(pl.BlockSpec(memory_space=pltpu.SMEM))
        args.append(sinks)
    in_specs.append(pl.BlockSpec((Q_BLOCK, N_PAIR * LANES), lambda i: (i, col_q // N_PAIR)))
    args.append(y_att)
    for d in range(n_prev + 1):
        in_specs.append(pl.BlockSpec((Q_BLOCK, LANES),
                                     lambda i, d=d: (jnp.maximum(i - d, 0), col_k)))
        args.append(y_att)
    for d in range(n_prev + 1):
        in_specs.append(pl.BlockSpec((None, None, LANES, Q_BLOCK),
                                     lambda i, d=d: (vt_index, jnp.maximum(i - d, 0), 0, 0)))
        args.append(vt4)
    kern = functools.partial(_banded_kernel, n_prev=n_prev, window=window, has_sink=has_sink)
    return pl.pallas_call(
        kern,
        grid=(nb,),
        in_specs=in_specs,
        out_specs=pl.BlockSpec((Q_BLOCK, N_PAIR * LANES), lambda i: (i, 0)),
        out_shape=jax.ShapeDtypeStruct((s, N_PAIR * LANES), out_dtype),
        compiler_params=_params("parallel"),
        name="banded_attention_w%d" % window,
    )(*args)


def _sb_kernel(q_ref, k_ref, vt_ref, o_ref, ls_ref, acc_ref, *, scale, heads):
    i = pl.program_id(1)
    row = lax.broadcasted_iota(jnp.int32, (SB_TILE, SB_TILE), 0)
    col = lax.broadcasted_iota(jnp.int32, (SB_TILE, SB_TILE), 1)
    later = jnp.where(col > row, 1.0, 0.0).astype(BF16)
    before = row < col

    def visit(kj, diagonal):
        gone = [ls_ref[h] for h in range(heads)]
        pv, gone_new = [], []
        for h in range(heads):
            q = q_ref[:, h * LANES:(h + 1) * LANES]
            k = k_ref[pl.ds(pl.multiple_of(kj * SB_TILE, SB_TILE), SB_TILE),
                      h * LANES:(h + 1) * LANES]
            zz = _nt(k, q) * (scale * LOG2E)
            sp_raw = jnp.maximum(zz, 0.0) + jnp.log(1.0 + jnp.exp2(-jnp.abs(zz))) * LOG2E
            sp = jnp.where(before, sp_raw, 0.0) if diagonal else sp_raw
            a = jnp.exp2((zz - sp_raw) - _dot(later, sp.astype(BF16)) - gone[h])
            if diagonal:
                a = jnp.where(before, a, 0.0)
            pv.append(_dot(vt_ref[h, kj], a.astype(BF16)))
            gone_new.append(gone[h] + jnp.sum(sp, axis=0, keepdims=True))
        alive = gone_new[0]
        for h in range(heads):
            acc_ref[h] += pv[h]
            ls_ref[h] = gone_new[h]
            alive = jnp.minimum(alive, gone_new[h])
        return jnp.min(alive)

    ls_ref[...] = jnp.zeros(ls_ref.shape, F32)
    acc_ref[...] = jnp.zeros(acc_ref.shape, F32)
    alive0 = visit(i, True)

    def cond(carry):
        return (carry[0] < i) & (carry[1] < SB_DEAD_BITS)

    def body(carry):
        return carry[0] + 1, visit(i - 1 - carry[0], False)

    lax.while_loop(cond, body, (jnp.int32(0), alive0))
    for h in range(heads):
        o_ref[:, h * LANES:(h + 1) * LANES] = acc_ref[h].T.astype(o_ref.dtype)


def stick_breaking(y_att, vt4, heads=SB_HEADS):
    s = y_att.shape[0]
    nt = s // SB_TILE
    kern = functools.partial(_sb_kernel, scale=LANES ** -0.5, heads=heads)
    resident = pl.Buffered(1)
    return pl.pallas_call(
        kern,
        grid=(SB_HEADS // heads, nt),
        in_specs=[pl.BlockSpec((SB_TILE, heads * LANES), lambda g, i: (i, COL_QB // heads + g)),
                  pl.BlockSpec((s, heads * LANES), lambda g, i: (0, COL_KB // heads + g),
                               pipeline_mode=resident),
                  pl.BlockSpec((heads, nt, LANES, SB_TILE), lambda g, i: (g, 0, 0, 0),
                               pipeline_mode=resident)],
        out_specs=pl.BlockSpec((SB_TILE, heads * LANES), lambda g, i: (i, g)),
        out_shape=jax.ShapeDtypeStruct((s, SB_HEADS * LANES), BF16),
        scratch_shapes=[pltpu.VMEM((heads, 1, SB_TILE), F32),
                        pltpu.VMEM((heads, LANES, SB_TILE), F32)],
        compiler_params=_params("parallel", "arbitrary"),
        name="stick_breaking",
    )(y_att, y_att, vt4)


def _compress_kernel(t_ref, pe_ref, w1_ref, w2_ref, o_ref):
    t = t_ref[...]
    half = w1_ref.shape[0] // 2
    w1 = w1_ref[...]
    a = _dot(t, w1[:half])
    b = _dot(t, w1[half:])
    bias = _dot(pe_ref[...], w1)[0:1]
    n = t.shape[0]
    pre = a + pltpu.roll(b, n - 1, 0) + bias
    hid = jax.nn.gelu(pre)
    o_ref[...] = _dot(hid.astype(BF16), w2_ref[...]).astype(o_ref.dtype)


def compress_blocks(t_flat, pe, w1, w2):
    g, n, k = t_flat.shape
    pe_rows = jnp.zeros((8, 2 * k), BF16).at[0].set(pe.reshape(-1).astype(BF16))
    return pl.pallas_call(
        _compress_kernel,
        grid=(g,),
        in_specs=[pl.BlockSpec((None, n, k), lambda gi: (gi, 0, 0)),
                  pl.BlockSpec((8, 2 * k), lambda gi: (0, 0)),
                  pl.BlockSpec((2 * k, CMP_HIDDEN), lambda gi: (0, 0)),
                  pl.BlockSpec((CMP_HIDDEN, HALF), lambda gi: (0, 0))],
        out_specs=pl.BlockSpec((None, n, HALF), lambda gi: (gi, 0, 0)),
        out_shape=jax.ShapeDtypeStruct((g, n, HALF), BF16),
        compiler_params=_params("parallel"),
        name="compress_blocks",
    )(t_flat, pe_rows, w1.astype(BF16), w2.astype(BF16))


def _nsa_cmp_kernel(q_ref, kc_ref, vct_ref, ov_ref, oc_ref, sel_ref, cnt_ref):
    i = pl.program_id(0)
    n_all = kc_ref.shape[0]
    chunk = min(CMP_CHUNK, n_all)
    needed = lax.div(8 * i + 7 + (chunk - 1), chunk)
    body = functools.partial(_nsa_cmp_body, q_ref, kc_ref, vct_ref, ov_ref, oc_ref, sel_ref, cnt_ref)
    for c in range(1, n_all // chunk + 1):
        pl.when(needed == c)(functools.partial(body, n_cmp=c * chunk))


def _nsa_cmp_body(q_ref, kc_ref, vct_ref, ov_ref, oc_ref, sel_ref, cnt_ref, *, n_cmp):
    i = pl.program_id(0)
    kc = kc_ref[:n_cmp, :]
    vct = vct_ref[:, :n_cmp]
    t = i * Q_BLOCK + lax.broadcasted_iota(jnp.int32, (n_cmp, Q_BLOCK), 1)
    cmp_end = lax.broadcasted_iota(jnp.int32, (n_cmp, Q_BLOCK), 0) * CMP_STRIDE + (CMP_BLOCK - 1)
    dist = t - cmp_end
    bias = jnp.where(dist >= 0, 0.0, NEG)
    seen = jnp.where(t[0:1] >= CMP_BLOCK - 1, 1.0, 0.0)
    distf = dist.astype(F32)
    lo = _half_masks()
    top = lax.broadcasted_iota(jnp.int32, (Q_BLOCK, Q_BLOCK), 0) < HALF
    psum = [jnp.zeros((n_cmp, Q_BLOCK), F32), jnp.zeros((n_cmp, Q_BLOCK), F32)]
    for pair in range(N_PAIR):
        q128 = _scaled_q(q_ref[:, pair * LANES:(pair + 1) * LANES])
        halves = []
        for group in range(2):
            qm = jnp.where(lo if group == 0 else jnp.logical_not(lo), q128, jnp.zeros_like(q128))
            s = _nt(kc, qm) - _head_slope(pair, group) * distf + bias
            m = jnp.max(s, axis=0, keepdims=True)
            p = jnp.exp(s - m)
            p = p * (seen / jnp.maximum(jnp.sum(p, axis=0, keepdims=True) * seen, 1e-30))
            halves.append(_dot(vct, p.astype(BF16)))
            psum[group] = psum[group] + p
        ot = jnp.where(top, halves[0], halves[1])
        oc_ref[:, pair * LANES:(pair + 1) * LANES] = ot.T

    n_sel_all = ov_ref.shape[0]
    n_sel = min(n_sel_all, n_cmp // (SEL_BLOCK // CMP_STRIDE))
    def all_rows(x):
        if n_sel == n_sel_all:
            return x
        return jnp.concatenate([x, jnp.zeros((n_sel_all - n_sel, Q_BLOCK), F32)], axis=0)

    ov = ov_ref[:n_sel, :n_cmp]
    tq = i * Q_BLOCK + lax.broadcasted_iota(jnp.int32, (n_sel, Q_BLOCK), 1)
    jblk = lax.broadcasted_iota(jnp.int32, (n_sel, Q_BLOCK), 0)
    jblk_f = jblk.astype(F32)
    jt = tq // SEL_BLOCK
    valid = jblk * SEL_BLOCK <= tq
    forced = (jblk == 0) | (jblk == jt) | (jblk == jt - 1)
    total = jnp.zeros((n_sel, Q_BLOCK), F32)
    for group in range(2):
        hi = psum[group].astype(BF16)
        lo_part = (psum[group] - hi.astype(F32)).astype(BF16)
        imp = _dot(ov, hi) + _dot(ov, lo_part)
        score = jnp.where(valid, jnp.where(forced, SEL_FORCE, imp), -SEL_FORCE)
        sel = jnp.zeros((n_sel, Q_BLOCK), F32)
        for _ in range(min(SEL_TOPK, n_sel)):
            best = jnp.max(score, axis=0, keepdims=True)
            first = jnp.min(jnp.where(score == best, jblk_f, float(n_sel)), axis=0, keepdims=True)
            hit = jblk_f == first
            sel = jnp.where(hit, 1.0, sel)
            score = jnp.where(hit, SEL_TAKEN, score)
        sel_ref[0, group] = all_rows(sel)
        total = total + sel
    cnt_ref[0] = _nt(jnp.ones((8, Q_BLOCK), BF16), all_rows(total).astype(BF16))


def nsa_compressed(y_att, kc128, vct128, overlap_t):
    s = y_att.shape[0]
    nb = s // Q_BLOCK
    n_cmp = kc128.shape[0]
    n_sel = overlap_t.shape[0]
    return pl.pallas_call(
        _nsa_cmp_kernel,
        grid=(nb,),
        in_specs=[pl.BlockSpec((Q_BLOCK, N_PAIR * LANES), lambda i: (i, COL_QC // N_PAIR)),
                  pl.BlockSpec((n_cmp, LANES), lambda i: (0, 0)),
                  pl.BlockSpec((LANES, n_cmp), lambda i: (0, 0)),
                  pl.BlockSpec((n_sel, n_cmp), lambda i: (0, 0))],
        out_specs=[pl.BlockSpec((Q_BLOCK, N_PAIR * LANES), lambda i: (i, 0)),
                   pl.BlockSpec((1, 2, n_sel, Q_BLOCK), lambda i: (i, 0, 0, 0)),
                   pl.BlockSpec((1, 8, n_sel), lambda i: (i, 0, 0))],
        out_shape=[jax.ShapeDtypeStruct((s, N_PAIR * LANES), F32),
                   jax.ShapeDtypeStruct((nb, 2, n_sel, Q_BLOCK), F32),
                   jax.ShapeDtypeStruct((nb, 8, n_sel), F32)],
        compiler_params=_params("parallel"),
        name="nsa_compressed",
    )(y_att, kc128, vct128, overlap_t)


def _nsa_sel_kernel(todo_ref, count_ref, q_ref, ks_ref, vst_ref, sel_ref, slope_ref, gc_ref, oc_ref,
                    ow_ref, o_ref, qm_ref, m_ref, l_ref, acc_ref, *, n_tiles):
    i = pl.program_id(0)
    n_heads = 2 * N_PAIR
    n_active = count_ref[i]
    lo = _half_masks()
    for pair in range(N_PAIR):
        q128 = _scaled_q(q_ref[:, pair * LANES:(pair + 1) * LANES])
        for group in range(2):
            h = pair * 2 + group
            qm_ref[h * Q_BLOCK:(h + 1) * Q_BLOCK, :] = jnp.where(
                lo if group == 0 else jnp.logical_not(lo), q128, jnp.zeros_like(q128))
    m_ref[...] = jnp.full(m_ref.shape, NEG, F32)
    l_ref[...] = jnp.zeros(l_ref.shape, F32)
    acc_ref[...] = jnp.zeros(acc_ref.shape, F32)
    top = lax.broadcasted_iota(jnp.int32, (Q_BLOCK, Q_BLOCK), 0) < HALF
    slopes = slope_ref[...]

    def tile_terms(p, live):
        key_pos = p * Q_BLOCK + lax.broadcasted_iota(jnp.int32, (Q_BLOCK, Q_BLOCK), 0)
        t = i * Q_BLOCK + lax.broadcasted_iota(jnp.int32, (Q_BLOCK, Q_BLOCK), 1)
        dist = t - key_pos
        bias = []
        for group in range(2):
            r0 = sel_ref[0, group, pl.ds(2 * p, 1), :]
            r1 = sel_ref[0, group, pl.ds(2 * p + 1, 1), :]
            picked = jnp.where(top, r0, r1)
            masked = jnp.where((picked > 0.5) & (dist >= 0), 0.0, NEG)
            bias.append(masked if live is True else jnp.where(live, masked, NEG))
        return ks_ref[p], vst_ref[p], dist.astype(F32), bias

    def body(step, carry):
        first = 2 * step
        has_second = first + 1 < n_active
        p0 = todo_ref[i * n_tiles + first]
        p1 = todo_ref[i * n_tiles + jnp.where(has_second, first + 1, first)]
        k0, vt0, dist0, bias0 = tile_terms(p0, True)
        k1, vt1, dist1, bias1 = tile_terms(p1, has_second)
        k = jnp.concatenate([k0, k1], axis=0)
        vt = jnp.concatenate([vt0, vt1], axis=1)
        bias_all = jnp.concatenate(
            [jnp.concatenate([bias0[h % 2], bias1[h % 2]], axis=0) for h in range(n_heads)], axis=1)
        dist_all = jnp.concatenate([jnp.concatenate([dist0, dist1], axis=0)] * n_heads, axis=1)
        s = _nt(k, qm_ref[...]) - slopes * dist_all + bias_all
        m_old = m_ref[...]
        m_new = jnp.maximum(m_old, jnp.max(s, axis=0, keepdims=True))
        alpha = jnp.exp(m_old - m_new)
        pr = jnp.where(s > 0.5 * NEG, jnp.exp(s - m_new), 0.0)
        l_ref[...] = alpha * l_ref[...] + jnp.sum(pr, axis=0, keepdims=True)
        pv = _dot(vt, pr.astype(BF16))
        own = jnp.concatenate(
            [pv[(h % 2) * HALF:(h % 2 + 1) * HALF, h * Q_BLOCK:(h + 1) * Q_BLOCK]
             for h in range(n_heads)], axis=1)
        acc_ref[...] = alpha * acc_ref[...] + own
        m_ref[...] = m_new
        return carry

    lax.fori_loop(0, lax.div(n_active + 1, 2), body, 0)

    gates = jax.nn.sigmoid(gc_ref[...])
    o_all = acc_ref[...] * (1.0 / jnp.maximum(l_ref[...], 1e-30))
    for pair in range(N_PAIR):
        h = pair * 2
        o_s = jnp.concatenate([o_all[:, h * Q_BLOCK:(h + 1) * Q_BLOCK],
                               o_all[:, (h + 1) * Q_BLOCK:(h + 2) * Q_BLOCK]], axis=0).T
        cols = slice(pair * LANES, (pair + 1) * LANES)

        def gate(branch):
            c = branch * 8 + pair * 2
            return jnp.where(lo, gates[:, c:c + 1], gates[:, c + 1:c + 2])

        y = gate(0) * oc_ref[:, cols] + gate(1) * o_s + gate(2) * ow_ref[:, cols]
        o_ref[:, cols] = y.astype(o_ref.dtype)


def _slope_row():
    row = np.concatenate([np.full((Q_BLOCK,), _head_slope(h // 2, h % 2), np.float32)
                          for h in range(2 * N_PAIR)])
    return jnp.asarray(row.reshape(1, -1))


def _visit_lists(cnt):
    nb = cnt.shape[0]
    tile = jnp.arange(nb, dtype=jnp.int32)
    picked = (cnt[:, 0, :].reshape(nb, nb, 2).sum(-1) > 0.5) & (tile[None, :] <= tile[:, None])
    slot = jnp.cumsum(picked.astype(jnp.int32), axis=1) - 1
    hit = picked[:, :, None] & (slot[:, :, None] == tile[None, None, :])
    todo = jnp.sum(jnp.where(hit, tile[None, :, None], 0), axis=1)
    return todo.reshape(-1), picked.sum(axis=1).astype(jnp.int32)


def nsa_selected(cnt, y_att, ks3, vt4, vt_index, sel_t, y_rest, o_c, o_w):
    s = y_att.shape[0]
    nb = s // Q_BLOCK
    n_sel = sel_t.shape[2]
    n_heads = 2 * N_PAIR
    todo, count = _visit_lists(cnt)
    kern = functools.partial(_nsa_sel_kernel, n_tiles=nb)
    wide = pl.BlockSpec((Q_BLOCK, N_PAIR * LANES), lambda i, *_: (i, 0))
    grid_spec = pltpu.PrefetchScalarGridSpec(
        num_scalar_prefetch=2,
        grid=(nb,),
        in_specs=[pl.BlockSpec((Q_BLOCK, N_PAIR * LANES), lambda i, *_: (i, COL_QC // N_PAIR)),
                  pl.BlockSpec((nb, Q_BLOCK, LANES), lambda i, *_: (0, 0, 0)),
                  pl.BlockSpec((None, nb, LANES, Q_BLOCK), lambda i, *_: (vt_index, 0, 0, 0)),
                  pl.BlockSpec((1, 2, n_sel, Q_BLOCK), lambda i, *_: (i, 0, 0, 0)),
                  pl.BlockSpec((1, n_heads * Q_BLOCK), lambda i, *_: (0, 0)),
                  pl.BlockSpec((Q_BLOCK, LANES), lambda i, *_: (i, N_GATES // LANES)),
                  wide, wide],
        out_specs=wide,
        scratch_shapes=[pltpu.VMEM((n_heads * Q_BLOCK, LANES), BF16),
                        pltpu.VMEM((1, n_heads * Q_BLOCK), F32),
                        pltpu.VMEM((1, n_heads * Q_BLOCK), F32),
                        pltpu.VMEM((HALF, n_heads * Q_BLOCK), F32)],
    )
    return pl.pallas_call(
        kern,
        grid_spec=grid_spec,
        out_shape=jax.ShapeDtypeStruct((s, N_PAIR * LANES), BF16),
        compiler_params=_params("arbitrary"),
        name="nsa_selected",
    )(todo, count, y_att, ks3, vt4, sel_t, _slope_row(), y_rest, o_c, o_w)


def _pair_cols(w):
    lead = w.shape[:-1]
    return w.reshape(*lead, 2, N_PAIR, HALF).swapaxes(-3, -2).reshape(*lead, 2 * N_PAIR * HALF)


def _pair_rows(w):
    return w.reshape(2, N_PAIR, HALF, w.shape[-1]).swapaxes(0, 1).reshape(2 * N_PAIR * HALF, w.shape[-1])


def _prep_w_in(w_in):
    qa = _pair_cols(w_in[:, 0:512])
    qc = _pair_cols(w_in[:, 2304:2816])
    cols = lambda a, b: w_in[:, a:b]
    w_att = jnp.concatenate(
        [qa, qc, cols(768, 2304),
         cols(512, 640), cols(2816, 3072), cols(3072, 3200), cols(3328, 3456),
         cols(640, 768), cols(3200, 3328), cols(3456, 3584)],
        axis=1).astype(BF16)
    gc = w_in[:, N_ATT:N_ATT + N_GC].reshape(-1, 2, N_PAIR, 3)
    gc = gc.transpose(0, 3, 2, 1).reshape(-1, N_GC)
    gc = jnp.pad(gc, ((0, 0), (0, IN_PROJ_TN - N_GC)))
    return jnp.concatenate([w_att, w_in[:, N_ATT + N_GC:].astype(BF16), gc.astype(BF16)], axis=1)


def _transpose_kernel(x_ref, o_ref):
    tile = o_ref.shape[-1]
    for c in range(o_ref.shape[0]):
        o_ref[c] = x_ref[c * tile:(c + 1) * tile, :].T


def transpose_tiles(y_att, col, width, tile):
    s = y_att.shape[0]
    rows = min(TRANSPOSE_ROWS, s)
    return pl.pallas_call(
        _transpose_kernel,
        grid=(width, s // rows),
        in_specs=[pl.BlockSpec((rows, LANES), lambda c, r: (r, col + c))],
        out_specs=pl.BlockSpec((None, rows // tile, LANES, tile), lambda c, r: (c, r, 0, 0)),
        out_shape=jax.ShapeDtypeStruct((width, s // tile, LANES, tile), y_att.dtype),
        compiler_params=_params("parallel", "parallel"),
        name="transpose_tiles",
    )(y_att)


def _blocks(y_att, col, width=1, tile=Q_BLOCK):
    s = y_att.shape[0]
    t = y_att[:, col * LANES:(col + width) * LANES].reshape(s // tile, tile, width, LANES)
    return t.transpose(2, 0, 1, 3)


def _chunk_rows(y_att, col):
    s = y_att.shape[0]
    t = y_att[:, col * LANES:(col + 1) * LANES].reshape(s // CMP_STRIDE, CMP_STRIDE, 2, HALF)
    return t.transpose(2, 0, 1, 3).reshape(2, s // CMP_STRIDE, CMP_STRIDE * HALF)


def _overlap_t(s):
    n_cmp = s // CMP_STRIDE
    n_sel = s // SEL_BLOCK
    cmp_lo = np.arange(n_cmp) * CMP_STRIDE
    cmp_end = cmp_lo + CMP_BLOCK - 1
    sel_lo = np.arange(n_sel) * SEL_BLOCK
    ov = ((cmp_lo[None, :] <= sel_lo[:, None] + SEL_BLOCK - 1)
          & (cmp_end[None, :] >= sel_lo[:, None]) & (np.arange(n_cmp)[None, :] < n_cmp - 1))
    return jnp.asarray(ov.astype(np.float32), dtype=BF16)


def nsa_attention(y_att, y_rest, vt4, pe_k, w1_k, w2_k, pe_v, w1_v, w2_v):
    s = y_att.shape[0]
    kc = compress_blocks(_chunk_rows(y_att, COL_KC), pe_k, w1_k, w2_k)
    vc = compress_blocks(_chunk_rows(y_att, COL_VC), pe_v, w1_v, w2_v)
    kc128 = kc.transpose(1, 0, 2).reshape(kc.shape[1], LANES)
    vct128 = vc.transpose(0, 2, 1).reshape(LANES, vc.shape[1])
    o_c, sel_t, cnt = nsa_compressed(y_att, kc128, vct128, _overlap_t(s))
    o_w = banded_attention(y_att, vt4, COL_VW - COL_VA, COL_QC, COL_KW, NSA_WINDOW, None, F32)
    ks3 = _blocks(y_att, COL_KS)[0]
    return nsa_selected(cnt, y_att, ks3, vt4, COL_VS - COL_VA, sel_t, y_rest, o_c, o_w)


def mixer_layer(x, norm_mix, w_in, sinks, pe_k, w1_k, w2_k, pe_v, w1_v, w2_v,
                w_br_a, w_br_b, w_br_c, w_out):
    y_att, y_rest = in_projection(x, norm_mix, _prep_w_in(w_in), IN_PROJ_TM, IN_PROJ_TN)
    vt4 = transpose_tiles(y_att, COL_VA, 3, Q_BLOCK)
    y_a = banded_attention(y_att, vt4, 0, COL_QA, COL_KA, SWA_WINDOW, sinks, BF16)
    y_b = stick_breaking(y_att, transpose_tiles(y_att, COL_VB, SB_HEADS, SB_TILE))
    y_c = nsa_attention(y_att, y_rest, vt4, pe_k, w1_k, w2_k, pe_v, w1_v, w2_v)
    return merge_out_projection(y_a, y_b, y_c, _pair_rows(w_br_a).astype(BF16),
                                w_br_b.astype(BF16), _pair_rows(w_br_c).astype(BF16),
                                y_rest, w_out.astype(BF16), x, ROW_TM)


def ffn_layer(x, norm_ffn, w_gate, w_up, w_down, norm_after=None):
    u = ffn_up(x, norm_ffn, w_gate.astype(BF16), w_up.astype(BF16), IN_PROJ_TM, FFN_TN)
    return matmul_residual(u, w_down.astype(BF16), x, ROW_TM, norm_after)


def kernel(x, norm_mix, w_in, swa_sinks, cmp_pe_k, cmp_w1_k, cmp_w2_k, cmp_pe_v, cmp_w1_v,
           cmp_w2_v, w_branch_swa, w_branch_sb, w_branch_nsa, w_out, norm_ffn, w_gate, w_up,
           w_down, norm_final):
    b, s, d = x.shape
    outs = []
    for bi in range(b):
        xb = x[bi]
        depth = norm_mix.shape[0]
        assert depth >= 1
        for layer in range(depth):
            xb = mixer_layer(xb, norm_mix[layer], w_in[layer], swa_sinks[layer],
                             cmp_pe_k[layer], cmp_w1_k[layer], cmp_w2_k[layer],
                             cmp_pe_v[layer], cmp_w1_v[layer], cmp_w2_v[layer],
                             w_branch_swa[layer], w_branch_sb[layer], w_branch_nsa[layer],
                             w_out[layer])
            xb = ffn_layer(xb, norm_ffn[layer], w_gate[layer], w_up[layer], w_down[layer],
                           norm_final if layer == depth - 1 else None)
        outs.append(xb)
    return jnp.stack(outs, axis=0)
```

```python
import functools

import jax
import jax.numpy as jnp
import numpy as np
from jax import lax
from jax.experimental import pallas as pl
from jax.experimental.pallas import tpu as pltpu

F32 = jnp.float32
BF16 = jnp.bfloat16

D_MODEL = 2048
Q_BLOCK = 128
LANES = 128
HALF = 64
N_PAIR = 4
SWA_WINDOW = 128
NSA_WINDOW = 512
CMP_BLOCK = 32
CMP_STRIDE = 16
CMP_HIDDEN = 256
CMP_CHUNK = 256
SEL_BLOCK = 64
SEL_TOPK = 8
SB_HEADS = 4
SB_TILE = 256
D_FF = 5632
NEG = -1e30
SEL_FORCE = 1e9
SEL_TAKEN = -3e38
RMS_EPS = 1e-6
LOG2E = 1.4426950408889634
SB_DEAD_BITS = 150.0
VMEM_LIMIT = 56 * 1024 * 1024

N_ATT = 3584
N_GC = 24
N_GATES = 3 * D_MODEL
IN_PROJ_TM = 1024
IN_PROJ_TN = 512
FFN_TN = 512
ROW_TM = 256
COL_QA, COL_QC, COL_QB, COL_KB, COL_VB = 0, 4, 8, 12, 16
COL_KA, COL_KC, COL_VC, COL_KS, COL_KW = 20, 21, 22, 23, 24
COL_VA, COL_VS, COL_VW = 25, 26, 27
TRANSPOSE_ROWS = 2048


def _nt(a, b):
    return lax.dot_general(a, b, (((1,), (1,)), ((), ())), preferred_element_type=F32)


def _dot(a, b):
    return jnp.dot(a, b, preferred_element_type=F32)


def _params(*sem):
    return pltpu.CompilerParams(dimension_semantics=sem, vmem_limit_bytes=VMEM_LIMIT)


def _head_slope(pair, group):
    return 2.0 ** -(group * N_PAIR + pair + 1)


def _in_proj_kernel(x_ref, g_ref, w_ref, oa_ref, or_ref, h_ref, *, att_tiles):
    j = pl.program_id(1)

    @pl.when(j == 0)
    def _():
        x = x_ref[...]
        ms = jnp.mean(x * x, axis=-1, keepdims=True)
        h_ref[...] = (x * lax.rsqrt(ms + RMS_EPS) * g_ref[...]).astype(BF16)

    @pl.when(j < att_tiles)
    def _():
        oa_ref[...] = _dot(h_ref[...], w_ref[...]).astype(oa_ref.dtype)

    @pl.when(j >= att_tiles)
    def _():
        or_ref[...] = _dot(h_ref[...], w_ref[...])


def in_projection(x, g, w_all, tm, tn):
    s, d = x.shape
    n_tiles = w_all.shape[1] // tn
    att_tiles = N_ATT // tn
    rest_tiles = n_tiles - att_tiles
    kern = functools.partial(_in_proj_kernel, att_tiles=att_tiles)
    return pl.pallas_call(
        kern,
        grid=(s // tm, n_tiles),
        in_specs=[pl.BlockSpec((tm, d), lambda i, j: (i, 0)),
                  pl.BlockSpec((1, d), lambda i, j: (0, 0)),
                  pl.BlockSpec((d, tn), lambda i, j: (0, j))],
        out_specs=[pl.BlockSpec((tm, tn), lambda i, j: (i, jnp.minimum(j, att_tiles - 1))),
                   pl.BlockSpec((tm, tn), lambda i, j: (i, jnp.maximum(j - att_tiles, 0)))],
        out_shape=[jax.ShapeDtypeStruct((s, N_ATT), BF16),
                   jax.ShapeDtypeStruct((s, rest_tiles * tn), F32)],
        scratch_shapes=[pltpu.VMEM((tm, d), BF16)],
        compiler_params=_params("parallel", "arbitrary"),
        name="in_projection",
    )(x, g.reshape(1, d), w_all)


def _mm_res_kernel(a_ref, w_ref, r_ref, *rest, final_norm):
    y = r_ref[...] + _dot(a_ref[...], w_ref[...])
    if final_norm:
        g_ref, o_ref = rest
        ms = jnp.mean(y * y, axis=-1, keepdims=True)
        y = y * lax.rsqrt(ms + RMS_EPS) * g_ref[...]
    else:
        o_ref, = rest
    o_ref[...] = y


def matmul_residual(a, w, res, tm, norm_g=None):
    s, k = a.shape
    n = w.shape[1]
    in_specs = [pl.BlockSpec((tm, k), lambda i: (i, 0)),
                pl.BlockSpec((k, n), lambda i: (0, 0), pipeline_mode=pl.Buffered(1)),
                pl.BlockSpec((tm, n), lambda i: (i, 0))]
    args = [a, w, res]
    if norm_g is not None:
        in_specs.append(pl.BlockSpec((1, n), lambda i: (0, 0)))
        args.append(norm_g.reshape(1, n))
    return pl.pallas_call(
        functools.partial(_mm_res_kernel, final_norm=norm_g is not None),
        grid=(s // tm,),
        in_specs=in_specs,
        out_specs=pl.BlockSpec((tm, n), lambda i: (i, 0)),
        out_shape=jax.ShapeDtypeStruct((s, n), F32),
        compiler_params=_params("parallel"),
        name="matmul_residual",
    )(*args)


def _merge_out_kernel(ya_ref, yb_ref, yc_ref, wa_ref, wb_ref, wc_ref,
                      ga_ref, gb_ref, gc_ref, wo_ref, x_ref, o_ref):
    m = jax.nn.sigmoid(ga_ref[...]) * _dot(ya_ref[...], wa_ref[...])
    m = m + jax.nn.sigmoid(gb_ref[...]) * _dot(yb_ref[...], wb_ref[...])
    m = m + jax.nn.sigmoid(gc_ref[...]) * _dot(yc_ref[...], wc_ref[...])
    o_ref[...] = x_ref[...] + _dot(m.astype(BF16), wo_ref[...])


def merge_out_projection(ya, yb, yc, wa, wb, wc, y_rest, w_out, x, tm):
    s, k = ya.shape
    n = wa.shape[1]
    resident = pl.Buffered(1)
    y_spec = pl.BlockSpec((tm, k), lambda i: (i, 0))
    w_spec = pl.BlockSpec((k, n), lambda i: (0, 0), pipeline_mode=resident)

    def gate_spec(br):
        return pl.BlockSpec((tm, n), lambda i: (i, br))

    return pl.pallas_call(
        _merge_out_kernel,
        grid=(s // tm,),
        in_specs=[y_spec, y_spec, y_spec, w_spec, w_spec, w_spec,
                  gate_spec(0), gate_spec(1), gate_spec(2),
                  pl.BlockSpec((n, n), lambda i: (0, 0), pipeline_mode=resident),
                  pl.BlockSpec((tm, n), lambda i: (i, 0))],
        out_specs=pl.BlockSpec((tm, n), lambda i: (i, 0)),
        out_shape=jax.ShapeDtypeStruct((s, n), F32),
        compiler_params=_params("parallel"),
        name="merge_out_projection",
    )(ya, yb, yc, wa, wb, wc, y_rest, y_rest, y_rest, w_out, x)


def _ffn_up_kernel(x_ref, g_ref, wg_ref, wu_ref, o_ref, h_ref):
    @pl.when(pl.program_id(1) == 0)
    def _():
        x = x_ref[...]
        ms = jnp.mean(x * x, axis=-1, keepdims=True)
        h_ref[...] = (x * lax.rsqrt(ms + RMS_EPS) * g_ref[...]).astype(BF16)

    h = h_ref[...]
    o_ref[...] = (jax.nn.silu(_dot(h, wg_ref[...])) * _dot(h, wu_ref[...])).astype(o_ref.dtype)


def ffn_up(x, g, wg, wu, tm, tn):
    s, d = x.shape
    n = wg.shape[1]
    w_spec = pl.BlockSpec((d, tn), lambda i, j: (0, j))
    return pl.pallas_call(
        _ffn_up_kernel,
        grid=(s // tm, n // tn),
        in_specs=[pl.BlockSpec((tm, d), lambda i, j: (i, 0)),
                  pl.BlockSpec((1, d), lambda i, j: (0, 0)),
                  w_spec, w_spec],
        out_specs=pl.BlockSpec((tm, tn), lambda i, j: (i, j)),
        out_shape=jax.ShapeDtypeStruct((s, n), BF16),
        scratch_shapes=[pltpu.VMEM((tm, d), BF16)],
        compiler_params=_params("parallel", "arbitrary"),
        name="ffn_up",
    )(x, g.reshape(1, d), wg, wu)


def _half_masks():
    lane = lax.broadcasted_iota(jnp.int32, (Q_BLOCK, LANES), 1)
    return lane < HALF


def _scaled_q(q):
    return q * jnp.asarray(HALF ** -0.5, q.dtype)


def _banded_kernel(*refs, n_prev, window, has_sink):
    nk = n_prev + 1
    if has_sink:
        sink_ref, refs = refs[0], refs[1:]
    q_ref = refs[0]
    k_refs = refs[1:1 + nk]
    v_refs = refs[1 + nk:1 + 2 * nk]
    o_ref = refs[1 + 2 * nk]
    i = pl.program_id(0)
    span = nk * Q_BLOCK
    k_all = jnp.concatenate([k_refs[d][...] for d in range(n_prev, -1, -1)], axis=0)
    vt_all = jnp.concatenate([v_refs[d][...] for d in range(n_prev, -1, -1)], axis=1)
    key_rel = lax.broadcasted_iota(jnp.int32, (span, Q_BLOCK), 0)
    q_rel = lax.broadcasted_iota(jnp.int32, (span, Q_BLOCK), 1)
    dist = q_rel + n_prev * Q_BLOCK - key_rel
    key_pos = (i - n_prev) * Q_BLOCK + key_rel
    mask = (dist >= 0) & (dist < window) & (key_pos >= 0)
    bias = jnp.where(mask, 0.0, NEG)
    distf = dist.astype(F32)
    lo = _half_masks()
    top = lax.broadcasted_iota(jnp.int32, (Q_BLOCK, Q_BLOCK), 0) < HALF
    for pair in range(N_PAIR):
        q128 = _scaled_q(q_ref[:, pair * LANES:(pair + 1) * LANES])
        halves = []
        for group in range(2):
            qm = jnp.where(lo if group == 0 else jnp.logical_not(lo), q128, jnp.zeros_like(q128))
            s = _nt(k_all, qm) - _head_slope(pair, group) * distf + bias
            m = jnp.max(s, axis=0, keepdims=True)
            if has_sink:
                sink = sink_ref[group * N_PAIR + pair]
                m = jnp.maximum(m, sink)
                p = jnp.exp(s - m)
                denom = jnp.sum(p, axis=0, keepdims=True) + jnp.exp(sink - m)
            else:
                p = jnp.exp(s - m)
                denom = jnp.maximum(jnp.sum(p, axis=0, keepdims=True), 1e-30)
            pv = _dot(vt_all, p.astype(BF16))
            halves.append(pv * (1.0 / denom))
        ot = jnp.where(top, halves[0], halves[1])
        o_ref[:, pair * LANES:(pair + 1) * LANES] = ot.T.astype(o_ref.dtype)


def banded_attention(y_att, vt4, vt_index, col_q, col_k, window, sinks, out_dtype):
    s = y_att.shape[0]
    nb = s // Q_BLOCK
    n_prev = -(-window // Q_BLOCK)
    has_sink = sinks is not None
    in_specs = []
    args = []
    if has_sink:
        in_specs.append(pl.BlockSpec(memory_space=pltpu.SMEM))
        args.append(sinks)
    in_specs.append(pl.BlockSpec((Q_BLOCK, N_PAIR * LANES), lambda i: (i, col_q // N_PAIR)))
    args.append(y_att)
    for d in range(n_prev + 1):
        in_specs.append(pl.BlockSpec((Q_BLOCK, LANES),
                                     lambda i, d=d: (jnp.maximum(i - d, 0), col_k)))
        args.append(y_att)
    for d in range(n_prev + 1):
        in_specs.append(pl.BlockSpec((None, None, LANES, Q_BLOCK),
                                     lambda i, d=d: (vt_index, jnp.maximum(i - d, 0), 0, 0)))
        args.append(vt4)
    kern = functools.partial(_banded_kernel, n_prev=n_prev, window=window, has_sink=has_sink)
    return pl.pallas_call(
        kern,
        grid=(nb,),
        in_specs=in_specs,
        out_specs=pl.BlockSpec((Q_BLOCK, N_PAIR * LANES), lambda i: (i, 0)),
        out_shape=jax.ShapeDtypeStruct((s, N_PAIR * LANES), out_dtype),
        compiler_params=_params("parallel"),
        name="banded_attention_w%d" % window,
    )(*args)


def _sb_kernel(q_ref, k_ref, vt_ref, o_ref, ls_ref, acc_ref, *, scale, heads):
    i = pl.program_id(1)
    row = lax.broadcasted_iota(jnp.int32, (SB_TILE, SB_TILE), 0)
    col = lax.broadcasted_iota(jnp.int32, (SB_TILE, SB_TILE), 1)
    later = jnp.where(col > row, 1.0, 0.0).astype(BF16)
    before = row < col

    def visit(kj, diagonal):
        gone = [ls_ref[h] for h in range(heads)]
        pv, gone_new = [], []
        for h in range(heads):
            q = q_ref[:, h * LANES:(h + 1) * LANES]
            k = k_ref[pl.ds(pl.multiple_of(kj * SB_TILE, SB_TILE), SB_TILE),
                      h * LANES:(h + 1) * LANES]
            zz = _nt(k, q) * (scale * LOG2E)
            sp_raw = jnp.maximum(zz, 0.0) + jnp.log(1.0 + jnp.exp2(-jnp.abs(zz))) * LOG2E
            sp = jnp.where(before, sp_raw, 0.0) if diagonal else sp_raw
            a = jnp.exp2((zz - sp_raw) - _dot(later, sp.astype(BF16)) - gone[h])
            if diagonal:
                a = jnp.where(before, a, 0.0)
            pv.append(_dot(vt_ref[h, kj], a.astype(BF16)))
            gone_new.append(gone[h] + jnp.sum(sp, axis=0, keepdims=True))
        alive = gone_new[0]
        for h in range(heads):
            acc_ref[h] += pv[h]
            ls_ref[h] = gone_new[h]
            alive = jnp.minimum(alive, gone_new[h])
        return jnp.min(alive)

    ls_ref[...] = jnp.zeros(ls_ref.shape, F32)
    acc_ref[...] = jnp.zeros(acc_ref.shape, F32)
    alive0 = visit(i, True)

    def cond(carry):
        return (carry[0] < i) & (carry[1] < SB_DEAD_BITS)

    def body(carry):
        return carry[0] + 1, visit(i - 1 - carry[0], False)

    lax.while_loop(cond, body, (jnp.int32(0), alive0))
    for h in range(heads):
        o_ref[:, h * LANES:(h + 1) * LANES] = acc_ref[h].T.astype(o_ref.dtype)


def stick_breaking(y_att, vt4, heads=SB_HEADS):
    s = y_att.shape[0]
    nt = s // SB_TILE
    kern = functools.partial(_sb_kernel, scale=LANES ** -0.5, heads=heads)
    resident = pl.Buffered(1)
    return pl.pallas_call(
        kern,
        grid=(SB_HEADS // heads, nt),
        in_specs=[pl.BlockSpec((SB_TILE, heads * LANES), lambda g, i: (i, COL_QB // heads + g)),
                  pl.BlockSpec((s, heads * LANES), lambda g, i: (0, COL_KB // heads + g),
                               pipeline_mode=resident),
                  pl.BlockSpec((heads, nt, LANES, SB_TILE), lambda g, i: (g, 0, 0, 0),
                               pipeline_mode=resident)],
        out_specs=pl.BlockSpec((SB_TILE, heads * LANES), lambda g, i: (i, g)),
        out_shape=jax.ShapeDtypeStruct((s, SB_HEADS * LANES), BF16),
        scratch_shapes=[pltpu.VMEM((heads, 1, SB_TILE), F32),
                        pltpu.VMEM((heads, LANES, SB_TILE), F32)],
        compiler_params=_params("parallel", "arbitrary"),
        name="stick_breaking",
    )(y_att, y_att, vt4)


def _compress_kernel(t_ref, pe_ref, w1_ref, w2_ref, o_ref):
    t = t_ref[...]
    half = w1_ref.shape[0] // 2
    w1 = w1_ref[...]
    a = _dot(t, w1[:half])
    b = _dot(t, w1[half:])
    bias = _dot(pe_ref[...], w1)[0:1]
    n = t.shape[0]
    pre = a + pltpu.roll(b, n - 1, 0) + bias
    hid = jax.nn.gelu(pre)
    o_ref[...] = _dot(hid.astype(BF16), w2_ref[...]).astype(o_ref.dtype)


def compress_blocks(t_flat, pe, w1, w2):
    g, n, k = t_flat.shape
    pe_rows = jnp.zeros((8, 2 * k), BF16).at[0].set(pe.reshape(-1).astype(BF16))
    return pl.pallas_call(
        _compress_kernel,
        grid=(g,),
        in_specs=[pl.BlockSpec((None, n, k), lambda gi: (gi, 0, 0)),
                  pl.BlockSpec((8, 2 * k), lambda gi: (0, 0)),
                  pl.BlockSpec((2 * k, CMP_HIDDEN), lambda gi: (0, 0)),
                  pl.BlockSpec((CMP_HIDDEN, HALF), lambda gi: (0, 0))],
        out_specs=pl.BlockSpec((None, n, HALF), lambda gi: (gi, 0, 0)),
        out_shape=jax.ShapeDtypeStruct((g, n, HALF), BF16),
        compiler_params=_params("parallel"),
        name="compress_blocks",
    )(t_flat, pe_rows, w1.astype(BF16), w2.astype(BF16))


def _nsa_cmp_kernel(q_ref, kc_ref, vct_ref, ov_ref, oc_ref, sel_ref, cnt_ref):
    i = pl.program_id(0)
    n_all = kc_ref.shape[0]
    chunk = min(CMP_CHUNK, n_all)
    needed = lax.div(8 * i + 7 + (chunk - 1), chunk)
    body = functools.partial(_nsa_cmp_body, q_ref, kc_ref, vct_ref, ov_ref, oc_ref, sel_ref, cnt_ref)
    for c in range(1, n_all // chunk + 1):
        pl.when(needed == c)(functools.partial(body, n_cmp=c * chunk))


def _nsa_cmp_body(q_ref, kc_ref, vct_ref, ov_ref, oc_ref, sel_ref, cnt_ref, *, n_cmp):
    i = pl.program_id(0)
    kc = kc_ref[:n_cmp, :]
    vct = vct_ref[:, :n_cmp]
    t = i * Q_BLOCK + lax.broadcasted_iota(jnp.int32, (n_cmp, Q_BLOCK), 1)
    cmp_end = lax.broadcasted_iota(jnp.int32, (n_cmp, Q_BLOCK), 0) * CMP_STRIDE + (CMP_BLOCK - 1)
    dist = t - cmp_end
    bias = jnp.where(dist >= 0, 0.0, NEG)
    seen = jnp.where(t[0:1] >= CMP_BLOCK - 1, 1.0, 0.0)
    distf = dist.astype(F32)
    lo = _half_masks()
    top = lax.broadcasted_iota(jnp.int32, (Q_BLOCK, Q_BLOCK), 0) < HALF
    psum = [jnp.zeros((n_cmp, Q_BLOCK), F32), jnp.zeros((n_cmp, Q_BLOCK), F32)]
    for pair in range(N_PAIR):
        q128 = _scaled_q(q_ref[:, pair * LANES:(pair + 1) * LANES])
        halves = []
        for group in range(2):
            qm = jnp.where(lo if group == 0 else jnp.logical_not(lo), q128, jnp.zeros_like(q128))
            s = _nt(kc, qm) - _head_slope(pair, group) * distf + bias
            m = jnp.max(s, axis=0, keepdims=True)
            p = jnp.exp(s - m)
            p = p * (seen / jnp.maximum(jnp.sum(p, axis=0, keepdims=True) * seen, 1e-30))
            halves.append(_dot(vct, p.astype(BF16)))
            psum[group] = psum[group] + p
        ot = jnp.where(top, halves[0], halves[1])
        oc_ref[:, pair * LANES:(pair + 1) * LANES] = ot.T

    n_sel_all = ov_ref.shape[0]
    n_sel = min(n_sel_all, n_cmp // (SEL_BLOCK // CMP_STRIDE))
    def all_rows(x):
        if n_sel == n_sel_all:
            return x
        return jnp.concatenate([x, jnp.zeros((n_sel_all - n_sel, Q_BLOCK), F32)], axis=0)

    ov = ov_ref[:n_sel, :n_cmp]
    tq = i * Q_BLOCK + lax.broadcasted_iota(jnp.int32, (n_sel, Q_BLOCK), 1)
    jblk = lax.broadcasted_iota(jnp.int32, (n_sel, Q_BLOCK), 0)
    jblk_f = jblk.astype(F32)
    jt = tq // SEL_BLOCK
    valid = jblk * SEL_BLOCK <= tq
    forced = (jblk == 0) | (jblk == jt) | (jblk == jt - 1)
    total = jnp.zeros((n_sel, Q_BLOCK), F32)
    for group in range(2):
        hi = psum[group].astype(BF16)
        lo_part = (psum[group] - hi.astype(F32)).astype(BF16)
        imp = _dot(ov, hi) + _dot(ov, lo_part)
        score = jnp.where(valid, jnp.where(forced, SEL_FORCE, imp), -SEL_FORCE)
        sel = jnp.zeros((n_sel, Q_BLOCK), F32)
        for _ in range(min(SEL_TOPK, n_sel)):
            best = jnp.max(score, axis=0, keepdims=True)
            first = jnp.min(jnp.where(score == best, jblk_f, float(n_sel)), axis=0, keepdims=True)
            hit = jblk_f == first
            sel = jnp.where(hit, 1.0, sel)
            score = jnp.where(hit, SEL_TAKEN, score)
        sel_ref[0, group] = all_rows(sel)
        total = total + sel
    cnt_ref[0] = _nt(jnp.ones((8, Q_BLOCK), BF16), all_rows(total).astype(BF16))


def nsa_compressed(y_att, kc128, vct128, overlap_t):
    s = y_att.shape[0]
    nb = s // Q_BLOCK
    n_cmp = kc128.shape[0]
    n_sel = overlap_t.shape[0]
    return pl.pallas_call(
        _nsa_cmp_kernel,
        grid=(nb,),
        in_specs=[pl.BlockSpec((Q_BLOCK, N_PAIR * LANES), lambda i: (i, COL_QC // N_PAIR)),
                  pl.BlockSpec((n_cmp, LANES), lambda i: (0, 0)),
                  pl.BlockSpec((LANES, n_cmp), lambda i: (0, 0)),
                  pl.BlockSpec((n_sel, n_cmp), lambda i: (0, 0))],
        out_specs=[pl.BlockSpec((Q_BLOCK, N_PAIR * LANES), lambda i: (i, 0)),
                   pl.BlockSpec((1, 2, n_sel, Q_BLOCK), lambda i: (i, 0, 0, 0)),
                   pl.BlockSpec((1, 8, n_sel), lambda i: (i, 0, 0))],
        out_shape=[jax.ShapeDtypeStruct((s, N_PAIR * LANES), F32),
                   jax.ShapeDtypeStruct((nb, 2, n_sel, Q_BLOCK), F32),
                   jax.ShapeDtypeStruct((nb, 8, n_sel), F32)],
        compiler_params=_params("parallel"),
        name="nsa_compressed",
    )(y_att, kc128, vct128, overlap_t)


def _nsa_sel_kernel(todo_ref, count_ref, q_ref, ks_ref, vst_ref, sel_ref, slope_ref, gc_ref, oc_ref,
                    ow_ref, o_ref, qm_ref, m_ref, l_ref, acc_ref, *, n_tiles):
    i = pl.program_id(0)
    n_heads = 2 * N_PAIR
    n_active = count_ref[i]
    lo = _half_masks()
    for pair in range(N_PAIR):
        q128 = _scaled_q(q_ref[:, pair * LANES:(pair + 1) * LANES])
        for group in range(2):
            h = pair * 2 + group
            qm_ref[h * Q_BLOCK:(h + 1) * Q_BLOCK, :] = jnp.where(
                lo if group == 0 else jnp.logical_not(lo), q128, jnp.zeros_like(q128))
    m_ref[...] = jnp.full(m_ref.shape, 0.1 * NEG, F32)
    l_ref[...] = jnp.zeros(l_ref.shape, F32)
    acc_ref[...] = jnp.zeros(acc_ref.shape, F32)
    top = lax.broadcasted_iota(jnp.int32, (Q_BLOCK, Q_BLOCK), 0) < HALF
    slopes = slope_ref[...]

    def tile_terms(p, live):
        key_pos = p * Q_BLOCK + lax.broadcasted_iota(jnp.int32, (Q_BLOCK, Q_BLOCK), 0)
        t = i * Q_BLOCK + lax.broadcasted_iota(jnp.int32, (Q_BLOCK, Q_BLOCK), 1)
        dist = t - key_pos
        bias = []
        for group in range(2):
            r0 = sel_ref[0, group, pl.ds(2 * p, 1), :]
            r1 = sel_ref[0, group, pl.ds(2 * p + 1, 1), :]
            picked = jnp.where(top, r0, r1)
            masked = jnp.where((picked > 0.5) & (dist >= 0), 0.0, NEG)
            bias.append(masked if live is True else jnp.where(live, masked, NEG))
        k = ks_ref[pl.ds(pl.multiple_of(p * Q_BLOCK, Q_BLOCK), Q_BLOCK), :]
        return k, vst_ref[p], dist.astype(F32), bias

    def body(step, carry):
        first = 2 * step
        has_second = first + 1 < n_active
        p0 = todo_ref[i * n_tiles + first]
        p1 = todo_ref[i * n_tiles + jnp.where(has_second, first + 1, first)]
        k0, vt0, dist0, bias0 = tile_terms(p0, True)
        k1, vt1, dist1, bias1 = tile_terms(p1, has_second)
        k = jnp.concatenate([k0, k1], axis=0)
        vt = jnp.concatenate([vt0, vt1], axis=1)
        bias_all = jnp.concatenate(
            [jnp.concatenate([bias0[h % 2], bias1[h % 2]], axis=0) for h in range(n_heads)], axis=1)
        dist_all = jnp.concatenate([jnp.concatenate([dist0, dist1], axis=0)] * n_heads, axis=1)
        s = _nt(k, qm_ref[...]) - slopes * dist_all + bias_all
        m_old = m_ref[...]
        m_new = jnp.maximum(m_old, jnp.max(s, axis=0, keepdims=True))
        alpha = jnp.exp(m_old - m_new)
        pr = jnp.exp(s - m_new)
        l_ref[...] = alpha * l_ref[...] + jnp.sum(pr, axis=0, keepdims=True)
        pv = _dot(vt, pr.astype(BF16))
        own = jnp.concatenate(
            [pv[(h % 2) * HALF:(h % 2 + 1) * HALF, h * Q_BLOCK:(h + 1) * Q_BLOCK]
             for h in range(n_heads)], axis=1)
        acc_ref[...] = alpha * acc_ref[...] + own
        m_ref[...] = m_new
        return carry

    lax.fori_loop(0, lax.div(n_active + 1, 2), body, 0)

    gates = jax.nn.sigmoid(gc_ref[...])
    o_all = acc_ref[...] * (1.0 / jnp.maximum(l_ref[...], 1e-30))
    for pair in range(N_PAIR):
        h = pair * 2
        o_s = jnp.concatenate([o_all[:, h * Q_BLOCK:(h + 1) * Q_BLOCK],
                               o_all[:, (h + 1) * Q_BLOCK:(h + 2) * Q_BLOCK]], axis=0).T
        cols = slice(pair * LANES, (pair + 1) * LANES)

        def gate(branch):
            c = branch * 8 + pair * 2
            return jnp.where(lo, gates[:, c:c + 1], gates[:, c + 1:c + 2])

        y = gate(0) * oc_ref[:, cols] + gate(1) * o_s + gate(2) * ow_ref[:, cols]
        o_ref[:, cols] = y.astype(o_ref.dtype)


def _slope_row():
    row = np.concatenate([np.full((Q_BLOCK,), _head_slope(h // 2, h % 2), np.float32)
                          for h in range(2 * N_PAIR)])
    return jnp.asarray(row.reshape(1, -1))


def _visit_lists(cnt):
    nb = cnt.shape[0]
    tile = jnp.arange(nb, dtype=jnp.int32)
    picked = (cnt[:, 0, :].reshape(nb, nb, 2).sum(-1) > 0.5) & (tile[None, :] <= tile[:, None])
    slot = jnp.cumsum(picked.astype(jnp.int32), axis=1) - 1
    hit = picked[:, :, None] & (slot[:, :, None] == tile[None, None, :])
    todo = jnp.sum(jnp.where(hit, tile[None, :, None], 0), axis=1)
    return todo.reshape(-1), picked.sum(axis=1).astype(jnp.int32)


def nsa_selected(cnt, y_att, vt4, vt_index, sel_t, y_rest, o_c, o_w):
    s = y_att.shape[0]
    nb = s // Q_BLOCK
    n_sel = sel_t.shape[2]
    n_heads = 2 * N_PAIR
    todo, count = _visit_lists(cnt)
    kern = functools.partial(_nsa_sel_kernel, n_tiles=nb)
    wide = pl.BlockSpec((Q_BLOCK, N_PAIR * LANES), lambda i, *_: (i, 0))
    grid_spec = pltpu.PrefetchScalarGridSpec(
        num_scalar_prefetch=2,
        grid=(nb,),
        in_specs=[pl.BlockSpec((Q_BLOCK, N_PAIR * LANES), lambda i, *_: (i, COL_QC // N_PAIR)),
                  pl.BlockSpec((s, LANES), lambda i, *_: (0, COL_KS)),
                  pl.BlockSpec((None, nb, LANES, Q_BLOCK), lambda i, *_: (vt_index, 0, 0, 0)),
                  pl.BlockSpec((1, 2, n_sel, Q_BLOCK), lambda i, *_: (i, 0, 0, 0)),
                  pl.BlockSpec((1, n_heads * Q_BLOCK), lambda i, *_: (0, 0)),
                  pl.BlockSpec((Q_BLOCK, LANES), lambda i, *_: (i, N_GATES // LANES)),
                  wide, wide],
        out_specs=wide,
        scratch_shapes=[pltpu.VMEM((n_heads * Q_BLOCK, LANES), BF16),
                        pltpu.VMEM((1, n_heads * Q_BLOCK), F32),
                        pltpu.VMEM((1, n_heads * Q_BLOCK), F32),
                        pltpu.VMEM((HALF, n_heads * Q_BLOCK), F32)],
    )
    return pl.pallas_call(
        kern,
        grid_spec=grid_spec,
        out_shape=jax.ShapeDtypeStruct((s, N_PAIR * LANES), BF16),
        compiler_params=_params("arbitrary"),
        name="nsa_selected",
    )(todo, count, y_att, y_att, vt4, sel_t, _slope_row(), y_rest, o_c, o_w)


def _pair_cols(w):
    lead = w.shape[:-1]
    return w.reshape(*lead, 2, N_PAIR, HALF).swapaxes(-3, -2).reshape(*lead, 2 * N_PAIR * HALF)


def _pair_rows(w):
    return w.reshape(2, N_PAIR, HALF, w.shape[-1]).swapaxes(0, 1).reshape(2 * N_PAIR * HALF, w.shape[-1])


def _prep_w_in(w_in):
    qa = _pair_cols(w_in[:, 0:512])
    qc = _pair_cols(w_in[:, 2304:2816])
    cols = lambda a, b: w_in[:, a:b]
    w_att = jnp.concatenate(
        [qa, qc, cols(768, 2304),
         cols(512, 640), cols(2816, 3072), cols(3072, 3200), cols(3328, 3456),
         cols(640, 768), cols(3200, 3328), cols(3456, 3584)],
        axis=1).astype(BF16)
    gc = w_in[:, N_ATT:N_ATT + N_GC].reshape(-1, 2, N_PAIR, 3)
    gc = gc.transpose(0, 3, 2, 1).reshape(-1, N_GC)
    gc = jnp.pad(gc, ((0, 0), (0, IN_PROJ_TN - N_GC)))
    return jnp.concatenate([w_att, w_in[:, N_ATT + N_GC:].astype(BF16), gc.astype(BF16)], axis=1)


def _transpose_kernel(x_ref, o_ref):
    tile = o_ref.shape[-1]
    for c in range(o_ref.shape[0]):
        o_ref[c] = x_ref[c * tile:(c + 1) * tile, :].T


def transpose_tiles(y_att, col, width, tile):
    s = y_att.shape[0]
    rows = min(TRANSPOSE_ROWS, s)
    return pl.pallas_call(
        _transpose_kernel,
        grid=(width, s // rows),
        in_specs=[pl.BlockSpec((rows, LANES), lambda c, r: (r, col + c))],
        out_specs=pl.BlockSpec((None, rows // tile, LANES, tile), lambda c, r: (c, r, 0, 0)),
        out_shape=jax.ShapeDtypeStruct((width, s // tile, LANES, tile), y_att.dtype),
        compiler_params=_params("parallel", "parallel"),
        name="transpose_tiles",
    )(y_att)


def _chunk_rows(y_att, col):
    s = y_att.shape[0]
    t = y_att[:, col * LANES:(col + 1) * LANES].reshape(s // CMP_STRIDE, CMP_STRIDE, 2, HALF)
    return t.transpose(2, 0, 1, 3).reshape(2, s // CMP_STRIDE, CMP_STRIDE * HALF)


def _overlap_t(s):
    n_cmp = s // CMP_STRIDE
    n_sel = s // SEL_BLOCK
    cmp_lo = np.arange(n_cmp) * CMP_STRIDE
    cmp_end = cmp_lo + CMP_BLOCK - 1
    sel_lo = np.arange(n_sel) * SEL_BLOCK
    ov = ((cmp_lo[None, :] <= sel_lo[:, None] + SEL_BLOCK - 1)
          & (cmp_end[None, :] >= sel_lo[:, None]) & (np.arange(n_cmp)[None, :] < n_cmp - 1))
    return jnp.asarray(ov.astype(np.float32), dtype=BF16)


def nsa_attention(y_att, y_rest, vt4, pe_k, w1_k, w2_k, pe_v, w1_v, w2_v):
    s = y_att.shape[0]
    kc = compress_blocks(_chunk_rows(y_att, COL_KC), pe_k, w1_k, w2_k)
    vc = compress_blocks(_chunk_rows(y_att, COL_VC), pe_v, w1_v, w2_v)
    kc128 = kc.transpose(1, 0, 2).reshape(kc.shape[1], LANES)
    vct128 = vc.transpose(0, 2, 1).reshape(LANES, vc.shape[1])
    o_c, sel_t, cnt = nsa_compressed(y_att, kc128, vct128, _overlap_t(s))
    o_w = banded_attention(y_att, vt4, COL_VW - COL_VA, COL_QC, COL_KW, NSA_WINDOW, None, F32)
    return nsa_selected(cnt, y_att, vt4, COL_VS - COL_VA, sel_t, y_rest, o_c, o_w)


def mixer_layer(x, norm_mix, w_in, sinks, pe_k, w1_k, w2_k, pe_v, w1_v, w2_v,
                w_br_a, w_br_b, w_br_c, w_out):
    y_att, y_rest = in_projection(x, norm_mix, _prep_w_in(w_in), IN_PROJ_TM, IN_PROJ_TN)
    vt4 = transpose_tiles(y_att, COL_VA, 3, Q_BLOCK)
    y_a = banded_attention(y_att, vt4, 0, COL_QA, COL_KA, SWA_WINDOW, sinks, BF16)
    y_b = stick_breaking(y_att, transpose_tiles(y_att, COL_VB, SB_HEADS, SB_TILE))
    y_c = nsa_attention(y_att, y_rest, vt4, pe_k, w1_k, w2_k, pe_v, w1_v, w2_v)
    return merge_out_projection(y_a, y_b, y_c, _pair_rows(w_br_a).astype(BF16),
                                w_br_b.astype(BF16), _pair_rows(w_br_c).astype(BF16),
                                y_rest, w_out.astype(BF16), x, ROW_TM)


def ffn_layer(x, norm_ffn, w_gate, w_up, w_down, norm_after=None):
    u = ffn_up(x, norm_ffn, w_gate.astype(BF16), w_up.astype(BF16), IN_PROJ_TM, FFN_TN)
    return matmul_residual(u, w_down.astype(BF16), x, ROW_TM, norm_after)


def kernel(x, norm_mix, w_in, swa_sinks, cmp_pe_k, cmp_w1_k, cmp_w2_k, cmp_pe_v, cmp_w1_v,
           cmp_w2_v, w_branch_swa, w_branch_sb, w_branch_nsa, w_out, norm_ffn, w_gate, w_up,
           w_down, norm_final):
    b, s, d = x.shape
    outs = []
    for bi in range(b):
        xb = x[bi]
        depth = norm_mix.shape[0]
        assert depth >= 1
        for layer in range(depth):
            xb = mixer_layer(xb, norm_mix[layer], w_in[layer], swa_sinks[layer],
                             cmp_pe_k[layer], cmp_w1_k[layer], cmp_w2_k[layer],
                             cmp_pe_v[layer], cmp_w1_v[layer], cmp_w2_v[layer],
                             w_branch_swa[layer], w_branch_sb[layer], w_branch_nsa[layer],
                             w_out[layer])
            xb = ffn_layer(xb, norm_ffn[layer], w_gate[layer], w_up[layer], w_down[layer],
                           norm_final if layer == depth - 1 else None)
        outs.append(xb)
    return jnp.stack(outs, axis=0)
```

```python
import functools

import jax
import jax.numpy as jnp
import numpy as np
from jax import lax
from jax.experimental import pallas as pl
from jax.experimental.pallas import tpu as pltpu

F32 = jnp.float32
BF16 = jnp.bfloat16

D_MODEL = 2048
Q_BLOCK = 128
LANES = 128
HALF = 64
N_PAIR = 4
SWA_WINDOW = 128
NSA_WINDOW = 512
CMP_BLOCK = 32
CMP_STRIDE = 16
CMP_HIDDEN = 256
CMP_CHUNK = 256
SEL_BLOCK = 64
SEL_TOPK = 8
SB_HEADS = 4
SB_TILE = 256
D_FF = 5632
NEG = -1e30
SEL_FORCE = 1e9
SEL_TAKEN = -3e38
SUM_ROWS = 16
RMS_EPS = 1e-6
LOG2E = 1.4426950408889634
SB_DEAD_BITS = 150.0
VMEM_LIMIT = 56 * 1024 * 1024

N_ATT = 3584
N_GC = 24
N_GATES = 3 * D_MODEL
IN_PROJ_TM = 1024
IN_PROJ_TN = 512
FFN_TN = 512
ROW_TM = 256
COL_QA, COL_QC, COL_QB, COL_KB, COL_VB = 0, 4, 8, 12, 16
COL_KA, COL_KC, COL_VC, COL_KS, COL_KW = 20, 21, 22, 23, 24
COL_VA, COL_VS, COL_VW = 25, 26, 27
TRANSPOSE_ROWS = 2048


def _nt(a, b):
    return lax.dot_general(a, b, (((1,), (1,)), ((), ())), preferred_element_type=F32)


def _dot(a, b):
    return jnp.dot(a, b, preferred_element_type=F32)


def _params(*sem):
    return pltpu.CompilerParams(dimension_semantics=sem, vmem_limit_bytes=VMEM_LIMIT)


def _head_slope(pair, group):
    return 2.0 ** -(group * N_PAIR + pair + 1)


def _in_proj_kernel(x_ref, g_ref, w_ref, oa_ref, or_ref, h_ref, *, att_tiles):
    j = pl.program_id(1)

    @pl.when(j == 0)
    def _():
        x = x_ref[...]
        ms = jnp.mean(x * x, axis=-1, keepdims=True)
        h_ref[...] = (x * lax.rsqrt(ms + RMS_EPS) * g_ref[...]).astype(BF16)

    @pl.when(j < att_tiles)
    def _():
        oa_ref[...] = _dot(h_ref[...], w_ref[...]).astype(oa_ref.dtype)

    @pl.when(j >= att_tiles)
    def _():
        or_ref[...] = _dot(h_ref[...], w_ref[...])


def in_projection(x, g, w_all, tm, tn):
    s, d = x.shape
    n_tiles = w_all.shape[1] // tn
    att_tiles = N_ATT // tn
    rest_tiles = n_tiles - att_tiles
    kern = functools.partial(_in_proj_kernel, att_tiles=att_tiles)
    return pl.pallas_call(
        kern,
        grid=(s // tm, n_tiles),
        in_specs=[pl.BlockSpec((tm, d), lambda i, j: (i, 0)),
                  pl.BlockSpec((1, d), lambda i, j: (0, 0)),
                  pl.BlockSpec((d, tn), lambda i, j: (0, j))],
        out_specs=[pl.BlockSpec((tm, tn), lambda i, j: (i, jnp.minimum(j, att_tiles - 1))),
                   pl.BlockSpec((tm, tn), lambda i, j: (i, jnp.maximum(j - att_tiles, 0)))],
        out_shape=[jax.ShapeDtypeStruct((s, N_ATT), BF16),
                   jax.ShapeDtypeStruct((s, rest_tiles * tn), F32)],
        scratch_shapes=[pltpu.VMEM((tm, d), BF16)],
        compiler_params=_params("parallel", "arbitrary"),
        name="in_projection",
    )(x, g.reshape(1, d), w_all)


def _mm_res_kernel(a_ref, w_ref, r_ref, *rest, final_norm):
    y = r_ref[...] + _dot(a_ref[...], w_ref[...])
    if final_norm:
        g_ref, o_ref = rest
        ms = jnp.mean(y * y, axis=-1, keepdims=True)
        y = y * lax.rsqrt(ms + RMS_EPS) * g_ref[...]
    else:
        o_ref, = rest
    o_ref[...] = y


def matmul_residual(a, w, res, tm, norm_g=None):
    s, k = a.shape
    n = w.shape[1]
    in_specs = [pl.BlockSpec((tm, k), lambda i: (i, 0)),
                pl.BlockSpec((k, n), lambda i: (0, 0), pipeline_mode=pl.Buffered(1)),
                pl.BlockSpec((tm, n), lambda i: (i, 0))]
    args = [a, w, res]
    if norm_g is not None:
        in_specs.append(pl.BlockSpec((1, n), lambda i: (0, 0)))
        args.append(norm_g.reshape(1, n))
    return pl.pallas_call(
        functools.partial(_mm_res_kernel, final_norm=norm_g is not None),
        grid=(s // tm,),
        in_specs=in_specs,
        out_specs=pl.BlockSpec((tm, n), lambda i: (i, 0)),
        out_shape=jax.ShapeDtypeStruct((s, n), F32),
        compiler_params=_params("parallel"),
        name="matmul_residual",
    )(*args)


def _merge_out_kernel(ya_ref, yb_ref, yc_ref, wa_ref, wb_ref, wc_ref,
                      ga_ref, gb_ref, gc_ref, wo_ref, x_ref, o_ref):
    m = jax.nn.sigmoid(ga_ref[...]) * _dot(ya_ref[...], wa_ref[...])
    m = m + jax.nn.sigmoid(gb_ref[...]) * _dot(yb_ref[...], wb_ref[...])
    m = m + jax.nn.sigmoid(gc_ref[...]) * _dot(yc_ref[...], wc_ref[...])
    o_ref[...] = x_ref[...] + _dot(m.astype(BF16), wo_ref[...])


def merge_out_projection(ya, yb, yc, wa, wb, wc, y_rest, w_out, x, tm):
    s, k = ya.shape
    n = wa.shape[1]
    resident = pl.Buffered(1)
    y_spec = pl.BlockSpec((tm, k), lambda i: (i, 0))
    w_spec = pl.BlockSpec((k, n), lambda i: (0, 0), pipeline_mode=resident)

    def gate_spec(br):
        return pl.BlockSpec((tm, n), lambda i: (i, br))

    return pl.pallas_call(
        _merge_out_kernel,
        grid=(s // tm,),
        in_specs=[y_spec, y_spec, y_spec, w_spec, w_spec, w_spec,
                  gate_spec(0), gate_spec(1), gate_spec(2),
                  pl.BlockSpec((n, n), lambda i: (0, 0), pipeline_mode=resident),
                  pl.BlockSpec((tm, n), lambda i: (i, 0))],
        out_specs=pl.BlockSpec((tm, n), lambda i: (i, 0)),
        out_shape=jax.ShapeDtypeStruct((s, n), F32),
        compiler_params=_params("parallel"),
        name="merge_out_projection",
    )(ya, yb, yc, wa, wb, wc, y_rest, y_rest, y_rest, w_out, x)


def _ffn_up_kernel(x_ref, g_ref, wg_ref, wu_ref, o_ref, h_ref):
    @pl.when(pl.program_id(1) == 0)
    def _():
        x = x_ref[...]
        ms = jnp.mean(x * x, axis=-1, keepdims=True)
        h_ref[...] = (x * lax.rsqrt(ms + RMS_EPS) * g_ref[...]).astype(BF16)

    h = h_ref[...]
    o_ref[...] = (jax.nn.silu(_dot(h, wg_ref[...])) * _dot(h, wu_ref[...])).astype(o_ref.dtype)


def ffn_up(x, g, wg, wu, tm, tn):
    s, d = x.shape
    n = wg.shape[1]
    w_spec = pl.BlockSpec((d, tn), lambda i, j: (0, j))
    return pl.pallas_call(
        _ffn_up_kernel,
        grid=(s // tm, n // tn),
        in_specs=[pl.BlockSpec((tm, d), lambda i, j: (i, 0)),
                  pl.BlockSpec((1, d), lambda i, j: (0, 0)),
                  w_spec, w_spec],
        out_specs=pl.BlockSpec((tm, tn), lambda i, j: (i, j)),
        out_shape=jax.ShapeDtypeStruct((s, n), BF16),
        scratch_shapes=[pltpu.VMEM((tm, d), BF16)],
        compiler_params=_params("parallel", "arbitrary"),
        name="ffn_up",
    )(x, g.reshape(1, d), wg, wu)


def _half_masks():
    lane = lax.broadcasted_iota(jnp.int32, (Q_BLOCK, LANES), 1)
    return lane < HALF


def _scaled_q(q):
    return q * jnp.asarray(HALF ** -0.5, q.dtype)


def _banded_kernel(*refs, n_prev, window, has_sink):
    nk = n_prev + 1
    if has_sink:
        sink_ref, refs = refs[0], refs[1:]
    q_ref = refs[0]
    k_refs = refs[1:1 + nk]
    v_refs = refs[1 + nk:1 + 2 * nk]
    o_ref = refs[1 + 2 * nk]
    i = pl.program_id(0)
    span = nk * Q_BLOCK
    k_all = jnp.concatenate([k_refs[d][...] for d in range(n_prev, -1, -1)], axis=0)
    vt_all = jnp.concatenate([v_refs[d][...] for d in range(n_prev, -1, -1)], axis=1)
    key_rel = lax.broadcasted_iota(jnp.int32, (span, Q_BLOCK), 0)
    q_rel = lax.broadcasted_iota(jnp.int32, (span, Q_BLOCK), 1)
    dist = q_rel + n_prev * Q_BLOCK - key_rel
    key_pos = (i - n_prev) * Q_BLOCK + key_rel
    mask = (dist >= 0) & (dist < window) & (key_pos >= 0)
    bias = jnp.where(mask, 0.0, NEG)
    distf = dist.astype(F32)
    lo = _half_masks()
    top = lax.broadcasted_iota(jnp.int32, (Q_BLOCK, Q_BLOCK), 0) < HALF
    for pair in range(N_PAIR):
        q128 = _scaled_q(q_ref[:, pair * LANES:(pair + 1) * LANES])
        halves = []
        for group in range(2):
            qm = jnp.where(lo if group == 0 else jnp.logical_not(lo), q128, jnp.zeros_like(q128))
            s = _nt(k_all, qm) - _head_slope(pair, group) * distf + bias
            m = jnp.max(s, axis=0, keepdims=True)
            if has_sink:
                sink = sink_ref[group * N_PAIR + pair]
                m = jnp.maximum(m, sink)
                p = jnp.exp(s - m)
                denom = jnp.sum(p, axis=0, keepdims=True) + jnp.exp(sink - m)
            else:
                p = jnp.exp(s - m)
                denom = jnp.maximum(jnp.sum(p, axis=0, keepdims=True), 1e-30)
            pv = _dot(vt_all, p.astype(BF16))
            halves.append(pv * (1.0 / denom))
        ot = jnp.where(top, halves[0], halves[1])
        o_ref[:, pair * LANES:(pair + 1) * LANES] = ot.T.astype(o_ref.dtype)


def banded_attention(y_att, vt4, vt_index, col_q, col_k, window, sinks, out_dtype):
    s = y_att.shape[0]
    nb = s // Q_BLOCK
    n_prev = -(-window // Q_BLOCK)
    has_sink = sinks is not None
    in_specs = []
    args = []
    if has_sink:
        in_specs.append(pl.BlockSpec(memory_space=pltpu.SMEM))
        args.append(sinks)
    in_specs.append(pl.BlockSpec((Q_BLOCK, N_PAIR * LANES), lambda i: (i, col_q // N_PAIR)))
    args.append(y_att)
    for d in range(n_prev + 1):
        in_specs.append(pl.BlockSpec((Q_BLOCK, LANES),
                                     lambda i, d=d: (jnp.maximum(i - d, 0), col_k)))
        args.append(y_att)
    for d in range(n_prev + 1):
        in_specs.append(pl.BlockSpec((None, None, LANES, Q_BLOCK),
                                     lambda i, d=d: (vt_index, jnp.maximum(i - d, 0), 0, 0)))
        args.append(vt4)
    kern = functools.partial(_banded_kernel, n_prev=n_prev, window=window, has_sink=has_sink)
    return pl.pallas_call(
        kern,
        grid=(nb,),
        in_specs=in_specs,
        out_specs=pl.BlockSpec((Q_BLOCK, N_PAIR * LANES), lambda i: (i, 0)),
        out_shape=jax.ShapeDtypeStruct((s, N_PAIR * LANES), out_dtype),
        compiler_params=_params("parallel"),
        name="banded_attention_w%d" % window,
    )(*args)


def _sb_kernel(q_ref, k_ref, vt_ref, o_ref, ls_ref, acc_ref, *, scale, heads):
    i = pl.program_id(1)
    row = lax.broadcasted_iota(jnp.int32, (SB_TILE, SB_TILE), 0)
    col = lax.broadcasted_iota(jnp.int32, (SB_TILE, SB_TILE), 1)
    later = jnp.where(col > row, 1.0, 0.0).astype(BF16)
    before = row < col

    def visit(kj, diagonal):
        gone = [ls_ref[h] for h in range(heads)]
        pv, gone_new = [], []
        for h in range(heads):
            q = q_ref[:, h * LANES:(h + 1) * LANES]
            k = k_ref[pl.ds(pl.multiple_of(kj * SB_TILE, SB_TILE), SB_TILE),
                      h * LANES:(h + 1) * LANES]
            zz = _nt(k, q) * (scale * LOG2E)
            sp_raw = jnp.maximum(zz, 0.0) + jnp.log(1.0 + jnp.exp2(-jnp.abs(zz))) * LOG2E
            sp = jnp.where(before, sp_raw, 0.0) if diagonal else sp_raw
            a = jnp.exp2((zz - sp_raw) - _dot(later, sp.astype(BF16)) - gone[h])
            if diagonal:
                a = jnp.where(before, a, 0.0)
            pv.append(_dot(vt_ref[h, kj], a.astype(BF16)))
            gone_new.append(gone[h] + jnp.sum(sp, axis=0, keepdims=True))
        alive = gone_new[0]
        for h in range(heads):
            acc_ref[h] += pv[h]
            ls_ref[h] = gone_new[h]
            alive = jnp.minimum(alive, gone_new[h])
        return jnp.min(alive)

    ls_ref[...] = jnp.zeros(ls_ref.shape, F32)
    acc_ref[...] = jnp.zeros(acc_ref.shape, F32)
    alive0 = visit(i, True)

    def cond(carry):
        return (carry[0] < i) & (carry[1] < SB_DEAD_BITS)

    def body(carry):
        return carry[0] + 1, visit(i - 1 - carry[0], False)

    lax.while_loop(cond, body, (jnp.int32(0), alive0))
    for h in range(heads):
        o_ref[:, h * LANES:(h + 1) * LANES] = acc_ref[h].T.astype(o_ref.dtype)


def stick_breaking(y_att, vt4, heads=SB_HEADS):
    s = y_att.shape[0]
    nt = s // SB_TILE
    kern = functools.partial(_sb_kernel, scale=LANES ** -0.5, heads=heads)
    resident = pl.Buffered(1)
    return pl.pallas_call(
        kern,
        grid=(SB_HEADS // heads, nt),
        in_specs=[pl.BlockSpec((SB_TILE, heads * LANES), lambda g, i: (i, COL_QB // heads + g)),
                  pl.BlockSpec((s, heads * LANES), lambda g, i: (0, COL_KB // heads + g),
                               pipeline_mode=resident),
                  pl.BlockSpec((heads, nt, LANES, SB_TILE), lambda g, i: (g, 0, 0, 0),
                               pipeline_mode=resident)],
        out_specs=pl.BlockSpec((SB_TILE, heads * LANES), lambda g, i: (i, g)),
        out_shape=jax.ShapeDtypeStruct((s, SB_HEADS * LANES), BF16),
        scratch_shapes=[pltpu.VMEM((heads, 1, SB_TILE), F32),
                        pltpu.VMEM((heads, LANES, SB_TILE), F32)],
        compiler_params=_params("parallel", "arbitrary"),
        name="stick_breaking",
    )(y_att, y_att, vt4)


def _compress_kernel(t_ref, pe_ref, w1_ref, w2_ref, o_ref):
    t = t_ref[...]
    half = w1_ref.shape[0] // 2
    w1 = w1_ref[...]
    a = _dot(t, w1[:half])
    b = _dot(t, w1[half:])
    bias = _dot(pe_ref[...], w1)[0:1]
    n = t.shape[0]
    pre = a + pltpu.roll(b, n - 1, 0) + bias
    hid = jax.nn.gelu(pre)
    o_ref[...] = _dot(hid.astype(BF16), w2_ref[...]).astype(o_ref.dtype)


def compress_blocks(t_flat, pe, w1, w2):
    g, n, k = t_flat.shape
    pe_rows = jnp.zeros((8, 2 * k), BF16).at[0].set(pe.reshape(-1).astype(BF16))
    return pl.pallas_call(
        _compress_kernel,
        grid=(g,),
        in_specs=[pl.BlockSpec((None, n, k), lambda gi: (gi, 0, 0)),
                  pl.BlockSpec((8, 2 * k), lambda gi: (0, 0)),
                  pl.BlockSpec((2 * k, CMP_HIDDEN), lambda gi: (0, 0)),
                  pl.BlockSpec((CMP_HIDDEN, HALF), lambda gi: (0, 0))],
        out_specs=pl.BlockSpec((None, n, HALF), lambda gi: (gi, 0, 0)),
        out_shape=jax.ShapeDtypeStruct((g, n, HALF), BF16),
        compiler_params=_params("parallel"),
        name="compress_blocks",
    )(t_flat, pe_rows, w1.astype(BF16), w2.astype(BF16))


def _nsa_cmp_kernel(q_ref, kc_ref, vct_ref, ov_ref, oc_ref, sel_ref, cnt_ref):
    i = pl.program_id(0)
    n_all = kc_ref.shape[0]
    chunk = min(CMP_CHUNK, n_all)
    needed = lax.div(8 * i + 7 + (chunk - 1), chunk)
    body = functools.partial(_nsa_cmp_body, q_ref, kc_ref, vct_ref, ov_ref, oc_ref, sel_ref, cnt_ref)
    for c in range(1, n_all // chunk + 1):
        pl.when(needed == c)(functools.partial(body, n_cmp=c * chunk))


def _nsa_cmp_body(q_ref, kc_ref, vct_ref, ov_ref, oc_ref, sel_ref, cnt_ref, *, n_cmp):
    i = pl.program_id(0)
    kc = kc_ref[:n_cmp, :]
    vct = vct_ref[:, :n_cmp]
    t = i * Q_BLOCK + lax.broadcasted_iota(jnp.int32, (n_cmp, Q_BLOCK), 1)
    cmp_end = lax.broadcasted_iota(jnp.int32, (n_cmp, Q_BLOCK), 0) * CMP_STRIDE + (CMP_BLOCK - 1)
    dist = t - cmp_end
    bias = jnp.where(dist >= 0, 0.0, NEG)
    seen = jnp.where(t[0:1] >= CMP_BLOCK - 1, 1.0, 0.0)
    distf = dist.astype(F32)
    lo = _half_masks()
    top = lax.broadcasted_iota(jnp.int32, (Q_BLOCK, Q_BLOCK), 0) < HALF
    psum = [jnp.zeros((n_cmp, Q_BLOCK), F32), jnp.zeros((n_cmp, Q_BLOCK), F32)]
    for pair in range(N_PAIR):
        q128 = _scaled_q(q_ref[:, pair * LANES:(pair + 1) * LANES])
        halves = []
        for group in range(2):
            qm = jnp.where(lo if group == 0 else jnp.logical_not(lo), q128, jnp.zeros_like(q128))
            s = _nt(kc, qm) - _head_slope(pair, group) * distf + bias
            m = jnp.max(s, axis=0, keepdims=True)
            p = jnp.exp(s - m)
            p = p * (seen / jnp.maximum(jnp.sum(p, axis=0, keepdims=True) * seen, 1e-30))
            halves.append(_dot(vct, p.astype(BF16)))
            psum[group] = psum[group] + p
        ot = jnp.where(top, halves[0], halves[1])
        oc_ref[:, pair * LANES:(pair + 1) * LANES] = ot.T

    n_sel_all = ov_ref.shape[0]
    n_sel = min(n_sel_all, n_cmp // (SEL_BLOCK // CMP_STRIDE))
    def all_rows(x):
        if n_sel == n_sel_all:
            return x
        return jnp.concatenate([x, jnp.zeros((n_sel_all - n_sel, Q_BLOCK), F32)], axis=0)

    ov = ov_ref[:n_sel, :n_cmp]
    tq = i * Q_BLOCK + lax.broadcasted_iota(jnp.int32, (n_sel, Q_BLOCK), 1)
    jblk = lax.broadcasted_iota(jnp.int32, (n_sel, Q_BLOCK), 0)
    jblk_f = jblk.astype(F32)
    jt = tq // SEL_BLOCK
    valid = jblk * SEL_BLOCK <= tq
    forced = (jblk == 0) | (jblk == jt) | (jblk == jt - 1)
    total = jnp.zeros((n_sel, Q_BLOCK), F32)
    for group in range(2):
        hi = psum[group].astype(BF16)
        lo_part = (psum[group] - hi.astype(F32)).astype(BF16)
        imp = _dot(ov, hi) + _dot(ov, lo_part)
        score = jnp.where(valid, jnp.where(forced, SEL_FORCE, imp), -SEL_FORCE)
        sel = jnp.zeros((n_sel, Q_BLOCK), F32)
        for _ in range(min(SEL_TOPK, n_sel)):
            best = jnp.max(score, axis=0, keepdims=True)
            first = jnp.min(jnp.where(score == best, jblk_f, float(n_sel)), axis=0, keepdims=True)
            hit = jblk_f == first
            sel = jnp.where(hit, 1.0, sel)
            score = jnp.where(hit, SEL_TAKEN, score)
        sel_ref[0, group] = all_rows(sel)
        total = total + sel
    cnt_ref[0] = _nt(jnp.ones((8, Q_BLOCK), BF16), all_rows(total).astype(BF16))


def nsa_compressed(y_att, kc128, vct128, overlap_t):
    s = y_att.shape[0]
    nb = s // Q_BLOCK
    n_cmp = kc128.shape[0]
    n_sel = overlap_t.shape[0]
    return pl.pallas_call(
        _nsa_cmp_kernel,
        grid=(nb,),
        in_specs=[pl.BlockSpec((Q_BLOCK, N_PAIR * LANES), lambda i: (i, COL_QC // N_PAIR)),
                  pl.BlockSpec((n_cmp, LANES), lambda i: (0, 0)),
                  pl.BlockSpec((LANES, n_cmp), lambda i: (0, 0)),
                  pl.BlockSpec((n_sel, n_cmp), lambda i: (0, 0))],
        out_specs=[pl.BlockSpec((Q_BLOCK, N_PAIR * LANES), lambda i: (i, 0)),
                   pl.BlockSpec((1, 2, n_sel, Q_BLOCK), lambda i: (i, 0, 0, 0)),
                   pl.BlockSpec((1, 8, n_sel), lambda i: (i, 0, 0))],
        out_shape=[jax.ShapeDtypeStruct((s, N_PAIR * LANES), F32),
                   jax.ShapeDtypeStruct((nb, 2, n_sel, Q_BLOCK), F32),
                   jax.ShapeDtypeStruct((nb, 8, n_sel), F32)],
        compiler_params=_params("parallel"),
        name="nsa_compressed",
    )(y_att, kc128, vct128, overlap_t)


def _nsa_sel_kernel(todo_ref, count_ref, q_ref, ks_ref, vst_ref, sel_ref, slope_ref, gc_ref, oc_ref,
                    ow_ref, o_ref, qm_ref, m_ref, l_ref, acc_ref, *, n_tiles):
    i = pl.program_id(0)
    n_heads = 2 * N_PAIR
    n_active = count_ref[i]
    lo = _half_masks()
    for pair in range(N_PAIR):
        q128 = _scaled_q(q_ref[:, pair * LANES:(pair + 1) * LANES])
        for group in range(2):
            h = pair * 2 + group
            qm_ref[h * Q_BLOCK:(h + 1) * Q_BLOCK, :] = jnp.where(
                lo if group == 0 else jnp.logical_not(lo), q128, jnp.zeros_like(q128))
    m_ref[...] = jnp.full(m_ref.shape, 0.1 * NEG, F32)
    l_ref[...] = jnp.zeros(l_ref.shape, F32)
    acc_ref[...] = jnp.zeros(acc_ref.shape, F32)
    top = lax.broadcasted_iota(jnp.int32, (Q_BLOCK, Q_BLOCK), 0) < HALF
    slopes = slope_ref[...]

    def tile_terms(p, live):
        key_pos = p * Q_BLOCK + lax.broadcasted_iota(jnp.int32, (Q_BLOCK, Q_BLOCK), 0)
        t = i * Q_BLOCK + lax.broadcasted_iota(jnp.int32, (Q_BLOCK, Q_BLOCK), 1)
        dist = t - key_pos
        bias = []
        for group in range(2):
            r0 = sel_ref[0, group, pl.ds(2 * p, 1), :]
            r1 = sel_ref[0, group, pl.ds(2 * p + 1, 1), :]
            picked = jnp.where(top, r0, r1)
            masked = jnp.where((picked > 0.5) & (dist >= 0), 0.0, NEG)
            bias.append(masked if live is True else jnp.where(live, masked, NEG))
        k = ks_ref[pl.ds(pl.multiple_of(p * Q_BLOCK, Q_BLOCK), Q_BLOCK), :]
        return k, vst_ref[p], dist.astype(F32), bias

    def body(step, carry):
        first = 2 * step
        has_second = first + 1 < n_active
        p0 = todo_ref[i * n_tiles + first]
        p1 = todo_ref[i * n_tiles + jnp.where(has_second, first + 1, first)]
        k0, vt0, dist0, bias0 = tile_terms(p0, True)
        k1, vt1, dist1, bias1 = tile_terms(p1, has_second)
        k = jnp.concatenate([k0, k1], axis=0)
        vt = jnp.concatenate([vt0, vt1], axis=1)
        bias_all = jnp.concatenate(
            [jnp.concatenate([bias0[h % 2], bias1[h % 2]], axis=0) for h in range(n_heads)], axis=1)
        dist_all = jnp.concatenate([jnp.concatenate([dist0, dist1], axis=0)] * n_heads, axis=1)
        s = _nt(k, qm_ref[...]) - slopes * dist_all + bias_all
        m_old = m_ref[...]
        m_new = jnp.maximum(m_old, jnp.max(s, axis=0, keepdims=True))
        alpha = jnp.exp(m_old - m_new)
        pr = jnp.exp((s - m_new).astype(BF16))
        vt_ext = jnp.concatenate([vt, jnp.ones((SUM_ROWS, 2 * Q_BLOCK), BF16)], axis=0)
        pv = _dot(vt_ext, pr)
        l_ref[...] = alpha * l_ref[...] + pv[LANES:LANES + 1]
        own = jnp.concatenate(
            [pv[(h % 2) * HALF:(h % 2 + 1) * HALF, h * Q_BLOCK:(h + 1) * Q_BLOCK]
             for h in range(n_heads)], axis=1)
        acc_ref[...] = alpha * acc_ref[...] + own
        m_ref[...] = m_new
        return carry

    lax.fori_loop(0, lax.div(n_active + 1, 2), body, 0)

    gates = jax.nn.sigmoid(gc_ref[...])
    o_all = acc_ref[...] * (1.0 / jnp.maximum(l_ref[...], 1e-30))
    for pair in range(N_PAIR):
        h = pair * 2
        o_s = jnp.concatenate([o_all[:, h * Q_BLOCK:(h + 1) * Q_BLOCK],
                               o_all[:, (h + 1) * Q_BLOCK:(h + 2) * Q_BLOCK]], axis=0).T
        cols = slice(pair * LANES, (pair + 1) * LANES)

        def gate(branch):
            c = branch * 8 + pair * 2
            return jnp.where(lo, gates[:, c:c + 1], gates[:, c + 1:c + 2])

        y = gate(0) * oc_ref[:, cols] + gate(1) * o_s + gate(2) * ow_ref[:, cols]
        o_ref[:, cols] = y.astype(o_ref.dtype)


def _slope_row():
    row = np.concatenate([np.full((Q_BLOCK,), _head_slope(h // 2, h % 2), np.float32)
                          for h in range(2 * N_PAIR)])
    return jnp.asarray(row.reshape(1, -1))


def _visit_lists(cnt):
    nb = cnt.shape[0]
    tile = jnp.arange(nb, dtype=jnp.int32)
    picked = (cnt[:, 0, :].reshape(nb, nb, 2).sum(-1) > 0.5) & (tile[None, :] <= tile[:, None])
    slot = jnp.cumsum(picked.astype(jnp.int32), axis=1) - 1
    hit = picked[:, :, None] & (slot[:, :, None] == tile[None, None, :])
    todo = jnp.sum(jnp.where(hit, tile[None, :, None], 0), axis=1)
    return todo.reshape(-1), picked.sum(axis=1).astype(jnp.int32)


def nsa_selected(cnt, y_att, vt4, vt_index, sel_t, y_rest, o_c, o_w):
    s = y_att.shape[0]
    nb = s // Q_BLOCK
    n_sel = sel_t.shape[2]
    n_heads = 2 * N_PAIR
    todo, count = _visit_lists(cnt)
    kern = functools.partial(_nsa_sel_kernel, n_tiles=nb)
    wide = pl.BlockSpec((Q_BLOCK, N_PAIR * LANES), lambda i, *_: (i, 0))
    grid_spec = pltpu.PrefetchScalarGridSpec(
        num_scalar_prefetch=2,
        grid=(nb,),
        in_specs=[pl.BlockSpec((Q_BLOCK, N_PAIR * LANES), lambda i, *_: (i, COL_QC // N_PAIR)),
                  pl.BlockSpec((s, LANES), lambda i, *_: (0, COL_KS)),
                  pl.BlockSpec((None, nb, LANES, Q_BLOCK), lambda i, *_: (vt_index, 0, 0, 0)),
                  pl.BlockSpec((1, 2, n_sel, Q_BLOCK), lambda i, *_: (i, 0, 0, 0)),
                  pl.BlockSpec((1, n_heads * Q_BLOCK), lambda i, *_: (0, 0)),
                  pl.BlockSpec((Q_BLOCK, LANES), lambda i, *_: (i, N_GATES // LANES)),
                  wide, wide],
        out_specs=wide,
        scratch_shapes=[pltpu.VMEM((n_heads * Q_BLOCK, LANES), BF16),
                        pltpu.VMEM((1, n_heads * Q_BLOCK), F32),
                        pltpu.VMEM((1, n_heads * Q_BLOCK), F32),
                        pltpu.VMEM((HALF, n_heads * Q_BLOCK), F32)],
    )
    return pl.pallas_call(
        kern,
        grid_spec=grid_spec,
        out_shape=jax.ShapeDtypeStruct((s, N_PAIR * LANES), BF16),
        compiler_params=_params("arbitrary"),
        name="nsa_selected",
    )(todo, count, y_att, y_att, vt4, sel_t, _slope_row(), y_rest, o_c, o_w)


def _pair_cols(w):
    lead = w.shape[:-1]
    return w.reshape(*lead, 2, N_PAIR, HALF).swapaxes(-3, -2).reshape(*lead, 2 * N_PAIR * HALF)


def _pair_rows(w):
    return w.reshape(2, N_PAIR, HALF, w.shape[-1]).swapaxes(0, 1).reshape(2 * N_PAIR * HALF, w.shape[-1])


def _prep_w_in(w_in):
    qa = _pair_cols(w_in[:, 0:512])
    qc = _pair_cols(w_in[:, 2304:2816])
    cols = lambda a, b: w_in[:, a:b]
    w_att = jnp.concatenate(
        [qa, qc, cols(768, 2304),
         cols(512, 640), cols(2816, 3072), cols(3072, 3200), cols(3328, 3456),
         cols(640, 768), cols(3200, 3328), cols(3456, 3584)],
        axis=1).astype(BF16)
    gc = w_in[:, N_ATT:N_ATT + N_GC].reshape(-1, 2, N_PAIR, 3)
    gc = gc.transpose(0, 3, 2, 1).reshape(-1, N_GC)
    gc = jnp.pad(gc, ((0, 0), (0, IN_PROJ_TN - N_GC)))
    return jnp.concatenate([w_att, w_in[:, N_ATT + N_GC:].astype(BF16), gc.astype(BF16)], axis=1)


def _transpose_kernel(x_ref, o_ref):
    tile = o_ref.shape[-1]
    for c in range(o_ref.shape[0]):
        o_ref[c] = x_ref[c * tile:(c + 1) * tile, :].T


def transpose_tiles(y_att, col, width, tile):
    s = y_att.shape[0]
    rows = min(TRANSPOSE_ROWS, s)
    return pl.pallas_call(
        _transpose_kernel,
        grid=(width, s // rows),
        in_specs=[pl.BlockSpec((rows, LANES), lambda c, r: (r, col + c))],
        out_specs=pl.BlockSpec((None, rows // tile, LANES, tile), lambda c, r: (c, r, 0, 0)),
        out_shape=jax.ShapeDtypeStruct((width, s // tile, LANES, tile), y_att.dtype),
        compiler_params=_params("parallel", "parallel"),
        name="transpose_tiles",
    )(y_att)


def _chunk_rows(y_att, col):
    s = y_att.shape[0]
    t = y_att[:, col * LANES:(col + 1) * LANES].reshape(s // CMP_STRIDE, CMP_STRIDE, 2, HALF)
    return t.transpose(2, 0, 1, 3).reshape(2, s // CMP_STRIDE, CMP_STRIDE * HALF)


def _overlap_t(s):
    n_cmp = s // CMP_STRIDE
    n_sel = s // SEL_BLOCK
    cmp_lo = np.arange(n_cmp) * CMP_STRIDE
    cmp_end = cmp_lo + CMP_BLOCK - 1
    sel_lo = np.arange(n_sel) * SEL_BLOCK
    ov = ((cmp_lo[None, :] <= sel_lo[:, None] + SEL_BLOCK - 1)
          & (cmp_end[None, :] >= sel_lo[:, None]) & (np.arange(n_cmp)[None, :] < n_cmp - 1))
    return jnp.asarray(ov.astype(np.float32), dtype=BF16)


def nsa_attention(y_att, y_rest, vt4, pe_k, w1_k, w2_k, pe_v, w1_v, w2_v):
    s = y_att.shape[0]
    kc = compress_blocks(_chunk_rows(y_att, COL_KC), pe_k, w1_k, w2_k)
    vc = compress_blocks(_chunk_rows(y_att, COL_VC), pe_v, w1_v, w2_v)
    kc128 = kc.transpose(1, 0, 2).reshape(kc.shape[1], LANES)
    vct128 = vc.transpose(0, 2, 1).reshape(LANES, vc.shape[1])
    o_c, sel_t, cnt = nsa_compressed(y_att, kc128, vct128, _overlap_t(s))
    o_w = banded_attention(y_att, vt4, COL_VW - COL_VA, COL_QC, COL_KW, NSA_WINDOW, None, F32)
    return nsa_selected(cnt, y_att, vt4, COL_VS - COL_VA, sel_t, y_rest, o_c, o_w)


def mixer_layer(x, norm_mix, w_in, sinks, pe_k, w1_k, w2_k, pe_v, w1_v, w2_v,
                w_br_a, w_br_b, w_br_c, w_out):
    y_att, y_rest = in_projection(x, norm_mix, _prep_w_in(w_in), IN_PROJ_TM, IN_PROJ_TN)
    vt4 = transpose_tiles(y_att, COL_VA, 3, Q_BLOCK)
    y_a = banded_attention(y_att, vt4, 0, COL_QA, COL_KA, SWA_WINDOW, sinks, BF16)
    y_b = stick_breaking(y_att, transpose_tiles(y_att, COL_VB, SB_HEADS, SB_TILE))
    y_c = nsa_attention(y_att, y_rest, vt4, pe_k, w1_k, w2_k, pe_v, w1_v, w2_v)
    return merge_out_projection(y_a, y_b, y_c, _pair_rows(w_br_a).astype(BF16),
                                w_br_b.astype(BF16), _pair_rows(w_br_c).astype(BF16),
                                y_rest, w_out.astype(BF16), x, ROW_TM)


def ffn_layer(x, norm_ffn, w_gate, w_up, w_down, norm_after=None):
    u = ffn_up(x, norm_ffn, w_gate.astype(BF16), w_up.astype(BF16), IN_PROJ_TM, FFN_TN)
    return matmul_residual(u, w_down.astype(BF16), x, ROW_TM, norm_after)


def kernel(x, norm_mix, w_in, swa_sinks, cmp_pe_k, cmp_w1_k, cmp_w2_k, cmp_pe_v, cmp_w1_v,
           cmp_w2_v, w_branch_swa, w_branch_sb, w_branch_nsa, w_out, norm_ffn, w_gate, w_up,
           w_down, norm_final):
    b, s, d = x.shape
    outs = []
    for bi in range(b):
        xb = x[bi]
        depth = norm_mix.shape[0]
        assert depth >= 1
        for layer in range(depth):
            xb = mixer_layer(xb, norm_mix[layer], w_in[layer], swa_sinks[layer],
                             cmp_pe_k[layer], cmp_w1_k[layer], cmp_w2_k[layer],
                             cmp_pe_v[layer], cmp_w1_v[layer], cmp_w2_v[layer],
                             w_branch_swa[layer], w_branch_sb[layer], w_branch_nsa[layer],
                             w_out[layer])
            xb = ffn_layer(xb, norm_ffn[layer], w_gate[layer], w_up[layer], w_down[layer],
                           norm_final if layer == depth - 1 else None)
        outs.append(xb)
    return jnp.stack(outs, axis=0)
```

```python
import functools

import jax
import jax.numpy as jnp
import numpy as np
from jax import lax
from jax.experimental import pallas as pl
from jax.experimental.pallas import tpu as pltpu

F32 = jnp.float32
BF16 = jnp.bfloat16

D_MODEL = 2048
Q_BLOCK = 128
LANES = 128
HALF = 64
N_PAIR = 4
SWA_WINDOW = 128
NSA_WINDOW = 512
CMP_BLOCK = 32
CMP_STRIDE = 16
CMP_HIDDEN = 256
CMP_CHUNK = 256
SEL_BLOCK = 64
SEL_TOPK = 8
SB_HEADS = 4
SB_TILE = 256
D_FF = 5632
NEG = -1e30
SEL_FORCE = 1e9
SEL_TAKEN = -3e38
SUM_ROWS = 16
RMS_EPS = 1e-6
LOG2E = 1.4426950408889634
SB_DEAD_BITS = 150.0
VMEM_LIMIT = 56 * 1024 * 1024

N_ATT = 3584
N_GC = 24
N_GATES = 3 * D_MODEL
IN_PROJ_TM = 1024
IN_PROJ_TN = 512
FFN_TN = 512
ROW_TM = 256
COL_QA, COL_QC, COL_QB, COL_KB, COL_VB = 0, 4, 8, 12, 16
COL_KA, COL_KC, COL_VC, COL_KS, COL_KW = 20, 21, 22, 23, 24
COL_VA, COL_VS, COL_VW = 25, 26, 27
TRANSPOSE_ROWS = 2048


def _nt(a, b):
    return lax.dot_general(a, b, (((1,), (1,)), ((), ())), preferred_element_type=F32)


def _dot(a, b):
    return jnp.dot(a, b, preferred_element_type=F32)


def _params(*sem):
    return pltpu.CompilerParams(dimension_semantics=sem, vmem_limit_bytes=VMEM_LIMIT)


def _head_slope(pair, group):
    return 2.0 ** -(group * N_PAIR + pair + 1)


def _in_proj_kernel(x_ref, g_ref, w_ref, oa_ref, or_ref, h_ref, *, att_tiles):
    j = pl.program_id(1)

    @pl.when(j == 0)
    def _():
        x = x_ref[...]
        ms = jnp.mean(x * x, axis=-1, keepdims=True)
        h_ref[...] = (x * lax.rsqrt(ms + RMS_EPS) * g_ref[...]).astype(BF16)

    @pl.when(j < att_tiles)
    def _():
        oa_ref[...] = _dot(h_ref[...], w_ref[...]).astype(oa_ref.dtype)

    @pl.when(j >= att_tiles)
    def _():
        or_ref[...] = _dot(h_ref[...], w_ref[...])


def in_projection(x, g, w_all, tm, tn):
    s, d = x.shape
    n_tiles = w_all.shape[1] // tn
    att_tiles = N_ATT // tn
    rest_tiles = n_tiles - att_tiles
    kern = functools.partial(_in_proj_kernel, att_tiles=att_tiles)
    return pl.pallas_call(
        kern,
        grid=(s // tm, n_tiles),
        in_specs=[pl.BlockSpec((tm, d), lambda i, j: (i, 0)),
                  pl.BlockSpec((1, d), lambda i, j: (0, 0)),
                  pl.BlockSpec((d, tn), lambda i, j: (0, j))],
        out_specs=[pl.BlockSpec((tm, tn), lambda i, j: (i, jnp.minimum(j, att_tiles - 1))),
                   pl.BlockSpec((tm, tn), lambda i, j: (i, jnp.maximum(j - att_tiles, 0)))],
        out_shape=[jax.ShapeDtypeStruct((s, N_ATT), BF16),
                   jax.ShapeDtypeStruct((s, rest_tiles * tn), F32)],
        scratch_shapes=[pltpu.VMEM((tm, d), BF16)],
        compiler_params=_params("parallel", "arbitrary"),
        name="in_projection",
    )(x, g.reshape(1, d), w_all)


def _mm_res_kernel(a_ref, w_ref, r_ref, *rest, final_norm):
    y = r_ref[...] + _dot(a_ref[...], w_ref[...])
    if final_norm:
        g_ref, o_ref = rest
        ms = jnp.mean(y * y, axis=-1, keepdims=True)
        y = y * lax.rsqrt(ms + RMS_EPS) * g_ref[...]
    else:
        o_ref, = rest
    o_ref[...] = y


def matmul_residual(a, w, res, tm, norm_g=None):
    s, k = a.shape
    n = w.shape[1]
    in_specs = [pl.BlockSpec((tm, k), lambda i: (i, 0)),
                pl.BlockSpec((k, n), lambda i: (0, 0), pipeline_mode=pl.Buffered(1)),
                pl.BlockSpec((tm, n), lambda i: (i, 0))]
    args = [a, w, res]
    if norm_g is not None:
        in_specs.append(pl.BlockSpec((1, n), lambda i: (0, 0)))
        args.append(norm_g.reshape(1, n))
    return pl.pallas_call(
        functools.partial(_mm_res_kernel, final_norm=norm_g is not None),
        grid=(s // tm,),
        in_specs=in_specs,
        out_specs=pl.BlockSpec((tm, n), lambda i: (i, 0)),
        out_shape=jax.ShapeDtypeStruct((s, n), F32),
        compiler_params=_params("parallel"),
        name="matmul_residual",
    )(*args)


def _merge_out_kernel(ya_ref, yb_ref, yc_ref, wa_ref, wb_ref, wc_ref,
                      ga_ref, gb_ref, gc_ref, wo_ref, x_ref, o_ref):
    m = jax.nn.sigmoid(ga_ref[...]) * _dot(ya_ref[...], wa_ref[...])
    m = m + jax.nn.sigmoid(gb_ref[...]) * _dot(yb_ref[...], wb_ref[...])
    m = m + jax.nn.sigmoid(gc_ref[...]) * _dot(yc_ref[...], wc_ref[...])
    o_ref[...] = x_ref[...] + _dot(m.astype(BF16), wo_ref[...])


def merge_out_projection(ya, yb, yc, wa, wb, wc, y_rest, w_out, x, tm):
    s, k = ya.shape
    n = wa.shape[1]
    resident = pl.Buffered(1)
    y_spec = pl.BlockSpec((tm, k), lambda i: (i, 0))
    w_spec = pl.BlockSpec((k, n), lambda i: (0, 0), pipeline_mode=resident)

    def gate_spec(br):
        return pl.BlockSpec((tm, n), lambda i: (i, br))

    return pl.pallas_call(
        _merge_out_kernel,
        grid=(s // tm,),
        in_specs=[y_spec, y_spec, y_spec, w_spec, w_spec, w_spec,
                  gate_spec(0), gate_spec(1), gate_spec(2),
                  pl.BlockSpec((n, n), lambda i: (0, 0), pipeline_mode=resident),
                  pl.BlockSpec((tm, n), lambda i: (i, 0))],
        out_specs=pl.BlockSpec((tm, n), lambda i: (i, 0)),
        out_shape=jax.ShapeDtypeStruct((s, n), F32),
        compiler_params=_params("parallel"),
        name="merge_out_projection",
    )(ya, yb, yc, wa, wb, wc, y_rest, y_rest, y_rest, w_out, x)


def _ffn_up_kernel(x_ref, g_ref, wg_ref, wu_ref, o_ref, h_ref):
    @pl.when(pl.program_id(1) == 0)
    def _():
        x = x_ref[...]
        ms = jnp.mean(x * x, axis=-1, keepdims=True)
        h_ref[...] = (x * lax.rsqrt(ms + RMS_EPS) * g_ref[...]).astype(BF16)

    h = h_ref[...]
    o_ref[...] = (jax.nn.silu(_dot(h, wg_ref[...])) * _dot(h, wu_ref[...])).astype(o_ref.dtype)


def ffn_up(x, g, wg, wu, tm, tn):
    s, d = x.shape
    n = wg.shape[1]
    w_spec = pl.BlockSpec((d, tn), lambda i, j: (0, j))
    return pl.pallas_call(
        _ffn_up_kernel,
        grid=(s // tm, n // tn),
        in_specs=[pl.BlockSpec((tm, d), lambda i, j: (i, 0)),
                  pl.BlockSpec((1, d), lambda i, j: (0, 0)),
                  w_spec, w_spec],
        out_specs=pl.BlockSpec((tm, tn), lambda i, j: (i, j)),
        out_shape=jax.ShapeDtypeStruct((s, n), BF16),
        scratch_shapes=[pltpu.VMEM((tm, d), BF16)],
        compiler_params=_params("parallel", "arbitrary"),
        name="ffn_up",
    )(x, g.reshape(1, d), wg, wu)


def _half_masks():
    lane = lax.broadcasted_iota(jnp.int32, (Q_BLOCK, LANES), 1)
    return lane < HALF


def _scaled_q(q):
    return q * jnp.asarray(HALF ** -0.5, q.dtype)


def _masked_q_rows(q_ref):
    lo = _half_masks()
    rows = []
    for pair in range(N_PAIR):
        q128 = _scaled_q(q_ref[:, pair * LANES:(pair + 1) * LANES])
        for group in range(2):
            rows.append(jnp.where(lo if group == 0 else jnp.logical_not(lo), q128,
                                  jnp.zeros_like(q128)))
    return jnp.concatenate(rows, axis=0)


def _pair_tile(pv, pair):
    top = lax.broadcasted_iota(jnp.int32, (LANES, Q_BLOCK), 0) < HALF
    h = 2 * pair
    return jnp.where(top, pv[:, h * Q_BLOCK:(h + 1) * Q_BLOCK], pv[:, (h + 1) * Q_BLOCK:(h + 2) * Q_BLOCK])


def _banded_kernel(*refs, n_prev, window, has_sink):
    nk = n_prev + 1
    n_heads = 2 * N_PAIR
    if has_sink:
        sink_ref, refs = refs[0], refs[1:]
    slope_ref, q_ref = refs[0], refs[1]
    refs = refs[1:]
    k_refs = refs[1:1 + nk]
    v_refs = refs[1 + nk:1 + 2 * nk]
    o_ref = refs[1 + 2 * nk]
    i = pl.program_id(0)
    span = nk * Q_BLOCK
    k_all = jnp.concatenate([k_refs[d][...] for d in range(n_prev, -1, -1)], axis=0)
    vt_all = jnp.concatenate([v_refs[d][...] for d in range(n_prev, -1, -1)], axis=1)
    key_rel = lax.broadcasted_iota(jnp.int32, (span, Q_BLOCK), 0)
    q_rel = lax.broadcasted_iota(jnp.int32, (span, Q_BLOCK), 1)
    dist = q_rel + n_prev * Q_BLOCK - key_rel
    key_pos = (i - n_prev) * Q_BLOCK + key_rel
    mask = (dist >= 0) & (dist < window) & (key_pos >= 0)
    bias = jnp.where(mask, 0.0, NEG)
    distf = dist.astype(F32)
    bias_all = jnp.concatenate([bias] * n_heads, axis=1)
    dist_all = jnp.concatenate([distf] * n_heads, axis=1)
    s = _nt(k_all, _masked_q_rows(q_ref)) - slope_ref[...] * dist_all + bias_all
    m = jnp.max(s, axis=0, keepdims=True)
    if has_sink:
        sink = jnp.concatenate(
            [jnp.full((1, Q_BLOCK), sink_ref[(h % 2) * N_PAIR + h // 2], F32) for h in range(n_heads)],
            axis=1)
        m = jnp.maximum(m, sink)
        p = jnp.exp(s - m)
        denom = jnp.sum(p, axis=0, keepdims=True) + jnp.exp(sink - m)
    else:
        p = jnp.exp(s - m)
        denom = jnp.maximum(jnp.sum(p, axis=0, keepdims=True), 1e-30)
    pv = _dot(vt_all, p.astype(BF16)) * (1.0 / denom)
    for pair in range(N_PAIR):
        o_ref[:, pair * LANES:(pair + 1) * LANES] = _pair_tile(pv, pair).T.astype(o_ref.dtype)


def banded_attention(y_att, vt4, vt_index, col_q, col_k, window, sinks, out_dtype):
    s = y_att.shape[0]
    nb = s // Q_BLOCK
    n_prev = -(-window // Q_BLOCK)
    has_sink = sinks is not None
    in_specs = []
    args = []
    if has_sink:
        in_specs.append(pl.BlockSpec(memory_space=pltpu.SMEM))
        args.append(sinks)
    in_specs.append(pl.BlockSpec((1, 2 * N_PAIR * Q_BLOCK), lambda i: (0, 0)))
    args.append(_slope_row())
    in_specs.append(pl.BlockSpec((Q_BLOCK, N_PAIR * LANES), lambda i: (i, col_q // N_PAIR)))
    args.append(y_att)
    for d in range(n_prev + 1):
        in_specs.append(pl.BlockSpec((Q_BLOCK, LANES),
                                     lambda i, d=d: (jnp.maximum(i - d, 0), col_k)))
        args.append(y_att)
    for d in range(n_prev + 1):
        in_specs.append(pl.BlockSpec((None, None, LANES, Q_BLOCK),
                                     lambda i, d=d: (vt_index, jnp.maximum(i - d, 0), 0, 0)))
        args.append(vt4)
    kern = functools.partial(_banded_kernel, n_prev=n_prev, window=window, has_sink=has_sink)
    return pl.pallas_call(
        kern,
        grid=(nb,),
        in_specs=in_specs,
        out_specs=pl.BlockSpec((Q_BLOCK, N_PAIR * LANES), lambda i: (i, 0)),
        out_shape=jax.ShapeDtypeStruct((s, N_PAIR * LANES), out_dtype),
        compiler_params=_params("parallel"),
        name="banded_attention_w%d" % window,
    )(*args)


def _sb_kernel(q_ref, k_ref, vt_ref, o_ref, ls_ref, acc_ref, *, scale, heads):
    i = pl.program_id(1)
    row = lax.broadcasted_iota(jnp.int32, (SB_TILE, SB_TILE), 0)
    col = lax.broadcasted_iota(jnp.int32, (SB_TILE, SB_TILE), 1)
    later = jnp.where(col > row, 1.0, 0.0).astype(BF16)
    key_i = lax.broadcasted_iota(jnp.int32, (SB_TILE, heads * SB_TILE), 0)
    query_i = lax.broadcasted_iota(jnp.int32, (SB_TILE, heads * SB_TILE), 1) & (SB_TILE - 1)
    seen = key_i < query_i

    def visit(kj, diagonal):
        zz = []
        for h in range(heads):
            q = q_ref[:, h * LANES:(h + 1) * LANES]
            k = k_ref[pl.ds(pl.multiple_of(kj * SB_TILE, SB_TILE), SB_TILE),
                      h * LANES:(h + 1) * LANES]
            zz.append(_nt(k, q))
        zz = jnp.concatenate(zz, axis=1) * (scale * LOG2E)
        sp_raw = jnp.maximum(zz, 0.0) + jnp.log(1.0 + jnp.exp2(-jnp.abs(zz))) * LOG2E
        sp = jnp.where(seen, sp_raw, 0.0) if diagonal else sp_raw
        gone = ls_ref[...]
        a = jnp.exp2((zz - sp_raw) - _dot(later, sp.astype(BF16)) - gone)
        if diagonal:
            a = jnp.where(seen, a, 0.0)
        a = a.astype(BF16)
        for h in range(heads):
            acc_ref[h] += _dot(vt_ref[h, kj], a[:, h * SB_TILE:(h + 1) * SB_TILE])
        gone = gone + jnp.sum(sp, axis=0, keepdims=True)
        ls_ref[...] = gone
        return jnp.min(gone)

    ls_ref[...] = jnp.zeros(ls_ref.shape, F32)
    acc_ref[...] = jnp.zeros(acc_ref.shape, F32)
    alive0 = visit(i, True)

    def cond(carry):
        return (carry[0] < i) & (carry[1] < SB_DEAD_BITS)

    def body(carry):
        return carry[0] + 1, visit(i - 1 - carry[0], False)

    lax.while_loop(cond, body, (jnp.int32(0), alive0))
    for h in range(heads):
        o_ref[:, h * LANES:(h + 1) * LANES] = acc_ref[h].T.astype(o_ref.dtype)


def stick_breaking(y_att, vt4, heads=SB_HEADS):
    s = y_att.shape[0]
    nt = s // SB_TILE
    kern = functools.partial(_sb_kernel, scale=LANES ** -0.5, heads=heads)
    resident = pl.Buffered(1)
    return pl.pallas_call(
        kern,
        grid=(SB_HEADS // heads, nt),
        in_specs=[pl.BlockSpec((SB_TILE, heads * LANES), lambda g, i: (i, COL_QB // heads + g)),
                  pl.BlockSpec((s, heads * LANES), lambda g, i: (0, COL_KB // heads + g),
                               pipeline_mode=resident),
                  pl.BlockSpec((heads, nt, LANES, SB_TILE), lambda g, i: (g, 0, 0, 0),
                               pipeline_mode=resident)],
        out_specs=pl.BlockSpec((SB_TILE, heads * LANES), lambda g, i: (i, g)),
        out_shape=jax.ShapeDtypeStruct((s, SB_HEADS * LANES), BF16),
        scratch_shapes=[pltpu.VMEM((1, heads * SB_TILE), F32),
                        pltpu.VMEM((heads, LANES, SB_TILE), F32)],
        compiler_params=_params("parallel", "arbitrary"),
        name="stick_breaking",
    )(y_att, y_att, vt4)


def _compress_kernel(t_ref, pe_ref, w1_ref, w2_ref, o_ref):
    t = t_ref[...]
    half = w1_ref.shape[0] // 2
    w1 = w1_ref[...]
    a = _dot(t, w1[:half])
    b = _dot(t, w1[half:])
    bias = _dot(pe_ref[...], w1)[0:1]
    n = t.shape[0]
    pre = a + pltpu.roll(b, n - 1, 0) + bias
    hid = jax.nn.gelu(pre)
    o_ref[...] = _dot(hid.astype(BF16), w2_ref[...]).astype(o_ref.dtype)


def compress_blocks(t_flat, pe, w1, w2):
    g, n, k = t_flat.shape
    pe_rows = jnp.zeros((8, 2 * k), BF16).at[0].set(pe.reshape(-1).astype(BF16))
    return pl.pallas_call(
        _compress_kernel,
        grid=(g,),
        in_specs=[pl.BlockSpec((None, n, k), lambda gi: (gi, 0, 0)),
                  pl.BlockSpec((8, 2 * k), lambda gi: (0, 0)),
                  pl.BlockSpec((2 * k, CMP_HIDDEN), lambda gi: (0, 0)),
                  pl.BlockSpec((CMP_HIDDEN, HALF), lambda gi: (0, 0))],
        out_specs=pl.BlockSpec((None, n, HALF), lambda gi: (gi, 0, 0)),
        out_shape=jax.ShapeDtypeStruct((g, n, HALF), BF16),
        compiler_params=_params("parallel"),
        name="compress_blocks",
    )(t_flat, pe_rows, w1.astype(BF16), w2.astype(BF16))


def _nsa_cmp_kernel(q_ref, kc_ref, vct_ref, ov_ref, slope_ref, oc_ref, sel_ref, cnt_ref):
    i = pl.program_id(0)
    n_all = kc_ref.shape[0]
    chunk = min(CMP_CHUNK, n_all)
    needed = lax.div(8 * i + 7 + (chunk - 1), chunk)
    body = functools.partial(_nsa_cmp_body, q_ref, kc_ref, vct_ref, ov_ref, slope_ref, oc_ref,
                             sel_ref, cnt_ref)
    for c in range(1, n_all // chunk + 1):
        pl.when(needed == c)(functools.partial(body, n_cmp=c * chunk))


def _nsa_cmp_body(q_ref, kc_ref, vct_ref, ov_ref, slope_ref, oc_ref, sel_ref, cnt_ref, *, n_cmp):
    i = pl.program_id(0)
    kc = kc_ref[:n_cmp, :]
    vct = vct_ref[:, :n_cmp]
    t = i * Q_BLOCK + lax.broadcasted_iota(jnp.int32, (n_cmp, Q_BLOCK), 1)
    cmp_end = lax.broadcasted_iota(jnp.int32, (n_cmp, Q_BLOCK), 0) * CMP_STRIDE + (CMP_BLOCK - 1)
    dist = t - cmp_end
    bias = jnp.where(dist >= 0, 0.0, NEG)
    seen = jnp.where(t[0:1] >= CMP_BLOCK - 1, 1.0, 0.0)
    distf = dist.astype(F32)
    n_heads = 2 * N_PAIR
    bias_all = jnp.concatenate([bias] * n_heads, axis=1)
    dist_all = jnp.concatenate([distf] * n_heads, axis=1)
    seen_all = jnp.concatenate([seen] * n_heads, axis=1)
    s = _nt(kc, _masked_q_rows(q_ref)) - slope_ref[...] * dist_all + bias_all
    m = jnp.max(s, axis=0, keepdims=True)
    p = jnp.exp(s - m)
    p = p * (seen_all / jnp.maximum(jnp.sum(p, axis=0, keepdims=True) * seen_all, 1e-30))
    pv = _dot(vct, p.astype(BF16))
    for pair in range(N_PAIR):
        oc_ref[:, pair * LANES:(pair + 1) * LANES] = _pair_tile(pv, pair).T
    psum = []
    for group in range(2):
        heads = [p[:, (2 * pair + group) * Q_BLOCK:(2 * pair + group + 1) * Q_BLOCK]
                 for pair in range(N_PAIR)]
        psum.append(((heads[0] + heads[1]) + heads[2]) + heads[3])

    n_sel_all = ov_ref.shape[0]
    n_sel = min(n_sel_all, n_cmp // (SEL_BLOCK // CMP_STRIDE))
    def all_rows(x):
        if n_sel == n_sel_all:
            return x
        return jnp.concatenate([x, jnp.zeros((n_sel_all - n_sel, Q_BLOCK), F32)], axis=0)

    ov = ov_ref[:n_sel, :n_cmp]
    tq = i * Q_BLOCK + lax.broadcasted_iota(jnp.int32, (n_sel, Q_BLOCK), 1)
    jblk = lax.broadcasted_iota(jnp.int32, (n_sel, Q_BLOCK), 0)
    jblk_f = jblk.astype(F32)
    jt = tq // SEL_BLOCK
    valid = jblk * SEL_BLOCK <= tq
    forced = (jblk == 0) | (jblk == jt) | (jblk == jt - 1)
    total = jnp.zeros((n_sel, Q_BLOCK), F32)
    for group in range(2):
        hi = psum[group].astype(BF16)
        lo_part = (psum[group] - hi.astype(F32)).astype(BF16)
        imp = _dot(ov, hi) + _dot(ov, lo_part)
        score = jnp.where(valid, jnp.where(forced, SEL_FORCE, imp), -SEL_FORCE)
        sel = jnp.zeros((n_sel, Q_BLOCK), F32)
        for _ in range(min(SEL_TOPK, n_sel)):
            best = jnp.max(score, axis=0, keepdims=True)
            first = jnp.min(jnp.where(score == best, jblk_f, float(n_sel)), axis=0, keepdims=True)
            hit = jblk_f == first
            sel = jnp.where(hit, 1.0, sel)
            score = jnp.where(hit, SEL_TAKEN, score)
        sel_ref[0, group] = all_rows(sel)
        total = total + sel
    cnt_ref[0] = _nt(jnp.ones((8, Q_BLOCK), BF16), all_rows(total).astype(BF16))


def nsa_compressed(y_att, kc128, vct128, overlap_t):
    s = y_att.shape[0]
    nb = s // Q_BLOCK
    n_cmp = kc128.shape[0]
    n_sel = overlap_t.shape[0]
    return pl.pallas_call(
        _nsa_cmp_kernel,
        grid=(nb,),
        in_specs=[pl.BlockSpec((Q_BLOCK, N_PAIR * LANES), lambda i: (i, COL_QC // N_PAIR)),
                  pl.BlockSpec((n_cmp, LANES), lambda i: (0, 0)),
                  pl.BlockSpec((LANES, n_cmp), lambda i: (0, 0)),
                  pl.BlockSpec((n_sel, n_cmp), lambda i: (0, 0)),
                  pl.BlockSpec((1, 2 * N_PAIR * Q_BLOCK), lambda i: (0, 0))],
        out_specs=[pl.BlockSpec((Q_BLOCK, N_PAIR * LANES), lambda i: (i, 0)),
                   pl.BlockSpec((1, 2, n_sel, Q_BLOCK), lambda i: (i, 0, 0, 0)),
                   pl.BlockSpec((1, 8, n_sel), lambda i: (i, 0, 0))],
        out_shape=[jax.ShapeDtypeStruct((s, N_PAIR * LANES), F32),
                   jax.ShapeDtypeStruct((nb, 2, n_sel, Q_BLOCK), F32),
                   jax.ShapeDtypeStruct((nb, 8, n_sel), F32)],
        compiler_params=_params("parallel"),
        name="nsa_compressed",
    )(y_att, kc128, vct128, overlap_t, _slope_row())


def _nsa_sel_kernel(todo_ref, count_ref, q_ref, ks_ref, vst_ref, sel_ref, slope_ref, gc_ref, oc_ref,
                    ow_ref, o_ref, qm_ref, m_ref, l_ref, acc_ref, *, n_tiles):
    i = pl.program_id(0)
    n_heads = 2 * N_PAIR
    n_active = count_ref[i]
    lo = _half_masks()
    for pair in range(N_PAIR):
        q128 = _scaled_q(q_ref[:, pair * LANES:(pair + 1) * LANES])
        for group in range(2):
            h = pair * 2 + group
            qm_ref[h * Q_BLOCK:(h + 1) * Q_BLOCK, :] = jnp.where(
                lo if group == 0 else jnp.logical_not(lo), q128, jnp.zeros_like(q128))
    m_ref[...] = jnp.full(m_ref.shape, 0.1 * NEG, F32)
    l_ref[...] = jnp.zeros(l_ref.shape, F32)
    acc_ref[...] = jnp.zeros(acc_ref.shape, F32)
    top = lax.broadcasted_iota(jnp.int32, (Q_BLOCK, Q_BLOCK), 0) < HALF
    slopes = slope_ref[...]

    def tile_terms(p, live):
        key_pos = p * Q_BLOCK + lax.broadcasted_iota(jnp.int32, (Q_BLOCK, Q_BLOCK), 0)
        t = i * Q_BLOCK + lax.broadcasted_iota(jnp.int32, (Q_BLOCK, Q_BLOCK), 1)
        dist = t - key_pos
        bias = []
        for group in range(2):
            r0 = sel_ref[0, group, pl.ds(2 * p, 1), :]
            r1 = sel_ref[0, group, pl.ds(2 * p + 1, 1), :]
            picked = jnp.where(top, r0, r1)
            masked = jnp.where((picked > 0.5) & (dist >= 0), 0.0, NEG)
            bias.append(masked if live is True else jnp.where(live, masked, NEG))
        k = ks_ref[pl.ds(pl.multiple_of(p * Q_BLOCK, Q_BLOCK), Q_BLOCK), :]
        return k, vst_ref[p], dist.astype(F32), bias

    def body(step, carry):
        first = 2 * step
        has_second = first + 1 < n_active
        p0 = todo_ref[i * n_tiles + first]
        p1 = todo_ref[i * n_tiles + jnp.where(has_second, first + 1, first)]
        k0, vt0, dist0, bias0 = tile_terms(p0, True)
        k1, vt1, dist1, bias1 = tile_terms(p1, has_second)
        k = jnp.concatenate([k0, k1], axis=0)
        vt = jnp.concatenate([vt0, vt1], axis=1)
        bias_all = jnp.concatenate(
            [jnp.concatenate([bias0[h % 2], bias1[h % 2]], axis=0) for h in range(n_heads)], axis=1)
        dist_all = jnp.concatenate([jnp.concatenate([dist0, dist1], axis=0)] * n_heads, axis=1)
        s = _nt(k, qm_ref[...]) - slopes * dist_all + bias_all
        m_old = m_ref[...]
        m_new = jnp.maximum(m_old, jnp.max(s, axis=0, keepdims=True))
        alpha = jnp.exp(m_old - m_new)
        pr = jnp.exp((s - m_new).astype(BF16))
        vt_ext = jnp.concatenate([vt, jnp.ones((SUM_ROWS, 2 * Q_BLOCK), BF16)], axis=0)
        pv = _dot(vt_ext, pr)
        l_ref[...] = alpha * l_ref[...] + pv[LANES:LANES + 1]
        own = jnp.concatenate(
            [pv[(h % 2) * HALF:(h % 2 + 1) * HALF, h * Q_BLOCK:(h + 1) * Q_BLOCK]
             for h in range(n_heads)], axis=1)
        acc_ref[...] = alpha * acc_ref[...] + own
        m_ref[...] = m_new
        return carry

    lax.fori_loop(0, lax.div(n_active + 1, 2), body, 0)

    gates = jax.nn.sigmoid(gc_ref[...])
    o_all = acc_ref[...] * (1.0 / jnp.maximum(l_ref[...], 1e-30))
    for pair in range(N_PAIR):
        h = pair * 2
        o_s = jnp.concatenate([o_all[:, h * Q_BLOCK:(h + 1) * Q_BLOCK],
                               o_all[:, (h + 1) * Q_BLOCK:(h + 2) * Q_BLOCK]], axis=0).T
        cols = slice(pair * LANES, (pair + 1) * LANES)

        def gate(branch):
            c = branch * 8 + pair * 2
            return jnp.where(lo, gates[:, c:c + 1], gates[:, c + 1:c + 2])

        y = gate(0) * oc_ref[:, cols] + gate(1) * o_s + gate(2) * ow_ref[:, cols]
        o_ref[:, cols] = y.astype(o_ref.dtype)


def _slope_row():
    row = np.concatenate([np.full((Q_BLOCK,), _head_slope(h // 2, h % 2), np.float32)
                          for h in range(2 * N_PAIR)])
    return jnp.asarray(row.reshape(1, -1))


def _visit_lists(cnt):
    nb = cnt.shape[0]
    tile = jnp.arange(nb, dtype=jnp.int32)
    picked = (cnt[:, 0, :].reshape(nb, nb, 2).sum(-1) > 0.5) & (tile[None, :] <= tile[:, None])
    slot = jnp.cumsum(picked.astype(jnp.int32), axis=1) - 1
    hit = picked[:, :, None] & (slot[:, :, None] == tile[None, None, :])
    todo = jnp.sum(jnp.where(hit, tile[None, :, None], 0), axis=1)
    return todo.reshape(-1), picked.sum(axis=1).astype(jnp.int32)


def nsa_selected(cnt, y_att, vt4, vt_index, sel_t, y_rest, o_c, o_w):
    s = y_att.shape[0]
    nb = s // Q_BLOCK
    n_sel = sel_t.shape[2]
    n_heads = 2 * N_PAIR
    todo, count = _visit_lists(cnt)
    kern = functools.partial(_nsa_sel_kernel, n_tiles=nb)
    wide = pl.BlockSpec((Q_BLOCK, N_PAIR * LANES), lambda i, *_: (i, 0))
    grid_spec = pltpu.PrefetchScalarGridSpec(
        num_scalar_prefetch=2,
        grid=(nb,),
        in_specs=[pl.BlockSpec((Q_BLOCK, N_PAIR * LANES), lambda i, *_: (i, COL_QC // N_PAIR)),
                  pl.BlockSpec((s, LANES), lambda i, *_: (0, COL_KS)),
                  pl.BlockSpec((None, nb, LANES, Q_BLOCK), lambda i, *_: (vt_index, 0, 0, 0)),
                  pl.BlockSpec((1, 2, n_sel, Q_BLOCK), lambda i, *_: (i, 0, 0, 0)),
                  pl.BlockSpec((1, n_heads * Q_BLOCK), lambda i, *_: (0, 0)),
                  pl.BlockSpec((Q_BLOCK, LANES), lambda i, *_: (i, N_GATES // LANES)),
                  wide, wide],
        out_specs=wide,
        scratch_shapes=[pltpu.VMEM((n_heads * Q_BLOCK, LANES), BF16),
                        pltpu.VMEM((1, n_heads * Q_BLOCK), F32),
                        pltpu.VMEM((1, n_heads * Q_BLOCK), F32),
                        pltpu.VMEM((HALF, n_heads * Q_BLOCK), F32)],
    )
    return pl.pallas_call(
        kern,
        grid_spec=grid_spec,
        out_shape=jax.ShapeDtypeStruct((s, N_PAIR * LANES), BF16),
        compiler_params=_params("arbitrary"),
        name="nsa_selected",
    )(todo, count, y_att, y_att, vt4, sel_t, _slope_row(), y_rest, o_c, o_w)


def _pair_cols(w):
    lead = w.shape[:-1]
    return w.reshape(*lead, 2, N_PAIR, HALF).swapaxes(-3, -2).reshape(*lead, 2 * N_PAIR * HALF)


def _pair_rows(w):
    return w.reshape(2, N_PAIR, HALF, w.shape[-1]).swapaxes(0, 1).reshape(2 * N_PAIR * HALF, w.shape[-1])


def _prep_w_in(w_in):
    qa = _pair_cols(w_in[:, 0:512])
    qc = _pair_cols(w_in[:, 2304:2816])
    cols = lambda a, b: w_in[:, a:b]
    w_att = jnp.concatenate(
        [qa, qc, cols(768, 2304),
         cols(512, 640), cols(2816, 3072), cols(3072, 3200), cols(3328, 3456),
         cols(640, 768), cols(3200, 3328), cols(3456, 3584)],
        axis=1).astype(BF16)
    gc = w_in[:, N_ATT:N_ATT + N_GC].reshape(-1, 2, N_PAIR, 3)
    gc = gc.transpose(0, 3, 2, 1).reshape(-1, N_GC)
    gc = jnp.pad(gc, ((0, 0), (0, IN_PROJ_TN - N_GC)))
    return jnp.concatenate([w_att, w_in[:, N_ATT + N_GC:].astype(BF16), gc.astype(BF16)], axis=1)


def _transpose_kernel(x_ref, o_ref):
    tile = o_ref.shape[-1]
    for c in range(o_ref.shape[0]):
        o_ref[c] = x_ref[c * tile:(c + 1) * tile, :].T


def transpose_tiles(y_att, col, width, tile):
    s = y_att.shape[0]
    rows = min(TRANSPOSE_ROWS, s)
    return pl.pallas_call(
        _transpose_kernel,
        grid=(width, s // rows),
        in_specs=[pl.BlockSpec((rows, LANES), lambda c, r: (r, col + c))],
        out_specs=pl.BlockSpec((None, rows // tile, LANES, tile), lambda c, r: (c, r, 0, 0)),
        out_shape=jax.ShapeDtypeStruct((width, s // tile, LANES, tile), y_att.dtype),
        compiler_params=_params("parallel", "parallel"),
        name="transpose_tiles",
    )(y_att)


def _chunk_rows(y_att, col):
    s = y_att.shape[0]
    t = y_att[:, col * LANES:(col + 1) * LANES].reshape(s // CMP_STRIDE, CMP_STRIDE, 2, HALF)
    return t.transpose(2, 0, 1, 3).reshape(2, s // CMP_STRIDE, CMP_STRIDE * HALF)


def _overlap_t(s):
    n_cmp = s // CMP_STRIDE
    n_sel = s // SEL_BLOCK
    cmp_lo = np.arange(n_cmp) * CMP_STRIDE
    cmp_end = cmp_lo + CMP_BLOCK - 1
    sel_lo = np.arange(n_sel) * SEL_BLOCK
    ov = ((cmp_lo[None, :] <= sel_lo[:, None] + SEL_BLOCK - 1)
          & (cmp_end[None, :] >= sel_lo[:, None]) & (np.arange(n_cmp)[None, :] < n_cmp - 1))
    return jnp.asarray(ov.astype(np.float32), dtype=BF16)


def nsa_attention(y_att, y_rest, vt4, pe_k, w1_k, w2_k, pe_v, w1_v, w2_v):
    s = y_att.shape[0]
    kc = compress_blocks(_chunk_rows(y_att, COL_KC), pe_k, w1_k, w2_k)
    vc = compress_blocks(_chunk_rows(y_att, COL_VC), pe_v, w1_v, w2_v)
    kc128 = kc.transpose(1, 0, 2).reshape(kc.shape[1], LANES)
    vct128 = vc.transpose(0, 2, 1).reshape(LANES, vc.shape[1])
    o_c, sel_t, cnt = nsa_compressed(y_att, kc128, vct128, _overlap_t(s))
    o_w = banded_attention(y_att, vt4, COL_VW - COL_VA, COL_QC, COL_KW, NSA_WINDOW, None, F32)
    return nsa_selected(cnt, y_att, vt4, COL_VS - COL_VA, sel_t, y_rest, o_c, o_w)


def mixer_layer(x, norm_mix, w_in, sinks, pe_k, w1_k, w2_k, pe_v, w1_v, w2_v,
                w_br_a, w_br_b, w_br_c, w_out):
    y_att, y_rest = in_projection(x, norm_mix, _prep_w_in(w_in), IN_PROJ_TM, IN_PROJ_TN)
    vt4 = transpose_tiles(y_att, COL_VA, 3, Q_BLOCK)
    y_a = banded_attention(y_att, vt4, 0, COL_QA, COL_KA, SWA_WINDOW, sinks, BF16)
    y_b = stick_breaking(y_att, transpose_tiles(y_att, COL_VB, SB_HEADS, SB_TILE))
    y_c = nsa_attention(y_att, y_rest, vt4, pe_k, w1_k, w2_k, pe_v, w1_v, w2_v)
    return merge_out_projection(y_a, y_b, y_c, _pair_rows(w_br_a).astype(BF16),
                                w_br_b.astype(BF16), _pair_rows(w_br_c).astype(BF16),
                                y_rest, w_out.astype(BF16), x, ROW_TM)


def ffn_layer(x, norm_ffn, w_gate, w_up, w_down, norm_after=None):
    u = ffn_up(x, norm_ffn, w_gate.astype(BF16), w_up.astype(BF16), IN_PROJ_TM, FFN_TN)
    return matmul_residual(u, w_down.astype(BF16), x, ROW_TM, norm_after)


def kernel(x, norm_mix, w_in, swa_sinks, cmp_pe_k, cmp_w1_k, cmp_w2_k, cmp_pe_v, cmp_w1_v,
           cmp_w2_v, w_branch_swa, w_branch_sb, w_branch_nsa, w_out, norm_ffn, w_gate, w_up,
           w_down, norm_final):
    b, s, d = x.shape
    outs = []
    for bi in range(b):
        xb = x[bi]
        depth = norm_mix.shape[0]
        assert depth >= 1
        for layer in range(depth):
            xb = mixer_layer(xb, norm_mix[layer], w_in[layer], swa_sinks[layer],
                             cmp_pe_k[layer], cmp_w1_k[layer], cmp_w2_k[layer],
                             cmp_pe_v[layer], cmp_w1_v[layer], cmp_w2_v[layer],
                             w_branch_swa[layer], w_branch_sb[layer], w_branch_nsa[layer],
                             w_out[layer])
            xb = ffn_layer(xb, norm_ffn[layer], w_gate[layer], w_up[layer], w_down[layer],
                           norm_final if layer == depth - 1 else None)
        outs.append(xb)
    return jnp.stack(outs, axis=0)
```

```python
import functools

import jax
import jax.numpy as jnp
import numpy as np
from jax import lax
from jax.experimental import pallas as pl
from jax.experimental.pallas import tpu as pltpu

F32 = jnp.float32
BF16 = jnp.bfloat16

D_MODEL = 2048
Q_BLOCK = 128
LANES = 128
HALF = 64
N_PAIR = 4
SWA_WINDOW = 128
NSA_WINDOW = 512
CMP_BLOCK = 32
CMP_STRIDE = 16
CMP_HIDDEN = 256
CMP_CHUNK = 256
SEL_BLOCK = 64
SEL_TOPK = 8
SB_HEADS = 4
SB_TILE = 256
D_FF = 5632
NEG = -1e30
SEL_FORCE = 1e9
SEL_TAKEN = -3e38
SUM_ROWS = 16
RMS_EPS = 1e-6
LOG2E = 1.4426950408889634
SB_DEAD_BITS = 150.0
VMEM_LIMIT = 56 * 1024 * 1024

N_ATT = 3584
N_GC = 24
N_GATES = 3 * D_MODEL
IN_PROJ_TM = 1024
IN_PROJ_TN = 512
FFN_TN = 512
ROW_TM = 256
COL_QA, COL_QC, COL_QB, COL_KB, COL_VB = 0, 4, 8, 12, 16
COL_KA, COL_KC, COL_VC, COL_KS, COL_KW = 20, 21, 22, 23, 24
COL_VA, COL_VS, COL_VW = 25, 26, 27
TRANSPOSE_ROWS = 2048


def _nt(a, b):
    return lax.dot_general(a, b, (((1,), (1,)), ((), ())), preferred_element_type=F32)


def _dot(a, b):
    return jnp.dot(a, b, preferred_element_type=F32)


def _params(*sem):
    return pltpu.CompilerParams(dimension_semantics=sem, vmem_limit_bytes=VMEM_LIMIT)


def _head_slope(pair, group):
    return 2.0 ** -(group * N_PAIR + pair + 1)


def _in_proj_kernel(x_ref, g_ref, w_ref, oa_ref, or_ref, h_ref, *, att_tiles):
    j = pl.program_id(1)

    @pl.when(j == 0)
    def _():
        x = x_ref[...]
        ms = jnp.mean(x * x, axis=-1, keepdims=True)
        h_ref[...] = (x * lax.rsqrt(ms + RMS_EPS) * g_ref[...]).astype(BF16)

    @pl.when(j < att_tiles)
    def _():
        oa_ref[...] = _dot(h_ref[...], w_ref[...]).astype(oa_ref.dtype)

    @pl.when(j >= att_tiles)
    def _():
        or_ref[...] = _dot(h_ref[...], w_ref[...])


def in_projection(x, g, w_all, tm, tn):
    s, d = x.shape
    n_tiles = w_all.shape[1] // tn
    att_tiles = N_ATT // tn
    rest_tiles = n_tiles - att_tiles
    kern = functools.partial(_in_proj_kernel, att_tiles=att_tiles)
    return pl.pallas_call(
        kern,
        grid=(s // tm, n_tiles),
        in_specs=[pl.BlockSpec((tm, d), lambda i, j: (i, 0)),
                  pl.BlockSpec((1, d), lambda i, j: (0, 0)),
                  pl.BlockSpec((d, tn), lambda i, j: (0, j))],
        out_specs=[pl.BlockSpec((tm, tn), lambda i, j: (i, jnp.minimum(j, att_tiles - 1))),
                   pl.BlockSpec((tm, tn), lambda i, j: (i, jnp.maximum(j - att_tiles, 0)))],
        out_shape=[jax.ShapeDtypeStruct((s, N_ATT), BF16),
                   jax.ShapeDtypeStruct((s, rest_tiles * tn), F32)],
        scratch_shapes=[pltpu.VMEM((tm, d), BF16)],
        compiler_params=_params("parallel", "arbitrary"),
        name="in_projection",
    )(x, g.reshape(1, d), w_all)


def _mm_res_kernel(a_ref, w_ref, r_ref, *rest, final_norm):
    y = r_ref[...] + _dot(a_ref[...], w_ref[...])
    if final_norm:
        g_ref, o_ref = rest
        ms = jnp.mean(y * y, axis=-1, keepdims=True)
        y = y * lax.rsqrt(ms + RMS_EPS) * g_ref[...]
    else:
        o_ref, = rest
    o_ref[...] = y


def matmul_residual(a, w, res, tm, norm_g=None):
    s, k = a.shape
    n = w.shape[1]
    in_specs = [pl.BlockSpec((tm, k), lambda i: (i, 0)),
                pl.BlockSpec((k, n), lambda i: (0, 0), pipeline_mode=pl.Buffered(1)),
                pl.BlockSpec((tm, n), lambda i: (i, 0))]
    args = [a, w, res]
    if norm_g is not None:
        in_specs.append(pl.BlockSpec((1, n), lambda i: (0, 0)))
        args.append(norm_g.reshape(1, n))
    return pl.pallas_call(
        functools.partial(_mm_res_kernel, final_norm=norm_g is not None),
        grid=(s // tm,),
        in_specs=in_specs,
        out_specs=pl.BlockSpec((tm, n), lambda i: (i, 0)),
        out_shape=jax.ShapeDtypeStruct((s, n), F32),
        compiler_params=_params("parallel"),
        name="matmul_residual",
    )(*args)


def _merge_out_kernel(ya_ref, yb_ref, yc_ref, wa_ref, wb_ref, wc_ref,
                      ga_ref, gb_ref, gc_ref, wo_ref, x_ref, o_ref):
    m = jax.nn.sigmoid(ga_ref[...]) * _dot(ya_ref[...], wa_ref[...])
    m = m + jax.nn.sigmoid(gb_ref[...]) * _dot(yb_ref[...], wb_ref[...])
    m = m + jax.nn.sigmoid(gc_ref[...]) * _dot(yc_ref[...], wc_ref[...])
    o_ref[...] = x_ref[...] + _dot(m.astype(BF16), wo_ref[...])


def merge_out_projection(ya, yb, yc, wa, wb, wc, y_rest, w_out, x, tm):
    s, k = ya.shape
    n = wa.shape[1]
    resident = pl.Buffered(1)
    y_spec = pl.BlockSpec((tm, k), lambda i: (i, 0))
    w_spec = pl.BlockSpec((k, n), lambda i: (0, 0), pipeline_mode=resident)

    def gate_spec(br):
        return pl.BlockSpec((tm, n), lambda i: (i, br))

    return pl.pallas_call(
        _merge_out_kernel,
        grid=(s // tm,),
        in_specs=[y_spec, y_spec, y_spec, w_spec, w_spec, w_spec,
                  gate_spec(0), gate_spec(1), gate_spec(2),
                  pl.BlockSpec((n, n), lambda i: (0, 0), pipeline_mode=resident),
                  pl.BlockSpec((tm, n), lambda i: (i, 0))],
        out_specs=pl.BlockSpec((tm, n), lambda i: (i, 0)),
        out_shape=jax.ShapeDtypeStruct((s, n), F32),
        compiler_params=_params("parallel"),
        name="merge_out_projection",
    )(ya, yb, yc, wa, wb, wc, y_rest, y_rest, y_rest, w_out, x)


def _ffn_up_kernel(x_ref, g_ref, wg_ref, wu_ref, o_ref, h_ref):
    @pl.when(pl.program_id(1) == 0)
    def _():
        x = x_ref[...]
        ms = jnp.mean(x * x, axis=-1, keepdims=True)
        h_ref[...] = (x * lax.rsqrt(ms + RMS_EPS) * g_ref[...]).astype(BF16)

    h = h_ref[...]
    o_ref[...] = (jax.nn.silu(_dot(h, wg_ref[...])) * _dot(h, wu_ref[...])).astype(o_ref.dtype)


def ffn_up(x, g, wg, wu, tm, tn):
    s, d = x.shape
    n = wg.shape[1]
    w_spec = pl.BlockSpec((d, tn), lambda i, j: (0, j))
    return pl.pallas_call(
        _ffn_up_kernel,
        grid=(s // tm, n // tn),
        in_specs=[pl.BlockSpec((tm, d), lambda i, j: (i, 0)),
                  pl.BlockSpec((1, d), lambda i, j: (0, 0)),
                  w_spec, w_spec],
        out_specs=pl.BlockSpec((tm, tn), lambda i, j: (i, j)),
        out_shape=jax.ShapeDtypeStruct((s, n), BF16),
        scratch_shapes=[pltpu.VMEM((tm, d), BF16)],
        compiler_params=_params("parallel", "arbitrary"),
        name="ffn_up",
    )(x, g.reshape(1, d), wg, wu)


def _half_masks():
    lane = lax.broadcasted_iota(jnp.int32, (Q_BLOCK, LANES), 1)
    return lane < HALF


def _scaled_q(q):
    return q * jnp.asarray(HALF ** -0.5, q.dtype)


def _masked_q_rows(q_ref):
    lo = _half_masks()
    rows = []
    for pair in range(N_PAIR):
        q128 = _scaled_q(q_ref[:, pair * LANES:(pair + 1) * LANES])
        for group in range(2):
            rows.append(jnp.where(lo if group == 0 else jnp.logical_not(lo), q128,
                                  jnp.zeros_like(q128)))
    return jnp.concatenate(rows, axis=0)


def _pair_tile(pv, pair):
    top = lax.broadcasted_iota(jnp.int32, (LANES, Q_BLOCK), 0) < HALF
    h = 2 * pair
    return jnp.where(top, pv[:, h * Q_BLOCK:(h + 1) * Q_BLOCK], pv[:, (h + 1) * Q_BLOCK:(h + 2) * Q_BLOCK])


def _banded_kernel(*refs, n_prev, window, has_sink):
    nk = n_prev + 1
    n_heads = 2 * N_PAIR
    if has_sink:
        sink_ref, refs = refs[0], refs[1:]
    slope_ref, q_ref, k_ref, vt_ref, o_ref = refs
    i = pl.program_id(0)
    span = nk * Q_BLOCK
    first = jnp.maximum(i - n_prev, 0)
    k_all = k_ref[pl.ds(pl.multiple_of(first * Q_BLOCK, Q_BLOCK), span), :]
    vt_all = jnp.concatenate([vt_ref[first + d] for d in range(nk)], axis=1)
    key_pos = first * Q_BLOCK + lax.broadcasted_iota(jnp.int32, (span, Q_BLOCK), 0)
    t = i * Q_BLOCK + lax.broadcasted_iota(jnp.int32, (span, Q_BLOCK), 1)
    dist = t - key_pos
    mask = (dist >= 0) & (dist < window)
    bias = jnp.where(mask, 0.0, NEG)
    distf = dist.astype(F32)
    bias_all = jnp.concatenate([bias] * n_heads, axis=1)
    dist_all = jnp.concatenate([distf] * n_heads, axis=1)
    s = _nt(k_all, _masked_q_rows(q_ref)) - slope_ref[...] * dist_all + bias_all
    m = jnp.max(s, axis=0, keepdims=True)
    if has_sink:
        sink = jnp.concatenate(
            [jnp.full((1, Q_BLOCK), sink_ref[(h % 2) * N_PAIR + h // 2], F32) for h in range(n_heads)],
            axis=1)
        m = jnp.maximum(m, sink)
        p = jnp.exp(s - m)
        denom = jnp.sum(p, axis=0, keepdims=True) + jnp.exp(sink - m)
    else:
        p = jnp.exp(s - m)
        denom = jnp.maximum(jnp.sum(p, axis=0, keepdims=True), 1e-30)
    pv = _dot(vt_all, p.astype(BF16)) * (1.0 / denom)
    for pair in range(N_PAIR):
        o_ref[:, pair * LANES:(pair + 1) * LANES] = _pair_tile(pv, pair).T.astype(o_ref.dtype)


def banded_attention(y_att, vt4, vt_index, col_q, col_k, window, sinks, out_dtype):
    s = y_att.shape[0]
    nb = s // Q_BLOCK
    n_prev = -(-window // Q_BLOCK)
    has_sink = sinks is not None
    in_specs = []
    args = []
    if has_sink:
        in_specs.append(pl.BlockSpec(memory_space=pltpu.SMEM))
        args.append(sinks)
    in_specs.append(pl.BlockSpec((1, 2 * N_PAIR * Q_BLOCK), lambda i: (0, 0)))
    args.append(_slope_row())
    in_specs.append(pl.BlockSpec((Q_BLOCK, N_PAIR * LANES), lambda i: (i, col_q // N_PAIR)))
    args.append(y_att)
    assert nb > n_prev
    in_specs.append(pl.BlockSpec((s, LANES), lambda i: (0, col_k)))
    args.append(y_att)
    in_specs.append(pl.BlockSpec((None, nb, LANES, Q_BLOCK), lambda i: (vt_index, 0, 0, 0)))
    args.append(vt4)
    kern = functools.partial(_banded_kernel, n_prev=n_prev, window=window, has_sink=has_sink)
    return pl.pallas_call(
        kern,
        grid=(nb,),
        in_specs=in_specs,
        out_specs=pl.BlockSpec((Q_BLOCK, N_PAIR * LANES), lambda i: (i, 0)),
        out_shape=jax.ShapeDtypeStruct((s, N_PAIR * LANES), out_dtype),
        compiler_params=_params("parallel"),
        name="banded_attention_w%d" % window,
    )(*args)


def _sb_kernel(q_ref, k_ref, vt_ref, o_ref, ls_ref, acc_ref, *, scale, heads):
    i = pl.program_id(1)
    row = lax.broadcasted_iota(jnp.int32, (SB_TILE, SB_TILE), 0)
    col = lax.broadcasted_iota(jnp.int32, (SB_TILE, SB_TILE), 1)
    later = jnp.where(col > row, 1.0, 0.0).astype(BF16)
    key_i = lax.broadcasted_iota(jnp.int32, (SB_TILE, heads * SB_TILE), 0)
    query_i = lax.broadcasted_iota(jnp.int32, (SB_TILE, heads * SB_TILE), 1) & (SB_TILE - 1)
    seen = key_i < query_i

    def visit(kj, diagonal):
        zz = []
        for h in range(heads):
            q = q_ref[:, h * LANES:(h + 1) * LANES]
            k = k_ref[pl.ds(pl.multiple_of(kj * SB_TILE, SB_TILE), SB_TILE),
                      h * LANES:(h + 1) * LANES]
            zz.append(_nt(k, q))
        zz = jnp.concatenate(zz, axis=1) * (scale * LOG2E)
        sp_raw = jnp.maximum(zz, 0.0) + jnp.log(1.0 + jnp.exp2(-jnp.abs(zz))) * LOG2E
        sp = jnp.where(seen, sp_raw, 0.0) if diagonal else sp_raw
        gone = ls_ref[...]
        a = jnp.exp2((zz - sp_raw) - _dot(later, sp.astype(BF16)) - gone)
        if diagonal:
            a = jnp.where(seen, a, 0.0)
        a = a.astype(BF16)
        for h in range(heads):
            acc_ref[h] += _dot(vt_ref[h, kj], a[:, h * SB_TILE:(h + 1) * SB_TILE])
        gone = gone + jnp.sum(sp, axis=0, keepdims=True)
        ls_ref[...] = gone
        return jnp.min(gone)

    ls_ref[...] = jnp.zeros(ls_ref.shape, F32)
    acc_ref[...] = jnp.zeros(acc_ref.shape, F32)
    alive0 = visit(i, True)

    def cond(carry):
        return (carry[0] < i) & (carry[1] < SB_DEAD_BITS)

    def body(carry):
        return carry[0] + 1, visit(i - 1 - carry[0], False)

    lax.while_loop(cond, body, (jnp.int32(0), alive0))
    for h in range(heads):
        o_ref[:, h * LANES:(h + 1) * LANES] = acc_ref[h].T.astype(o_ref.dtype)


def stick_breaking(y_att, vt4, heads=SB_HEADS):
    s = y_att.shape[0]
    nt = s // SB_TILE
    kern = functools.partial(_sb_kernel, scale=LANES ** -0.5, heads=heads)
    resident = pl.Buffered(1)
    return pl.pallas_call(
        kern,
        grid=(SB_HEADS // heads, nt),
        in_specs=[pl.BlockSpec((SB_TILE, heads * LANES), lambda g, i: (i, COL_QB // heads + g)),
                  pl.BlockSpec((s, heads * LANES), lambda g, i: (0, COL_KB // heads + g),
                               pipeline_mode=resident),
                  pl.BlockSpec((heads, nt, LANES, SB_TILE), lambda g, i: (g, 0, 0, 0),
                               pipeline_mode=resident)],
        out_specs=pl.BlockSpec((SB_TILE, heads * LANES), lambda g, i: (i, g)),
        out_shape=jax.ShapeDtypeStruct((s, SB_HEADS * LANES), BF16),
        scratch_shapes=[pltpu.VMEM((1, heads * SB_TILE), F32),
                        pltpu.VMEM((heads, LANES, SB_TILE), F32)],
        compiler_params=_params("parallel", "arbitrary"),
        name="stick_breaking",
    )(y_att, y_att, vt4)


def _compress_kernel(t_ref, pe_ref, w1_ref, w2_ref, o_ref):
    t = t_ref[...]
    half = w1_ref.shape[0] // 2
    w1 = w1_ref[...]
    a = _dot(t, w1[:half])
    b = _dot(t, w1[half:])
    bias = _dot(pe_ref[...], w1)[0:1]
    n = t.shape[0]
    pre = a + pltpu.roll(b, n - 1, 0) + bias
    hid = jax.nn.gelu(pre)
    o_ref[...] = _dot(hid.astype(BF16), w2_ref[...]).astype(o_ref.dtype)


def compress_blocks(t_flat, pe, w1, w2):
    g, n, k = t_flat.shape
    pe_rows = jnp.zeros((8, 2 * k), BF16).at[0].set(pe.reshape(-1).astype(BF16))
    return pl.pallas_call(
        _compress_kernel,
        grid=(g,),
        in_specs=[pl.BlockSpec((None, n, k), lambda gi: (gi, 0, 0)),
                  pl.BlockSpec((8, 2 * k), lambda gi: (0, 0)),
                  pl.BlockSpec((2 * k, CMP_HIDDEN), lambda gi: (0, 0)),
                  pl.BlockSpec((CMP_HIDDEN, HALF), lambda gi: (0, 0))],
        out_specs=pl.BlockSpec((None, n, HALF), lambda gi: (gi, 0, 0)),
        out_shape=jax.ShapeDtypeStruct((g, n, HALF), BF16),
        compiler_params=_params("parallel"),
        name="compress_blocks",
    )(t_flat, pe_rows, w1.astype(BF16), w2.astype(BF16))


def _nsa_cmp_kernel(q_ref, kc_ref, vct_ref, ov_ref, slope_ref, oc_ref, sel_ref, cnt_ref):
    i = pl.program_id(0)
    n_all = kc_ref.shape[0]
    chunk = min(CMP_CHUNK, n_all)
    needed = lax.div(8 * i + 7 + (chunk - 1), chunk)
    body = functools.partial(_nsa_cmp_body, q_ref, kc_ref, vct_ref, ov_ref, slope_ref, oc_ref,
                             sel_ref, cnt_ref)
    for c in range(1, n_all // chunk + 1):
        pl.when(needed == c)(functools.partial(body, n_cmp=c * chunk))


def _nsa_cmp_body(q_ref, kc_ref, vct_ref, ov_ref, slope_ref, oc_ref, sel_ref, cnt_ref, *, n_cmp):
    i = pl.program_id(0)
    kc = kc_ref[:n_cmp, :]
    vct = vct_ref[:, :n_cmp]
    t = i * Q_BLOCK + lax.broadcasted_iota(jnp.int32, (n_cmp, Q_BLOCK), 1)
    cmp_end = lax.broadcasted_iota(jnp.int32, (n_cmp, Q_BLOCK), 0) * CMP_STRIDE + (CMP_BLOCK - 1)
    dist = t - cmp_end
    bias = jnp.where(dist >= 0, 0.0, NEG)
    seen = jnp.where(t[0:1] >= CMP_BLOCK - 1, 1.0, 0.0)
    distf = dist.astype(F32)
    n_heads = 2 * N_PAIR
    bias_all = jnp.concatenate([bias] * n_heads, axis=1)
    dist_all = jnp.concatenate([distf] * n_heads, axis=1)
    seen_all = jnp.concatenate([seen] * n_heads, axis=1)
    s = _nt(kc, _masked_q_rows(q_ref)) - slope_ref[...] * dist_all + bias_all
    m = jnp.max(s, axis=0, keepdims=True)
    p = jnp.exp(s - m)
    p = p * (seen_all / jnp.maximum(jnp.sum(p, axis=0, keepdims=True) * seen_all, 1e-30))
    pv = _dot(vct, p.astype(BF16))
    for pair in range(N_PAIR):
        oc_ref[:, pair * LANES:(pair + 1) * LANES] = _pair_tile(pv, pair).T
    psum = []
    for group in range(2):
        heads = [p[:, (2 * pair + group) * Q_BLOCK:(2 * pair + group + 1) * Q_BLOCK]
                 for pair in range(N_PAIR)]
        psum.append(((heads[0] + heads[1]) + heads[2]) + heads[3])

    n_sel_all = ov_ref.shape[0]
    n_sel = min(n_sel_all, n_cmp // (SEL_BLOCK // CMP_STRIDE))
    def all_rows(x):
        if n_sel == n_sel_all:
            return x
        return jnp.concatenate([x, jnp.zeros((n_sel_all - n_sel, Q_BLOCK), F32)], axis=0)

    ov = ov_ref[:n_sel, :n_cmp]
    tq = i * Q_BLOCK + lax.broadcasted_iota(jnp.int32, (n_sel, Q_BLOCK), 1)
    jblk = lax.broadcasted_iota(jnp.int32, (n_sel, Q_BLOCK), 0)
    jblk_f = jblk.astype(F32)
    jt = tq // SEL_BLOCK
    valid = jblk * SEL_BLOCK <= tq
    forced = (jblk == 0) | (jblk == jt) | (jblk == jt - 1)
    total = jnp.zeros((n_sel, Q_BLOCK), F32)
    for group in range(2):
        hi = psum[group].astype(BF16)
        lo_part = (psum[group] - hi.astype(F32)).astype(BF16)
        imp = _dot(ov, hi) + _dot(ov, lo_part)
        score = jnp.where(valid, jnp.where(forced, SEL_FORCE, imp), -SEL_FORCE)
        sel = jnp.zeros((n_sel, Q_BLOCK), F32)
        for _ in range(min(SEL_TOPK, n_sel)):
            best = jnp.max(score, axis=0, keepdims=True)
            first = jnp.min(jnp.where(score == best, jblk_f, float(n_sel)), axis=0, keepdims=True)
            hit = jblk_f == first
            sel = jnp.where(hit, 1.0, sel)
            score = jnp.where(hit, SEL_TAKEN, score)
        sel_ref[0, group] = all_rows(sel)
        total = total + sel
    cnt_ref[0] = _nt(jnp.ones((8, Q_BLOCK), BF16), all_rows(total).astype(BF16))


def nsa_compressed(y_att, kc128, vct128, overlap_t):
    s = y_att.shape[0]
    nb = s // Q_BLOCK
    n_cmp = kc128.shape[0]
    n_sel = overlap_t.shape[0]
    return pl.pallas_call(
        _nsa_cmp_kernel,
        grid=(nb,),
        in_specs=[pl.BlockSpec((Q_BLOCK, N_PAIR * LANES), lambda i: (i, COL_QC // N_PAIR)),
                  pl.BlockSpec((n_cmp, LANES), lambda i: (0, 0)),
                  pl.BlockSpec((LANES, n_cmp), lambda i: (0, 0)),
                  pl.BlockSpec((n_sel, n_cmp), lambda i: (0, 0)),
                  pl.BlockSpec((1, 2 * N_PAIR * Q_BLOCK), lambda i: (0, 0))],
        out_specs=[pl.BlockSpec((Q_BLOCK, N_PAIR * LANES), lambda i: (i, 0)),
                   pl.BlockSpec((1, 2, n_sel, Q_BLOCK), lambda i: (i, 0, 0, 0)),
                   pl.BlockSpec((1, 8, n_sel), lambda i: (i, 0, 0))],
        out_shape=[jax.ShapeDtypeStruct((s, N_PAIR * LANES), F32),
                   jax.ShapeDtypeStruct((nb, 2, n_sel, Q_BLOCK), F32),
                   jax.ShapeDtypeStruct((nb, 8, n_sel), F32)],
        compiler_params=_params("parallel"),
        name="nsa_compressed",
    )(y_att, kc128, vct128, overlap_t, _slope_row())


def _nsa_sel_kernel(todo_ref, count_ref, q_ref, ks_ref, vst_ref, sel_ref, slope_ref, gc_ref, oc_ref,
                    ow_ref, o_ref, qm_ref, m_ref, l_ref, acc_ref, *, n_tiles):
    i = pl.program_id(0)
    n_heads = 2 * N_PAIR
    n_active = count_ref[i]
    lo = _half_masks()
    for pair in range(N_PAIR):
        q128 = _scaled_q(q_ref[:, pair * LANES:(pair + 1) * LANES])
        for group in range(2):
            h = pair * 2 + group
            qm_ref[h * Q_BLOCK:(h + 1) * Q_BLOCK, :] = jnp.where(
                lo if group == 0 else jnp.logical_not(lo), q128, jnp.zeros_like(q128))
    m_ref[...] = jnp.full(m_ref.shape, 0.1 * NEG, F32)
    l_ref[...] = jnp.zeros(l_ref.shape, F32)
    acc_ref[...] = jnp.zeros(acc_ref.shape, F32)
    top = lax.broadcasted_iota(jnp.int32, (Q_BLOCK, Q_BLOCK), 0) < HALF
    slopes = slope_ref[...]

    def tile_terms(p, live):
        key_pos = p * Q_BLOCK + lax.broadcasted_iota(jnp.int32, (Q_BLOCK, Q_BLOCK), 0)
        t = i * Q_BLOCK + lax.broadcasted_iota(jnp.int32, (Q_BLOCK, Q_BLOCK), 1)
        dist = t - key_pos
        bias = []
        for group in range(2):
            r0 = sel_ref[0, group, pl.ds(2 * p, 1), :]
            r1 = sel_ref[0, group, pl.ds(2 * p + 1, 1), :]
            picked = jnp.where(top, r0, r1)
            masked = jnp.where((picked > 0.5) & (dist >= 0), 0.0, NEG)
            bias.append(masked if live is True else jnp.where(live, masked, NEG))
        k = ks_ref[pl.ds(pl.multiple_of(p * Q_BLOCK, Q_BLOCK), Q_BLOCK), :]
        return k, vst_ref[p], dist.astype(F32), bias

    def body(step, carry):
        first = 2 * step
        has_second = first + 1 < n_active
        p0 = todo_ref[i * n_tiles + first]
        p1 = todo_ref[i * n_tiles + jnp.where(has_second, first + 1, first)]
        k0, vt0, dist0, bias0 = tile_terms(p0, True)
        k1, vt1, dist1, bias1 = tile_terms(p1, has_second)
        k = jnp.concatenate([k0, k1], axis=0)
        vt = jnp.concatenate([vt0, vt1], axis=1)
        bias_all = jnp.concatenate(
            [jnp.concatenate([bias0[h % 2], bias1[h % 2]], axis=0) for h in range(n_heads)], axis=1)
        dist_all = jnp.concatenate([jnp.concatenate([dist0, dist1], axis=0)] * n_heads, axis=1)
        s = _nt(k, qm_ref[...]) - slopes * dist_all + bias_all
        m_old = m_ref[...]
        m_new = jnp.maximum(m_old, jnp.max(s, axis=0, keepdims=True))
        alpha = jnp.exp(m_old - m_new)
        pr = jnp.exp((s - m_new).astype(BF16))
        vt_ext = jnp.concatenate([vt, jnp.ones((SUM_ROWS, 2 * Q_BLOCK), BF16)], axis=0)
        pv = _dot(vt_ext, pr)
        l_ref[...] = alpha * l_ref[...] + pv[LANES:LANES + 1]
        own = jnp.concatenate(
            [pv[(h % 2) * HALF:(h % 2 + 1) * HALF, h * Q_BLOCK:(h + 1) * Q_BLOCK]
             for h in range(n_heads)], axis=1)
        acc_ref[...] = alpha * acc_ref[...] + own
        m_ref[...] = m_new
        return carry

    lax.fori_loop(0, lax.div(n_active + 1, 2), body, 0)

    gates = jax.nn.sigmoid(gc_ref[...])
    o_all = acc_ref[...] * (1.0 / jnp.maximum(l_ref[...], 1e-30))
    for pair in range(N_PAIR):
        h = pair * 2
        o_s = jnp.concatenate([o_all[:, h * Q_BLOCK:(h + 1) * Q_BLOCK],
                               o_all[:, (h + 1) * Q_BLOCK:(h + 2) * Q_BLOCK]], axis=0).T
        cols = slice(pair * LANES, (pair + 1) * LANES)

        def gate(branch):
            c = branch * 8 + pair * 2
            return jnp.where(lo, gates[:, c:c + 1], gates[:, c + 1:c + 2])

        y = gate(0) * oc_ref[:, cols] + gate(1) * o_s + gate(2) * ow_ref[:, cols]
        o_ref[:, cols] = y.astype(o_ref.dtype)


def _slope_row():
    row = np.concatenate([np.full((Q_BLOCK,), _head_slope(h // 2, h % 2), np.float32)
                          for h in range(2 * N_PAIR)])
    return jnp.asarray(row.reshape(1, -1))


def _visit_lists(cnt):
    nb = cnt.shape[0]
    tile = jnp.arange(nb, dtype=jnp.int32)
    picked = (cnt[:, 0, :].reshape(nb, nb, 2).sum(-1) > 0.5) & (tile[None, :] <= tile[:, None])
    slot = jnp.cumsum(picked.astype(jnp.int32), axis=1) - 1
    hit = picked[:, :, None] & (slot[:, :, None] == tile[None, None, :])
    todo = jnp.sum(jnp.where(hit, tile[None, :, None], 0), axis=1)
    return todo.reshape(-1), picked.sum(axis=1).astype(jnp.int32)


def nsa_selected(cnt, y_att, vt4, vt_index, sel_t, y_rest, o_c, o_w):
    s = y_att.shape[0]
    nb = s // Q_BLOCK
    n_sel = sel_t.shape[2]
    n_heads = 2 * N_PAIR
    todo, count = _visit_lists(cnt)
    kern = functools.partial(_nsa_sel_kernel, n_tiles=nb)
    wide = pl.BlockSpec((Q_BLOCK, N_PAIR * LANES), lambda i, *_: (i, 0))
    grid_spec = pltpu.PrefetchScalarGridSpec(
        num_scalar_prefetch=2,
        grid=(nb,),
        in_specs=[pl.BlockSpec((Q_BLOCK, N_PAIR * LANES), lambda i, *_: (i, COL_QC // N_PAIR)),
                  pl.BlockSpec((s, LANES), lambda i, *_: (0, COL_KS)),
                  pl.BlockSpec((None, nb, LANES, Q_BLOCK), lambda i, *_: (vt_index, 0, 0, 0)),
                  pl.BlockSpec((1, 2, n_sel, Q_BLOCK), lambda i, *_: (i, 0, 0, 0)),
                  pl.BlockSpec((1, n_heads * Q_BLOCK), lambda i, *_: (0, 0)),
                  pl.BlockSpec((Q_BLOCK, LANES), lambda i, *_: (i, N_GATES // LANES)),
                  wide, wide],
        out_specs=wide,
        scratch_shapes=[pltpu.VMEM((n_heads * Q_BLOCK, LANES), BF16),
                        pltpu.VMEM((1, n_heads * Q_BLOCK), F32),
                        pltpu.VMEM((1, n_heads * Q_BLOCK), F32),
                        pltpu.VMEM((HALF, n_heads * Q_BLOCK), F32)],
    )
    return pl.pallas_call(
        kern,
        grid_spec=grid_spec,
        out_shape=jax.ShapeDtypeStruct((s, N_PAIR * LANES), BF16),
        compiler_params=_params("arbitrary"),
        name="nsa_selected",
    )(todo, count, y_att, y_att, vt4, sel_t, _slope_row(), y_rest, o_c, o_w)


def _pair_cols(w):
    lead = w.shape[:-1]
    return w.reshape(*lead, 2, N_PAIR, HALF).swapaxes(-3, -2).reshape(*lead, 2 * N_PAIR * HALF)


def _pair_rows(w):
    return w.reshape(2, N_PAIR, HALF, w.shape[-1]).swapaxes(0, 1).reshape(2 * N_PAIR * HALF, w.shape[-1])


def _prep_w_in(w_in):
    qa = _pair_cols(w_in[:, 0:512])
    qc = _pair_cols(w_in[:, 2304:2816])
    cols = lambda a, b: w_in[:, a:b]
    w_att = jnp.concatenate(
        [qa, qc, cols(768, 2304),
         cols(512, 640), cols(2816, 3072), cols(3072, 3200), cols(3328, 3456),
         cols(640, 768), cols(3200, 3328), cols(3456, 3584)],
        axis=1).astype(BF16)
    gc = w_in[:, N_ATT:N_ATT + N_GC].reshape(-1, 2, N_PAIR, 3)
    gc = gc.transpose(0, 3, 2, 1).reshape(-1, N_GC)
    gc = jnp.pad(gc, ((0, 0), (0, IN_PROJ_TN - N_GC)))
    return jnp.concatenate([w_att, w_in[:, N_ATT + N_GC:].astype(BF16), gc.astype(BF16)], axis=1)


def _transpose_kernel(x_ref, o_ref):
    tile = o_ref.shape[-1]
    for c in range(o_ref.shape[0]):
        o_ref[c] = x_ref[c * tile:(c + 1) * tile, :].T


def transpose_tiles(y_att, col, width, tile):
    s = y_att.shape[0]
    rows = min(TRANSPOSE_ROWS, s)
    return pl.pallas_call(
        _transpose_kernel,
        grid=(width, s // rows),
        in_specs=[pl.BlockSpec((rows, LANES), lambda c, r: (r, col + c))],
        out_specs=pl.BlockSpec((None, rows // tile, LANES, tile), lambda c, r: (c, r, 0, 0)),
        out_shape=jax.ShapeDtypeStruct((width, s // tile, LANES, tile), y_att.dtype),
        compiler_params=_params("parallel", "parallel"),
        name="transpose_tiles",
    )(y_att)


def _chunk_rows(y_att, col):
    s = y_att.shape[0]
    t = y_att[:, col * LANES:(col + 1) * LANES].reshape(s // CMP_STRIDE, CMP_STRIDE, 2, HALF)
    return t.transpose(2, 0, 1, 3).reshape(2, s // CMP_STRIDE, CMP_STRIDE * HALF)


def _overlap_t(s):
    n_cmp = s // CMP_STRIDE
    n_sel = s // SEL_BLOCK
    cmp_lo = np.arange(n_cmp) * CMP_STRIDE
    cmp_end = cmp_lo + CMP_BLOCK - 1
    sel_lo = np.arange(n_sel) * SEL_BLOCK
    ov = ((cmp_lo[None, :] <= sel_lo[:, None] + SEL_BLOCK - 1)
          & (cmp_end[None, :] >= sel_lo[:, None]) & (np.arange(n_cmp)[None, :] < n_cmp - 1))
    return jnp.asarray(ov.astype(np.float32), dtype=BF16)


def nsa_attention(y_att, y_rest, vt4, pe_k, w1_k, w2_k, pe_v, w1_v, w2_v):
    s = y_att.shape[0]
    kc = compress_blocks(_chunk_rows(y_att, COL_KC), pe_k, w1_k, w2_k)
    vc = compress_blocks(_chunk_rows(y_att, COL_VC), pe_v, w1_v, w2_v)
    kc128 = kc.transpose(1, 0, 2).reshape(kc.shape[1], LANES)
    vct128 = vc.transpose(0, 2, 1).reshape(LANES, vc.shape[1])
    o_c, sel_t, cnt = nsa_compressed(y_att, kc128, vct128, _overlap_t(s))
    o_w = banded_attention(y_att, vt4, COL_VW - COL_VA, COL_QC, COL_KW, NSA_WINDOW, None, F32)
    return nsa_selected(cnt, y_att, vt4, COL_VS - COL_VA, sel_t, y_rest, o_c, o_w)


def mixer_layer(x, norm_mix, w_in, sinks, pe_k, w1_k, w2_k, pe_v, w1_v, w2_v,
                w_br_a, w_br_b, w_br_c, w_out):
    y_att, y_rest = in_projection(x, norm_mix, _prep_w_in(w_in), IN_PROJ_TM, IN_PROJ_TN)
    vt4 = transpose_tiles(y_att, COL_VA, 3, Q_BLOCK)
    y_a = banded_attention(y_att, vt4, 0, COL_QA, COL_KA, SWA_WINDOW, sinks, BF16)
    y_b = stick_breaking(y_att, transpose_tiles(y_att, COL_VB, SB_HEADS, SB_TILE))
    y_c = nsa_attention(y_att, y_rest, vt4, pe_k, w1_k, w2_k, pe_v, w1_v, w2_v)
    return merge_out_projection(y_a, y_b, y_c, _pair_rows(w_br_a).astype(BF16),
                                w_br_b.astype(BF16), _pair_rows(w_br_c).astype(BF16),
                                y_rest, w_out.astype(BF16), x, ROW_TM)


def ffn_layer(x, norm_ffn, w_gate, w_up, w_down, norm_after=None):
    u = ffn_up(x, norm_ffn, w_gate.astype(BF16), w_up.astype(BF16), IN_PROJ_TM, FFN_TN)
    return matmul_residual(u, w_down.astype(BF16), x, ROW_TM, norm_after)


def kernel(x, norm_mix, w_in, swa_sinks, cmp_pe_k, cmp_w1_k, cmp_w2_k, cmp_pe_v, cmp_w1_v,
           cmp_w2_v, w_branch_swa, w_branch_sb, w_branch_nsa, w_out, norm_ffn, w_gate, w_up,
           w_down, norm_final):
    b, s, d = x.shape
    outs = []
    for bi in range(b):
        xb = x[bi]
        depth = norm_mix.shape[0]
        assert depth >= 1
        for layer in range(depth):
            xb = mixer_layer(xb, norm_mix[layer], w_in[layer], swa_sinks[layer],
                             cmp_pe_k[layer], cmp_w1_k[layer], cmp_w2_k[layer],
                             cmp_pe_v[layer], cmp_w1_v[layer], cmp_w2_v[layer],
                             w_branch_swa[layer], w_branch_sb[layer], w_branch_nsa[layer],
                             w_out[layer])
            xb = ffn_layer(xb, norm_ffn[layer], w_gate[layer], w_up[layer], w_down[layer],
                           norm_final if layer == depth - 1 else None)
        outs.append(xb)
    return jnp.stack(outs, axis=0)
```

```python
import functools

import jax
import jax.numpy as jnp
import numpy as np
from jax import lax
from jax.experimental import pallas as pl
from jax.experimental.pallas import tpu as pltpu

F32 = jnp.float32
BF16 = jnp.bfloat16

D_MODEL = 2048
Q_BLOCK = 128
LANES = 128
HALF = 64
N_PAIR = 4
SWA_WINDOW = 128
NSA_WINDOW = 512
CMP_BLOCK = 32
CMP_STRIDE = 16
CMP_HIDDEN = 256
CMP_CHUNK = 256
SEL_BLOCK = 64
SEL_TOPK = 8
SB_HEADS = 4
SB_TILE = 256
D_FF = 5632
NEG = -1e30
SEL_FORCE = 1e9
SEL_TAKEN = -3e38
SUM_ROWS = 16
RMS_EPS = 1e-6
LOG2E = 1.4426950408889634
SB_DEAD_BITS = 150.0
VMEM_LIMIT = 56 * 1024 * 1024

N_ATT = 3584
N_GC = 24
N_GATES = 3 * D_MODEL
IN_PROJ_TM = 1024
IN_PROJ_TN = 512
FFN_TN = 512
ROW_TM = 256
COL_QA, COL_QC, COL_QB, COL_KB, COL_VB = 0, 4, 8, 12, 16
COL_KA, COL_KC, COL_VC, COL_KS, COL_KW = 20, 21, 22, 23, 24
COL_VA, COL_VS, COL_VW = 25, 26, 27
TRANSPOSE_ROWS = 2048


def _nt(a, b):
    return lax.dot_general(a, b, (((1,), (1,)), ((), ())), preferred_element_type=F32)


def _dot(a, b):
    return jnp.dot(a, b, preferred_element_type=F32)


def _params(*sem):
    return pltpu.CompilerParams(dimension_semantics=sem, vmem_limit_bytes=VMEM_LIMIT)


def _head_slope(pair, group):
    return 2.0 ** -(group * N_PAIR + pair + 1)


def _in_proj_kernel(x_ref, g_ref, w_ref, oa_ref, or_ref, h_ref, *, att_tiles):
    j = pl.program_id(1)

    @pl.when(j == 0)
    def _():
        x = x_ref[...]
        ms = jnp.mean(x * x, axis=-1, keepdims=True)
        h_ref[...] = (x * lax.rsqrt(ms + RMS_EPS) * g_ref[...]).astype(BF16)

    @pl.when(j < att_tiles)
    def _():
        oa_ref[...] = _dot(h_ref[...], w_ref[...]).astype(oa_ref.dtype)

    @pl.when(j >= att_tiles)
    def _():
        or_ref[...] = _dot(h_ref[...], w_ref[...])


def in_projection(x, g, w_all, tm, tn):
    s, d = x.shape
    n_tiles = w_all.shape[1] // tn
    att_tiles = N_ATT // tn
    rest_tiles = n_tiles - att_tiles
    kern = functools.partial(_in_proj_kernel, att_tiles=att_tiles)
    return pl.pallas_call(
        kern,
        grid=(s // tm, n_tiles),
        in_specs=[pl.BlockSpec((tm, d), lambda i, j: (i, 0)),
                  pl.BlockSpec((1, d), lambda i, j: (0, 0)),
                  pl.BlockSpec((d, tn), lambda i, j: (0, j))],
        out_specs=[pl.BlockSpec((tm, tn), lambda i, j: (i, jnp.minimum(j, att_tiles - 1))),
                   pl.BlockSpec((tm, tn), lambda i, j: (i, jnp.maximum(j - att_tiles, 0)))],
        out_shape=[jax.ShapeDtypeStruct((s, N_ATT), BF16),
                   jax.ShapeDtypeStruct((s, rest_tiles * tn), F32)],
        scratch_shapes=[pltpu.VMEM((tm, d), BF16)],
        compiler_params=_params("parallel", "arbitrary"),
        name="in_projection",
    )(x, g.reshape(1, d), w_all)


def _mm_res_kernel(a_ref, w_ref, r_ref, *rest, final_norm):
    y = r_ref[...] + _dot(a_ref[...], w_ref[...])
    if final_norm:
        g_ref, o_ref = rest
        ms = jnp.mean(y * y, axis=-1, keepdims=True)
        y = y * lax.rsqrt(ms + RMS_EPS) * g_ref[...]
    else:
        o_ref, = rest
    o_ref[...] = y


def matmul_residual(a, w, res, tm, norm_g=None):
    s, k = a.shape
    n = w.shape[1]
    in_specs = [pl.BlockSpec((tm, k), lambda i: (i, 0)),
                pl.BlockSpec((k, n), lambda i: (0, 0), pipeline_mode=pl.Buffered(1)),
                pl.BlockSpec((tm, n), lambda i: (i, 0))]
    args = [a, w, res]
    if norm_g is not None:
        in_specs.append(pl.BlockSpec((1, n), lambda i: (0, 0)))
        args.append(norm_g.reshape(1, n))
    return pl.pallas_call(
        functools.partial(_mm_res_kernel, final_norm=norm_g is not None),
        grid=(s // tm,),
        in_specs=in_specs,
        out_specs=pl.BlockSpec((tm, n), lambda i: (i, 0)),
        out_shape=jax.ShapeDtypeStruct((s, n), F32),
        compiler_params=_params("parallel"),
        name="matmul_residual",
    )(*args)


def _merge_out_kernel(ya_ref, yb_ref, yc_ref, wa_ref, wb_ref, wc_ref,
                      ga_ref, gb_ref, gc_ref, wo_ref, x_ref, o_ref):
    m = jax.nn.sigmoid(ga_ref[...]) * _dot(ya_ref[...], wa_ref[...])
    m = m + jax.nn.sigmoid(gb_ref[...]) * _dot(yb_ref[...], wb_ref[...])
    m = m + jax.nn.sigmoid(gc_ref[...]) * _dot(yc_ref[...], wc_ref[...])
    o_ref[...] = x_ref[...] + _dot(m.astype(BF16), wo_ref[...])


def merge_out_projection(ya, yb, yc, wa, wb, wc, y_rest, w_out, x, tm):
    s, k = ya.shape
    n = wa.shape[1]
    resident = pl.Buffered(1)
    y_spec = pl.BlockSpec((tm, k), lambda i: (i, 0))
    w_spec = pl.BlockSpec((k, n), lambda i: (0, 0), pipeline_mode=resident)

    def gate_spec(br):
        return pl.BlockSpec((tm, n), lambda i: (i, br))

    return pl.pallas_call(
        _merge_out_kernel,
        grid=(s // tm,),
        in_specs=[y_spec, y_spec, y_spec, w_spec, w_spec, w_spec,
                  gate_spec(0), gate_spec(1), gate_spec(2),
                  pl.BlockSpec((n, n), lambda i: (0, 0), pipeline_mode=resident),
                  pl.BlockSpec((tm, n), lambda i: (i, 0))],
        out_specs=pl.BlockSpec((tm, n), lambda i: (i, 0)),
        out_shape=jax.ShapeDtypeStruct((s, n), F32),
        compiler_params=_params("parallel"),
        name="merge_out_projection",
    )(ya, yb, yc, wa, wb, wc, y_rest, y_rest, y_rest, w_out, x)


def _ffn_up_kernel(x_ref, g_ref, wg_ref, wu_ref, o_ref, h_ref):
    @pl.when(pl.program_id(1) == 0)
    def _():
        x = x_ref[...]
        ms = jnp.mean(x * x, axis=-1, keepdims=True)
        h_ref[...] = (x * lax.rsqrt(ms + RMS_EPS) * g_ref[...]).astype(BF16)

    h = h_ref[...]
    o_ref[...] = (jax.nn.silu(_dot(h, wg_ref[...])) * _dot(h, wu_ref[...])).astype(o_ref.dtype)


def ffn_up(x, g, wg, wu, tm, tn):
    s, d = x.shape
    n = wg.shape[1]
    w_spec = pl.BlockSpec((d, tn), lambda i, j: (0, j))
    return pl.pallas_call(
        _ffn_up_kernel,
        grid=(s // tm, n // tn),
        in_specs=[pl.BlockSpec((tm, d), lambda i, j: (i, 0)),
                  pl.BlockSpec((1, d), lambda i, j: (0, 0)),
                  w_spec, w_spec],
        out_specs=pl.BlockSpec((tm, tn), lambda i, j: (i, j)),
        out_shape=jax.ShapeDtypeStruct((s, n), BF16),
        scratch_shapes=[pltpu.VMEM((tm, d), BF16)],
        compiler_params=_params("parallel", "arbitrary"),
        name="ffn_up",
    )(x, g.reshape(1, d), wg, wu)


def _half_masks():
    lane = lax.broadcasted_iota(jnp.int32, (Q_BLOCK, LANES), 1)
    return lane < HALF


def _scaled_q(q):
    return q * jnp.asarray(HALF ** -0.5, q.dtype)


def _masked_q_rows(q_ref):
    lo = _half_masks()
    rows = []
    for pair in range(N_PAIR):
        q128 = _scaled_q(q_ref[:, pair * LANES:(pair + 1) * LANES])
        for group in range(2):
            rows.append(jnp.where(lo if group == 0 else jnp.logical_not(lo), q128,
                                  jnp.zeros_like(q128)))
    return jnp.concatenate(rows, axis=0)


def _pair_tile(pv, pair):
    top = lax.broadcasted_iota(jnp.int32, (LANES, Q_BLOCK), 0) < HALF
    h = 2 * pair
    return jnp.where(top, pv[:, h * Q_BLOCK:(h + 1) * Q_BLOCK], pv[:, (h + 1) * Q_BLOCK:(h + 2) * Q_BLOCK])


def _banded_kernel(*refs, n_prev, window, has_sink, transposed_out):
    nk = n_prev + 1
    n_heads = 2 * N_PAIR
    if has_sink:
        sink_ref, refs = refs[0], refs[1:]
    slope_ref, q_ref, k_ref, vt_ref, o_ref = refs
    i = pl.program_id(0)
    span = nk * Q_BLOCK
    first = jnp.maximum(i - n_prev, 0)
    k_all = k_ref[pl.ds(pl.multiple_of(first * Q_BLOCK, Q_BLOCK), span), :]
    vt_all = jnp.concatenate([vt_ref[first + d] for d in range(nk)], axis=1)
    key_pos = first * Q_BLOCK + lax.broadcasted_iota(jnp.int32, (span, Q_BLOCK), 0)
    t = i * Q_BLOCK + lax.broadcasted_iota(jnp.int32, (span, Q_BLOCK), 1)
    dist = t - key_pos
    mask = (dist >= 0) & (dist < window)
    bias = jnp.where(mask, 0.0, NEG)
    distf = dist.astype(F32)
    bias_all = jnp.concatenate([bias] * n_heads, axis=1)
    dist_all = jnp.concatenate([distf] * n_heads, axis=1)
    s = _nt(k_all, _masked_q_rows(q_ref)) - slope_ref[...] * dist_all + bias_all
    m = jnp.max(s, axis=0, keepdims=True)
    if has_sink:
        sink = jnp.concatenate(
            [jnp.full((1, Q_BLOCK), sink_ref[(h % 2) * N_PAIR + h // 2], F32) for h in range(n_heads)],
            axis=1)
        m = jnp.maximum(m, sink)
        p = jnp.exp(s - m)
        denom = jnp.sum(p, axis=0, keepdims=True) + jnp.exp(sink - m)
    else:
        p = jnp.exp(s - m)
        denom = jnp.maximum(jnp.sum(p, axis=0, keepdims=True), 1e-30)
    pv = _dot(vt_all, p.astype(BF16)) * (1.0 / denom)
    for pair in range(N_PAIR):
        tile = _pair_tile(pv, pair)
        if transposed_out:
            o_ref[0, pair * LANES:(pair + 1) * LANES, :] = tile
        else:
            o_ref[:, pair * LANES:(pair + 1) * LANES] = tile.T.astype(o_ref.dtype)


def banded_attention(y_att, vt4, vt_index, col_q, col_k, window, sinks, transposed_out):
    s = y_att.shape[0]
    nb = s // Q_BLOCK
    n_prev = -(-window // Q_BLOCK)
    has_sink = sinks is not None
    in_specs = []
    args = []
    if has_sink:
        in_specs.append(pl.BlockSpec(memory_space=pltpu.SMEM))
        args.append(sinks)
    in_specs.append(pl.BlockSpec((1, 2 * N_PAIR * Q_BLOCK), lambda i: (0, 0)))
    args.append(_slope_row())
    in_specs.append(pl.BlockSpec((Q_BLOCK, N_PAIR * LANES), lambda i: (i, col_q // N_PAIR)))
    args.append(y_att)
    assert nb > n_prev
    in_specs.append(pl.BlockSpec((s, LANES), lambda i: (0, col_k)))
    args.append(y_att)
    in_specs.append(pl.BlockSpec((None, nb, LANES, Q_BLOCK), lambda i: (vt_index, 0, 0, 0)))
    args.append(vt4)
    kern = functools.partial(_banded_kernel, n_prev=n_prev, window=window, has_sink=has_sink,
                             transposed_out=transposed_out)
    if transposed_out:
        out_spec = pl.BlockSpec((1, N_PAIR * LANES, Q_BLOCK), lambda i: (i, 0, 0))
        out_shape = jax.ShapeDtypeStruct((nb, N_PAIR * LANES, Q_BLOCK), F32)
    else:
        out_spec = pl.BlockSpec((Q_BLOCK, N_PAIR * LANES), lambda i: (i, 0))
        out_shape = jax.ShapeDtypeStruct((s, N_PAIR * LANES), BF16)
    return pl.pallas_call(
        kern,
        grid=(nb,),
        in_specs=in_specs,
        out_specs=out_spec,
        out_shape=out_shape,
        compiler_params=_params("parallel"),
        name="banded_attention_w%d" % window,
    )(*args)


def _sb_kernel(q_ref, k_ref, vt_ref, o_ref, ls_ref, acc_ref, *, scale, heads):
    i = pl.program_id(1)
    row = lax.broadcasted_iota(jnp.int32, (SB_TILE, SB_TILE), 0)
    col = lax.broadcasted_iota(jnp.int32, (SB_TILE, SB_TILE), 1)
    later = jnp.where(col > row, 1.0, 0.0).astype(BF16)
    key_i = lax.broadcasted_iota(jnp.int32, (SB_TILE, heads * SB_TILE), 0)
    query_i = lax.broadcasted_iota(jnp.int32, (SB_TILE, heads * SB_TILE), 1) & (SB_TILE - 1)
    seen = key_i < query_i

    def visit(kj, diagonal):
        zz = []
        for h in range(heads):
            q = q_ref[:, h * LANES:(h + 1) * LANES]
            k = k_ref[pl.ds(pl.multiple_of(kj * SB_TILE, SB_TILE), SB_TILE),
                      h * LANES:(h + 1) * LANES]
            zz.append(_nt(k, q))
        zz = jnp.concatenate(zz, axis=1) * (scale * LOG2E)
        sp_raw = jnp.maximum(zz, 0.0) + jnp.log(1.0 + jnp.exp2(-jnp.abs(zz))) * LOG2E
        sp = jnp.where(seen, sp_raw, 0.0) if diagonal else sp_raw
        gone = ls_ref[...]
        a = jnp.exp2((zz - sp_raw) - _dot(later, sp.astype(BF16)) - gone)
        if diagonal:
            a = jnp.where(seen, a, 0.0)
        a = a.astype(BF16)
        for h in range(heads):
            acc_ref[h] += _dot(vt_ref[h, kj], a[:, h * SB_TILE:(h + 1) * SB_TILE])
        gone = gone + jnp.sum(sp, axis=0, keepdims=True)
        ls_ref[...] = gone
        return jnp.min(gone)

    ls_ref[...] = jnp.zeros(ls_ref.shape, F32)
    acc_ref[...] = jnp.zeros(acc_ref.shape, F32)
    alive0 = visit(i, True)

    def cond(carry):
        return (carry[0] < i) & (carry[1] < SB_DEAD_BITS)

    def body(carry):
        return carry[0] + 1, visit(i - 1 - carry[0], False)

    lax.while_loop(cond, body, (jnp.int32(0), alive0))
    for h in range(heads):
        o_ref[:, h * LANES:(h + 1) * LANES] = acc_ref[h].T.astype(o_ref.dtype)


def stick_breaking(y_att, vt4, heads=SB_HEADS):
    s = y_att.shape[0]
    nt = s // SB_TILE
    kern = functools.partial(_sb_kernel, scale=LANES ** -0.5, heads=heads)
    resident = pl.Buffered(1)
    return pl.pallas_call(
        kern,
        grid=(SB_HEADS // heads, nt),
        in_specs=[pl.BlockSpec((SB_TILE, heads * LANES), lambda g, i: (i, COL_QB // heads + g)),
                  pl.BlockSpec((s, heads * LANES), lambda g, i: (0, COL_KB // heads + g),
                               pipeline_mode=resident),
                  pl.BlockSpec((heads, nt, LANES, SB_TILE), lambda g, i: (g, 0, 0, 0),
                               pipeline_mode=resident)],
        out_specs=pl.BlockSpec((SB_TILE, heads * LANES), lambda g, i: (i, g)),
        out_shape=jax.ShapeDtypeStruct((s, SB_HEADS * LANES), BF16),
        scratch_shapes=[pltpu.VMEM((1, heads * SB_TILE), F32),
                        pltpu.VMEM((heads, LANES, SB_TILE), F32)],
        compiler_params=_params("parallel", "arbitrary"),
        name="stick_breaking",
    )(y_att, y_att, vt4)


def _compress_kernel(t_ref, pe_ref, w1_ref, w2_ref, o_ref):
    t = t_ref[...]
    half = w1_ref.shape[0] // 2
    w1 = w1_ref[...]
    a = _dot(t, w1[:half])
    b = _dot(t, w1[half:])
    bias = _dot(pe_ref[...], w1)[0:1]
    n = t.shape[0]
    pre = a + pltpu.roll(b, n - 1, 0) + bias
    hid = jax.nn.gelu(pre)
    o_ref[...] = _dot(hid.astype(BF16), w2_ref[...]).astype(o_ref.dtype)


def compress_blocks(t_flat, pe, w1, w2):
    g, n, k = t_flat.shape
    pe_rows = jnp.zeros((8, 2 * k), BF16).at[0].set(pe.reshape(-1).astype(BF16))
    return pl.pallas_call(
        _compress_kernel,
        grid=(g,),
        in_specs=[pl.BlockSpec((None, n, k), lambda gi: (gi, 0, 0)),
                  pl.BlockSpec((8, 2 * k), lambda gi: (0, 0)),
                  pl.BlockSpec((2 * k, CMP_HIDDEN), lambda gi: (0, 0)),
                  pl.BlockSpec((CMP_HIDDEN, HALF), lambda gi: (0, 0))],
        out_specs=pl.BlockSpec((None, n, HALF), lambda gi: (gi, 0, 0)),
        out_shape=jax.ShapeDtypeStruct((g, n, HALF), BF16),
        compiler_params=_params("parallel"),
        name="compress_blocks",
    )(t_flat, pe_rows, w1.astype(BF16), w2.astype(BF16))


def _nsa_cmp_kernel(q_ref, kc_ref, vct_ref, ov_ref, slope_ref, oc_ref, sel_ref, cnt_ref):
    i = pl.program_id(0)
    n_all = kc_ref.shape[0]
    chunk = min(CMP_CHUNK, n_all)
    needed = lax.div(8 * i + 7 + (chunk - 1), chunk)
    body = functools.partial(_nsa_cmp_body, q_ref, kc_ref, vct_ref, ov_ref, slope_ref, oc_ref,
                             sel_ref, cnt_ref)
    for c in range(1, n_all // chunk + 1):
        pl.when(needed == c)(functools.partial(body, n_cmp=c * chunk))


def _nsa_cmp_body(q_ref, kc_ref, vct_ref, ov_ref, slope_ref, oc_ref, sel_ref, cnt_ref, *, n_cmp):
    i = pl.program_id(0)
    kc = kc_ref[:n_cmp, :]
    vct = vct_ref[:, :n_cmp]
    t = i * Q_BLOCK + lax.broadcasted_iota(jnp.int32, (n_cmp, Q_BLOCK), 1)
    cmp_end = lax.broadcasted_iota(jnp.int32, (n_cmp, Q_BLOCK), 0) * CMP_STRIDE + (CMP_BLOCK - 1)
    dist = t - cmp_end
    bias = jnp.where(dist >= 0, 0.0, NEG)
    seen = jnp.where(t[0:1] >= CMP_BLOCK - 1, 1.0, 0.0)
    distf = dist.astype(F32)
    n_heads = 2 * N_PAIR
    bias_all = jnp.concatenate([bias] * n_heads, axis=1)
    dist_all = jnp.concatenate([distf] * n_heads, axis=1)
    seen_all = jnp.concatenate([seen] * n_heads, axis=1)
    s = _nt(kc, _masked_q_rows(q_ref)) - slope_ref[...] * dist_all + bias_all
    m = jnp.max(s, axis=0, keepdims=True)
    p = jnp.exp(s - m)
    p = p * (seen_all / jnp.maximum(jnp.sum(p, axis=0, keepdims=True) * seen_all, 1e-30))
    pv = _dot(vct, p.astype(BF16))
    for pair in range(N_PAIR):
        oc_ref[0, pair * LANES:(pair + 1) * LANES, :] = _pair_tile(pv, pair)
    psum = []
    for group in range(2):
        heads = [p[:, (2 * pair + group) * Q_BLOCK:(2 * pair + group + 1) * Q_BLOCK]
                 for pair in range(N_PAIR)]
        psum.append(((heads[0] + heads[1]) + heads[2]) + heads[3])

    n_sel_all = ov_ref.shape[0]
    n_sel = min(n_sel_all, n_cmp // (SEL_BLOCK // CMP_STRIDE))
    def all_rows(x):
        if n_sel == n_sel_all:
            return x
        return jnp.concatenate([x, jnp.zeros((n_sel_all - n_sel, Q_BLOCK), F32)], axis=0)

    ov = ov_ref[:n_sel, :n_cmp]
    tq = i * Q_BLOCK + lax.broadcasted_iota(jnp.int32, (n_sel, Q_BLOCK), 1)
    jblk = lax.broadcasted_iota(jnp.int32, (n_sel, Q_BLOCK), 0)
    jblk_f = jblk.astype(F32)
    jt = tq // SEL_BLOCK
    valid = jblk * SEL_BLOCK <= tq
    forced = (jblk == 0) | (jblk == jt) | (jblk == jt - 1)
    total = jnp.zeros((n_sel, Q_BLOCK), F32)
    for group in range(2):
        hi = psum[group].astype(BF16)
        lo_part = (psum[group] - hi.astype(F32)).astype(BF16)
        imp = _dot(ov, hi) + _dot(ov, lo_part)
        score = jnp.where(valid, jnp.where(forced, SEL_FORCE, imp), -SEL_FORCE)
        sel = jnp.zeros((n_sel, Q_BLOCK), F32)
        for _ in range(min(SEL_TOPK, n_sel)):
            best = jnp.max(score, axis=0, keepdims=True)
            first = jnp.min(jnp.where(score == best, jblk_f, float(n_sel)), axis=0, keepdims=True)
            hit = jblk_f == first
            sel = jnp.where(hit, 1.0, sel)
            score = jnp.where(hit, SEL_TAKEN, score)
        sel_ref[0, group] = all_rows(sel)
        total = total + sel
    cnt_ref[0] = _nt(jnp.ones((8, Q_BLOCK), BF16), all_rows(total).astype(BF16))


def nsa_compressed(y_att, kc128, vct128, overlap_t):
    s = y_att.shape[0]
    nb = s // Q_BLOCK
    n_cmp = kc128.shape[0]
    n_sel = overlap_t.shape[0]
    return pl.pallas_call(
        _nsa_cmp_kernel,
        grid=(nb,),
        in_specs=[pl.BlockSpec((Q_BLOCK, N_PAIR * LANES), lambda i: (i, COL_QC // N_PAIR)),
                  pl.BlockSpec((n_cmp, LANES), lambda i: (0, 0)),
                  pl.BlockSpec((LANES, n_cmp), lambda i: (0, 0)),
                  pl.BlockSpec((n_sel, n_cmp), lambda i: (0, 0)),
                  pl.BlockSpec((1, 2 * N_PAIR * Q_BLOCK), lambda i: (0, 0))],
        out_specs=[pl.BlockSpec((1, N_PAIR * LANES, Q_BLOCK), lambda i: (i, 0, 0)),
                   pl.BlockSpec((1, 2, n_sel, Q_BLOCK), lambda i: (i, 0, 0, 0)),
                   pl.BlockSpec((1, 8, n_sel), lambda i: (i, 0, 0))],
        out_shape=[jax.ShapeDtypeStruct((nb, N_PAIR * LANES, Q_BLOCK), F32),
                   jax.ShapeDtypeStruct((nb, 2, n_sel, Q_BLOCK), F32),
                   jax.ShapeDtypeStruct((nb, 8, n_sel), F32)],
        compiler_params=_params("parallel"),
        name="nsa_compressed",
    )(y_att, kc128, vct128, overlap_t, _slope_row())


def _nsa_sel_kernel(todo_ref, count_ref, q_ref, ks_ref, vst_ref, sel_ref, slope_ref, gc_ref, oc_ref,
                    ow_ref, o_ref, qm_ref, m_ref, l_ref, acc_ref, *, n_tiles):
    i = pl.program_id(0)
    n_heads = 2 * N_PAIR
    n_active = count_ref[i]
    lo = _half_masks()
    for pair in range(N_PAIR):
        q128 = _scaled_q(q_ref[:, pair * LANES:(pair + 1) * LANES])
        for group in range(2):
            h = pair * 2 + group
            qm_ref[h * Q_BLOCK:(h + 1) * Q_BLOCK, :] = jnp.where(
                lo if group == 0 else jnp.logical_not(lo), q128, jnp.zeros_like(q128))
    m_ref[...] = jnp.full(m_ref.shape, 0.1 * NEG, F32)
    l_ref[...] = jnp.zeros(l_ref.shape, F32)
    acc_ref[...] = jnp.zeros(acc_ref.shape, F32)
    top = lax.broadcasted_iota(jnp.int32, (Q_BLOCK, Q_BLOCK), 0) < HALF
    slopes = slope_ref[...]

    def tile_terms(p, live):
        key_pos = p * Q_BLOCK + lax.broadcasted_iota(jnp.int32, (Q_BLOCK, Q_BLOCK), 0)
        t = i * Q_BLOCK + lax.broadcasted_iota(jnp.int32, (Q_BLOCK, Q_BLOCK), 1)
        dist = t - key_pos
        bias = []
        for group in range(2):
            r0 = sel_ref[0, group, pl.ds(2 * p, 1), :]
            r1 = sel_ref[0, group, pl.ds(2 * p + 1, 1), :]
            picked = jnp.where(top, r0, r1)
            masked = jnp.where((picked > 0.5) & (dist >= 0), 0.0, NEG)
            bias.append(masked if live is True else jnp.where(live, masked, NEG))
        k = ks_ref[pl.ds(pl.multiple_of(p * Q_BLOCK, Q_BLOCK), Q_BLOCK), :]
        return k, vst_ref[p], dist.astype(F32), bias

    def body(step, carry):
        first = 2 * step
        has_second = first + 1 < n_active
        p0 = todo_ref[i * n_tiles + first]
        p1 = todo_ref[i * n_tiles + jnp.where(has_second, first + 1, first)]
        k0, vt0, dist0, bias0 = tile_terms(p0, True)
        k1, vt1, dist1, bias1 = tile_terms(p1, has_second)
        k = jnp.concatenate([k0, k1], axis=0)
        vt = jnp.concatenate([vt0, vt1], axis=1)
        bias_all = jnp.concatenate(
            [jnp.concatenate([bias0[h % 2], bias1[h % 2]], axis=0) for h in range(n_heads)], axis=1)
        dist_all = jnp.concatenate([jnp.concatenate([dist0, dist1], axis=0)] * n_heads, axis=1)
        s = _nt(k, qm_ref[...]) - slopes * dist_all + bias_all
        m_old = m_ref[...]
        m_new = jnp.maximum(m_old, jnp.max(s, axis=0, keepdims=True))
        alpha = jnp.exp(m_old - m_new)
        pr = jnp.exp((s - m_new).astype(BF16))
        vt_ext = jnp.concatenate([vt, jnp.ones((SUM_ROWS, 2 * Q_BLOCK), BF16)], axis=0)
        pv = _dot(vt_ext, pr)
        l_ref[...] = alpha * l_ref[...] + pv[LANES:LANES + 1]
        own = jnp.concatenate(
            [pv[(h % 2) * HALF:(h % 2 + 1) * HALF, h * Q_BLOCK:(h + 1) * Q_BLOCK]
             for h in range(n_heads)], axis=1)
        acc_ref[...] = alpha * acc_ref[...] + own
        m_ref[...] = m_new
        return carry

    lax.fori_loop(0, lax.div(n_active + 1, 2), body, 0)

    gates = jax.nn.sigmoid(gc_ref[...])
    o_all = acc_ref[...] * (1.0 / jnp.maximum(l_ref[...], 1e-30))
    gates_t = gates.T
    for pair in range(N_PAIR):
        h = pair * 2
        o_s = jnp.concatenate([o_all[:, h * Q_BLOCK:(h + 1) * Q_BLOCK],
                               o_all[:, (h + 1) * Q_BLOCK:(h + 2) * Q_BLOCK]], axis=0)
        rows = slice(pair * LANES, (pair + 1) * LANES)

        def gate(branch):
            c = branch * 8 + pair * 2
            return jnp.where(top, gates_t[c:c + 1, :], gates_t[c + 1:c + 2, :])

        y = gate(0) * oc_ref[0, rows, :] + gate(1) * o_s + gate(2) * ow_ref[0, rows, :]
        o_ref[:, rows] = y.T.astype(o_ref.dtype)


def _slope_row():
    row = np.concatenate([np.full((Q_BLOCK,), _head_slope(h // 2, h % 2), np.float32)
                          for h in range(2 * N_PAIR)])
    return jnp.asarray(row.reshape(1, -1))


def _visit_lists(cnt):
    nb = cnt.shape[0]
    tile = jnp.arange(nb, dtype=jnp.int32)
    picked = (cnt[:, 0, :].reshape(nb, nb, 2).sum(-1) > 0.5) & (tile[None, :] <= tile[:, None])
    slot = jnp.cumsum(picked.astype(jnp.int32), axis=1) - 1
    hit = picked[:, :, None] & (slot[:, :, None] == tile[None, None, :])
    todo = jnp.sum(jnp.where(hit, tile[None, :, None], 0), axis=1)
    return todo.reshape(-1), picked.sum(axis=1).astype(jnp.int32)


def nsa_selected(cnt, y_att, vt4, vt_index, sel_t, y_rest, o_c, o_w):
    s = y_att.shape[0]
    nb = s // Q_BLOCK
    n_sel = sel_t.shape[2]
    n_heads = 2 * N_PAIR
    todo, count = _visit_lists(cnt)
    kern = functools.partial(_nsa_sel_kernel, n_tiles=nb)
    wide = pl.BlockSpec((Q_BLOCK, N_PAIR * LANES), lambda i, *_: (i, 0))
    tiles_t = pl.BlockSpec((1, N_PAIR * LANES, Q_BLOCK), lambda i, *_: (i, 0, 0))
    grid_spec = pltpu.PrefetchScalarGridSpec(
        num_scalar_prefetch=2,
        grid=(nb,),
        in_specs=[pl.BlockSpec((Q_BLOCK, N_PAIR * LANES), lambda i, *_: (i, COL_QC // N_PAIR)),
                  pl.BlockSpec((s, LANES), lambda i, *_: (0, COL_KS)),
                  pl.BlockSpec((None, nb, LANES, Q_BLOCK), lambda i, *_: (vt_index, 0, 0, 0)),
                  pl.BlockSpec((1, 2, n_sel, Q_BLOCK), lambda i, *_: (i, 0, 0, 0)),
                  pl.BlockSpec((1, n_heads * Q_BLOCK), lambda i, *_: (0, 0)),
                  pl.BlockSpec((Q_BLOCK, LANES), lambda i, *_: (i, N_GATES // LANES)),
                  tiles_t, tiles_t],
        out_specs=wide,
        scratch_shapes=[pltpu.VMEM((n_heads * Q_BLOCK, LANES), BF16),
                        pltpu.VMEM((1, n_heads * Q_BLOCK), F32),
                        pltpu.VMEM((1, n_heads * Q_BLOCK), F32),
                        pltpu.VMEM((HALF, n_heads * Q_BLOCK), F32)],
    )
    return pl.pallas_call(
        kern,
        grid_spec=grid_spec,
        out_shape=jax.ShapeDtypeStruct((s, N_PAIR * LANES), BF16),
        compiler_params=_params("arbitrary"),
        name="nsa_selected",
    )(todo, count, y_att, y_att, vt4, sel_t, _slope_row(), y_rest, o_c, o_w)


def _pair_cols(w):
    lead = w.shape[:-1]
    return w.reshape(*lead, 2, N_PAIR, HALF).swapaxes(-3, -2).reshape(*lead, 2 * N_PAIR * HALF)


def _pair_rows(w):
    return w.reshape(2, N_PAIR, HALF, w.shape[-1]).swapaxes(0, 1).reshape(2 * N_PAIR * HALF, w.shape[-1])


def _prep_w_in(w_in):
    qa = _pair_cols(w_in[:, 0:512])
    qc = _pair_cols(w_in[:, 2304:2816])
    cols = lambda a, b: w_in[:, a:b]
    w_att = jnp.concatenate(
        [qa, qc, cols(768, 2304),
         cols(512, 640), cols(2816, 3072), cols(3072, 3200), cols(3328, 3456),
         cols(640, 768), cols(3200, 3328), cols(3456, 3584)],
        axis=1).astype(BF16)
    gc = w_in[:, N_ATT:N_ATT + N_GC].reshape(-1, 2, N_PAIR, 3)
    gc = gc.transpose(0, 3, 2, 1).reshape(-1, N_GC)
    gc = jnp.pad(gc, ((0, 0), (0, IN_PROJ_TN - N_GC)))
    return jnp.concatenate([w_att, w_in[:, N_ATT + N_GC:].astype(BF16), gc.astype(BF16)], axis=1)


def _transpose_kernel(x_ref, o_ref):
    tile = o_ref.shape[-1]
    for c in range(o_ref.shape[0]):
        o_ref[c] = x_ref[c * tile:(c + 1) * tile, :].T


def transpose_tiles(y_att, col, width, tile):
    s = y_att.shape[0]
    rows = min(TRANSPOSE_ROWS, s)
    return pl.pallas_call(
        _transpose_kernel,
        grid=(width, s // rows),
        in_specs=[pl.BlockSpec((rows, LANES), lambda c, r: (r, col + c))],
        out_specs=pl.BlockSpec((None, rows // tile, LANES, tile), lambda c, r: (c, r, 0, 0)),
        out_shape=jax.ShapeDtypeStruct((width, s // tile, LANES, tile), y_att.dtype),
        compiler_params=_params("parallel", "parallel"),
        name="transpose_tiles",
    )(y_att)


def _chunk_rows(y_att, col):
    s = y_att.shape[0]
    t = y_att[:, col * LANES:(col + 1) * LANES].reshape(s // CMP_STRIDE, CMP_STRIDE, 2, HALF)
    return t.transpose(2, 0, 1, 3).reshape(2, s // CMP_STRIDE, CMP_STRIDE * HALF)


def _overlap_t(s):
    n_cmp = s // CMP_STRIDE
    n_sel = s // SEL_BLOCK
    cmp_lo = np.arange(n_cmp) * CMP_STRIDE
    cmp_end = cmp_lo + CMP_BLOCK - 1
    sel_lo = np.arange(n_sel) * SEL_BLOCK
    ov = ((cmp_lo[None, :] <= sel_lo[:, None] + SEL_BLOCK - 1)
          & (cmp_end[None, :] >= sel_lo[:, None]) & (np.arange(n_cmp)[None, :] < n_cmp - 1))
    return jnp.asarray(ov.astype(np.float32), dtype=BF16)


def nsa_attention(y_att, y_rest, vt4, pe_k, w1_k, w2_k, pe_v, w1_v, w2_v):
    s = y_att.shape[0]
    kc = compress_blocks(_chunk_rows(y_att, COL_KC), pe_k, w1_k, w2_k)
    vc = compress_blocks(_chunk_rows(y_att, COL_VC), pe_v, w1_v, w2_v)
    kc128 = kc.transpose(1, 0, 2).reshape(kc.shape[1], LANES)
    vct128 = vc.transpose(0, 2, 1).reshape(LANES, vc.shape[1])
    o_c, sel_t, cnt = nsa_compressed(y_att, kc128, vct128, _overlap_t(s))
    o_w = banded_attention(y_att, vt4, COL_VW - COL_VA, COL_QC, COL_KW, NSA_WINDOW, None, True)
    return nsa_selected(cnt, y_att, vt4, COL_VS - COL_VA, sel_t, y_rest, o_c, o_w)


def mixer_layer(x, norm_mix, w_in, sinks, pe_k, w1_k, w2_k, pe_v, w1_v, w2_v,
                w_br_a, w_br_b, w_br_c, w_out):
    y_att, y_rest = in_projection(x, norm_mix, _prep_w_in(w_in), IN_PROJ_TM, IN_PROJ_TN)
    vt4 = transpose_tiles(y_att, COL_VA, 3, Q_BLOCK)
    y_a = banded_attention(y_att, vt4, 0, COL_QA, COL_KA, SWA_WINDOW, sinks, False)
    y_b = stick_breaking(y_att, transpose_tiles(y_att, COL_VB, SB_HEADS, SB_TILE))
    y_c = nsa_attention(y_att, y_rest, vt4, pe_k, w1_k, w2_k, pe_v, w1_v, w2_v)
    return merge_out_projection(y_a, y_b, y_c, _pair_rows(w_br_a).astype(BF16),
                                w_br_b.astype(BF16), _pair_rows(w_br_c).astype(BF16),
                                y_rest, w_out.astype(BF16), x, ROW_TM)


def ffn_layer(x, norm_ffn, w_gate, w_up, w_down, norm_after=None):
    u = ffn_up(x, norm_ffn, w_gate.astype(BF16), w_up.astype(BF16), IN_PROJ_TM, FFN_TN)
    return matmul_residual(u, w_down.astype(BF16), x, ROW_TM, norm_after)


def kernel(x, norm_mix, w_in, swa_sinks, cmp_pe_k, cmp_w1_k, cmp_w2_k, cmp_pe_v, cmp_w1_v,
           cmp_w2_v, w_branch_swa, w_branch_sb, w_branch_nsa, w_out, norm_ffn, w_gate, w_up,
           w_down, norm_final):
    b, s, d = x.shape
    outs = []
    for bi in range(b):
        xb = x[bi]
        depth = norm_mix.shape[0]
        assert depth >= 1
        for layer in range(depth):
            xb = mixer_layer(xb, norm_mix[layer], w_in[layer], swa_sinks[layer],
                             cmp_pe_k[layer], cmp_w1_k[layer], cmp_w2_k[layer],
                             cmp_pe_v[layer], cmp_w1_v[layer], cmp_w2_v[layer],
                             w_branch_swa[layer], w_branch_sb[layer], w_branch_nsa[layer],
                             w_out[layer])
            xb = ffn_layer(xb, norm_ffn[layer], w_gate[layer], w_up[layer], w_down[layer],
                           norm_final if layer == depth - 1 else None)
        outs.append(xb)
    return jnp.stack(outs, axis=0)
```
